```python
import jax
import jax.numpy as jnp
from jax import lax
import numpy as np

D_MODEL = 4096
BATCH = 1
SEQ = 8192
DEPTH = 4

CHUNK = 64
EPS = 1e-6
ROPE_THETA = 500000.0
ROPE_FRACTION = 4

ML_HEADS = 4
ML_DV = 384
ML_DQK = ML_DV // 2
ML_CONV = 4
ML_GATE_CAP = 15.0
DSA_HEADS = 8
DSA_DH = 128
DSA_Q_RANK = 384
IDX_HEADS = 8
IDX_DH = 64
TOPK_MAX = 256
Q_BLOCK = 128
HG_HEADS = 12
HG_DK = 128
HG_DV = 128
D_FF = 4 * D_MODEL

ML_QK_W = ML_HEADS * ML_DQK
ML_V_W = ML_HEADS * ML_DV
DSA_W = DSA_HEADS * DSA_DH
IDX_W = IDX_HEADS * IDX_DH
HG_K_W = HG_HEADS * HG_DK
HG_V_W = HG_HEADS * HG_DV
D_MIX = ML_V_W + DSA_W + HG_V_W

IN_SPLITS = (
    ML_QK_W, ML_QK_W, ML_V_W, ML_HEADS, ML_HEADS, ML_V_W,
    DSA_Q_RANK, DSA_W, DSA_W, IDX_DH, IDX_HEADS,
    HG_K_W, HG_K_W, HG_V_W, HG_V_W,
)
D_IN = sum(IN_SPLITS)

kernel_name = 'hybrid_mlstm_dsa_hgrn2_block'


def rmsnorm(x, g):
    xf = x.astype(jnp.float32)
    y = xf * lax.rsqrt(jnp.mean(xf * xf, axis=-1, keepdims=True) + EPS)
    return (y * g.astype(jnp.float32)).astype(x.dtype)


def split_columns(u):
    out, start = [], 0
    for width in IN_SPLITS:
        out.append(u[..., start:start + width])
        start += width
    return out


def partial_rope(u, positions):
    d = u.shape[-1]
    rot = d // ROPE_FRACTION
    half = rot // 2
    inv_freq = jnp.power(ROPE_THETA, -jnp.arange(half, dtype=jnp.float32) * (2.0 / rot))
    ang = positions.astype(jnp.float32)[..., None] * inv_freq
    ang = ang.reshape(ang.shape[:2] + (1,) * (u.ndim - 3) + (half,))
    cos, sin = jnp.cos(ang), jnp.sin(ang)
    u1 = u[..., :half].astype(jnp.float32)
    u2 = u[..., half:rot].astype(jnp.float32)
    rotated = jnp.concatenate([u1 * cos - u2 * sin, u2 * cos + u1 * sin], axis=-1)
    return jnp.concatenate([rotated.astype(u.dtype), u[..., rot:]], axis=-1)


def causal_conv(u, w):
    width = w.shape[0]
    t = u.shape[1]
    up = jnp.pad(u, ((0, 0), (width - 1, 0), (0, 0)))
    out = up[:, 0:t] * w[0]
    for j in range(1, width):
        out = out + up[:, j:j + t] * w[j]
    return out


def to_chunks(u):
    b, t, h = u.shape[:3]
    u = u.reshape((b, t // CHUNK, CHUNK, h) + u.shape[3:])
    return jnp.moveaxis(u, (1, 2, 3), (0, 3, 2))


def from_chunks(u):
    u = jnp.moveaxis(u, (0, 3, 2), (1, 2, 3))
    b, nc, l, h = u.shape[:4]
    return u.reshape((b, nc * l, h) + u.shape[4:])


def soft_cap(u):
    return ML_GATE_CAP * jnp.tanh(u / ML_GATE_CAP)


def mlstm_mixer(q, k, v, i_pre, f_pre, o_pre, conv_w, gate_b, norm_g):
    bsz, t, _ = v.shape
    dtype = v.dtype
    f32 = jnp.float32
    qk = jax.nn.silu(causal_conv(jnp.concatenate([q, k], axis=-1), conv_w))
    q = qk[..., :ML_QK_W].reshape(bsz, t, ML_HEADS, ML_DQK).astype(f32)
    k = qk[..., ML_QK_W:].reshape(bsz, t, ML_HEADS, ML_DQK).astype(f32) * (ML_DQK ** -0.5)
    v = v.reshape(bsz, t, ML_HEADS, ML_DV).astype(f32)
    ig = soft_cap(i_pre.astype(f32) + gate_b[0].astype(f32))
    lf = jax.nn.log_sigmoid(soft_cap(f_pre.astype(f32) + gate_b[1].astype(f32)))
    causal = jnp.tril(jnp.ones((CHUNK, CHUNK), dtype=bool))

    def step(carry, inp):
        c_st, n_st, m_st = carry
        qc, kc, vc, igc, lfc = inp
        b = jnp.cumsum(lfc, axis=-1)
        dmat = jnp.where(causal, b[..., :, None] - b[..., None, :] + igc[..., None, :], -jnp.inf)
        inter = b + m_st[..., None]
        m_t = jnp.maximum(inter, jnp.max(dmat, axis=-1))
        s = jnp.einsum('bhtd,bhsd->bhts', qc, kc) * jnp.exp(dmat - m_t[..., None])
        w_inter = jnp.exp(inter - m_t)
        num = jnp.einsum('bhts,bhsv->bhtv', s, vc) + w_inter[..., None] * jnp.einsum('bhtd,bhdv->bhtv', qc, c_st)
        den = jnp.sum(s, axis=-1) + w_inter * jnp.einsum('bhtd,bhd->bht', qc, n_st)
        h = num / jnp.maximum(jnp.abs(den), jnp.exp(-m_t))[..., None]
        b_last = b[..., -1]
        g = b_last[..., None] - b + igc
        m_new = jnp.maximum(b_last + m_st, jnp.max(g, axis=-1))
        w_s = jnp.exp(g - m_new[..., None])
        w_old = jnp.exp(b_last + m_st - m_new)
        c_new = w_old[..., None, None] * c_st + jnp.einsum('bhs,bhsd,bhsv->bhdv', w_s, kc, vc)
        n_new = w_old[..., None] * n_st + jnp.einsum('bhs,bhsd->bhd', w_s, kc)
        return (c_new, n_new, m_new), h

    init = (jnp.zeros((bsz, ML_HEADS, ML_DQK, ML_DV), f32),
            jnp.zeros((bsz, ML_HEADS, ML_DQK), f32),
            jnp.zeros((bsz, ML_HEADS), f32))
    _, h = lax.scan(step, init, (to_chunks(q), to_chunks(k), to_chunks(v), to_chunks(ig), to_chunks(lf)))
    h = rmsnorm(from_chunks(h), norm_g.reshape(ML_HEADS, ML_DV)).reshape(bsz, t, ML_V_W)
    return (jax.nn.sigmoid(o_pre.astype(f32)) * h).astype(dtype)


def dsa_mixer(c_q, k, v, k_idx, w_idx, positions, q_norm_g, w_uq):
    bsz, t, _ = k.shape
    f32 = jnp.float32
    q_all = rmsnorm(c_q, q_norm_g) @ w_uq
    q = partial_rope(q_all[..., :DSA_W].reshape(bsz, t, DSA_HEADS, DSA_DH), positions)
    q_idx = partial_rope(q_all[..., DSA_W:].reshape(bsz, t, IDX_HEADS, IDX_DH), positions)
    k = partial_rope(k.reshape(bsz, t, DSA_HEADS, DSA_DH), positions)
    v = v.reshape(bsz, t, DSA_HEADS, DSA_DH)
    k_idx = partial_rope(k_idx, positions)
    topk = min(TOPK_MAX, t // 4)
    key_chunk = jnp.arange(t) // CHUNK

    def block(start):
        qb = lax.dynamic_slice_in_dim(q, start, Q_BLOCK, axis=1)
        qib = lax.dynamic_slice_in_dim(q_idx, start, Q_BLOCK, axis=1)
        wb = lax.dynamic_slice_in_dim(w_idx, start, Q_BLOCK, axis=1).astype(f32) * (IDX_HEADS ** -0.5)
        logits = jnp.einsum('bqhd,bsd->bqhs', qib, k_idx, preferred_element_type=f32) * (IDX_DH ** -0.5)
        score = jnp.einsum('bqh,bqhs->bqs', wb, jax.nn.relu(logits))
        q_chunk = (start + jnp.arange(Q_BLOCK)) // CHUNK
        allowed = key_chunk[None, :] <= q_chunk[:, None]
        score = jnp.where(allowed[None], score, -jnp.inf)
        top_val, top_idx = lax.top_k(score, topk)
        valid = top_val > -jnp.inf
        k_sel = jax.vmap(lambda kk, ii: kk[ii])(k, top_idx)
        v_sel = jax.vmap(lambda vv, ii: vv[ii])(v, top_idx)
        s = jnp.einsum('bqhd,bqkhd->bhqk', qb, k_sel, preferred_element_type=f32) * (DSA_DH ** -0.5)
        s = jnp.where(valid[:, None], s, -jnp.inf)
        p = jax.nn.softmax(s, axis=-1)
        return jnp.einsum('bhqk,bqkhd->bqhd', p.astype(v.dtype), v_sel)

    outs = lax.map(block, jnp.arange(t // Q_BLOCK) * Q_BLOCK)
    return jnp.moveaxis(outs, 0, 1).reshape(bsz, t, DSA_W)


def hgrn2_mixer(q, f_pre, i, g, lower_bound, norm_g):
    bsz, t, _ = q.shape
    dtype = q.dtype
    f32 = jnp.float32
    q = jax.nn.silu(q).reshape(bsz, t, HG_HEADS, HG_DK).astype(f32)
    fp = f_pre.reshape(bsz, t, HG_HEADS, HG_DK).astype(f32)
    lb = lower_bound.astype(f32).reshape(HG_HEADS, HG_DK)
    log_f = jnp.logaddexp(jnp.log(lb), jnp.log1p(-lb) + jax.nn.log_sigmoid(fp))
    k = (1.0 - lb) * jax.nn.sigmoid(-fp)
    v = i.reshape(bsz, t, HG_HEADS, HG_DV).astype(f32)
    causal = jnp.tril(jnp.ones((CHUNK, CHUNK), dtype=bool))

    def step(s_st, inp):
        qc, kc, vc, lfc = inp
        b = jnp.cumsum(lfc, axis=2)
        diff = b[:, :, :, None, :] - b[:, :, None, :, :]
        decay = jnp.exp(jnp.where(causal[:, :, None], diff, -jnp.inf))
        attn = jnp.einsum('bhtk,bhsk,bhtsk->bhts', qc, kc, decay)
        o = jnp.einsum('bhts,bhsv->bhtv', attn, vc) + jnp.einsum('bhtk,bhkv->bhtv', qc * jnp.exp(b), s_st)
        b_last = b[:, :, -1]
        k_w = kc * jnp.exp(b_last[:, :, None] - b)
        s_new = jnp.exp(b_last)[..., None] * s_st + jnp.einsum('bhsk,bhsv->bhkv', k_w, vc)
        return s_new, o

    init = jnp.zeros((bsz, HG_HEADS, HG_DK, HG_DV), f32)
    _, o = lax.scan(step, init, (to_chunks(q), to_chunks(k), to_chunks(v), to_chunks(log_f)))
    o = rmsnorm(from_chunks(o), norm_g.reshape(HG_HEADS, HG_DV)).reshape(bsz, t, HG_V_W)
    return (o * jax.nn.silu(g.astype(f32))).astype(dtype)


def setup_inputs(seed: int = 0) -> dict:
    key = jax.random.key(seed)
    ks = jax.random.split(key, 16)
    f32 = jnp.float32
    nrm = jax.random.normal
    x = nrm(ks[0], (BATCH, SEQ, D_MODEL), f32)
    offset = jax.random.randint(ks[1], (BATCH, 1), 0, 4096, dtype=jnp.int32)
    positions = offset + jnp.arange(SEQ, dtype=jnp.int32)[None, :]
    ln_mix_g = 1.0 + 0.02 * nrm(ks[2], (DEPTH, D_MODEL), f32)
    w_in = nrm(ks[3], (DEPTH, D_MODEL, D_IN), f32) * (D_MODEL ** -0.5)
    ml_conv_w = nrm(ks[4], (DEPTH, ML_CONV, 2 * ML_QK_W), f32) * (ML_CONV ** -0.5)
    ml_gate_b = 0.1 * nrm(ks[5], (DEPTH, 2, ML_HEADS), f32) + jnp.array([0.0, 3.0], f32)[None, :, None]
    ml_norm_g = 1.0 + 0.02 * nrm(ks[6], (DEPTH, ML_V_W), f32)
    dsa_q_norm_g = 1.0 + 0.02 * nrm(ks[7], (DEPTH, DSA_Q_RANK), f32)
    dsa_w_uq = nrm(ks[8], (DEPTH, DSA_Q_RANK, DSA_W + IDX_W), f32) * (DSA_Q_RANK ** -0.5)
    hg_lb_logits = 0.5 * nrm(ks[9], (DEPTH, HG_K_W), f32)
    hg_norm_g = 1.0 + 0.02 * nrm(ks[10], (DEPTH, HG_V_W), f32)
    w_out = nrm(ks[11], (DEPTH, D_MIX, D_MODEL), f32) * (D_MIX ** -0.5)
    ln_mlp_g = 1.0 + 0.02 * nrm(ks[12], (DEPTH, D_MODEL), f32)
    w_up = nrm(ks[13], (DEPTH, D_MODEL, D_FF), f32) * (D_MODEL ** -0.5)
    w_down = nrm(ks[14], (DEPTH, D_FF, D_MODEL), f32) * (D_FF ** -0.5)
    ln_final_g = 1.0 + 0.02 * nrm(ks[15], (D_MODEL,), f32)
    return {'x': x, 'positions': positions, 'ln_mix_g': ln_mix_g, 'w_in': w_in,
            'ml_conv_w': ml_conv_w, 'ml_gate_b': ml_gate_b, 'ml_norm_g': ml_norm_g,
            'dsa_q_norm_g': dsa_q_norm_g, 'dsa_w_uq': dsa_w_uq,
            'hg_lb_logits': hg_lb_logits, 'hg_norm_g': hg_norm_g, 'w_out': w_out,
            'ln_mlp_g': ln_mlp_g, 'w_up': w_up, 'w_down': w_down, 'ln_final_g': ln_final_g}


def reference(x, positions, ln_mix_g, w_in, ml_conv_w, ml_gate_b, ml_norm_g,
              dsa_q_norm_g, dsa_w_uq, hg_lb_logits, hg_norm_g, w_out,
              ln_mlp_g, w_up, w_down, ln_final_g):
    lb_cum = jnp.cumsum(jax.nn.softmax(hg_lb_logits.astype(jnp.float32), axis=0), axis=0)
    lower_bounds = lb_cum - lb_cum[:1]
    for layer in range(DEPTH):
        h = rmsnorm(x, ln_mix_g[layer])
        (ml_q, ml_k, ml_v, ml_i, ml_f, ml_o,
         dsa_cq, dsa_k, dsa_v, idx_k, idx_w,
         hg_q, hg_f, hg_i, hg_g) = split_columns(h @ w_in[layer])
        y_a = mlstm_mixer(ml_q, ml_k, ml_v, ml_i, ml_f, ml_o,
                          ml_conv_w[layer], ml_gate_b[layer], ml_norm_g[layer])
        y_b = dsa_mixer(dsa_cq, dsa_k, dsa_v, idx_k, idx_w, positions,
                        dsa_q_norm_g[layer], dsa_w_uq[layer])
        y_c = hgrn2_mixer(hg_q, hg_f, hg_i, hg_g, lower_bounds[layer], hg_norm_g[layer])
        x = x + jnp.concatenate([y_a, y_b, y_c], axis=-1) @ w_out[layer]
        h = rmsnorm(x, ln_mlp_g[layer])
        x = x + jnp.square(jax.nn.relu(h @ w_up[layer])) @ w_down[layer]
    return rmsnorm(x, ln_final_g)
```

```python
import functools

import jax
import jax.numpy as jnp
from jax import lax
from jax.experimental import pallas as pl
from jax.experimental.pallas import tpu as pltpu

F32 = jnp.float32
BF16 = jnp.bfloat16
MXU_DTYPE = jnp.bfloat16

D_MODEL = 4096
DEPTH = 4
CHUNK = 64
EPS = 1e-6
ROPE_THETA = 500000.0
ROPE_FRACTION = 4
ML_HEADS = 4
ML_DV = 384
ML_DQK = ML_DV // 2
ML_DQK_PAD = 256
ML_CONV = 4
ML_GATE_CAP = 15.0
DSA_HEADS = 8
DSA_DH = 128
DSA_Q_RANK = 384
IDX_HEADS = 8
IDX_DH = 64
IDX_SLOT = 128
TOPK_MAX = 256
HG_HEADS = 12
HG_DK = 128
HG_DV = 128
D_FF = 4 * D_MODEL

ML_QK_W = ML_HEADS * ML_DQK
ML_V_W = ML_HEADS * ML_DV
DSA_W = DSA_HEADS * DSA_DH
IDX_W = IDX_HEADS * IDX_DH
HG_K_W = HG_HEADS * HG_DK
HG_V_W = HG_HEADS * HG_DV
D_MIX = ML_V_W + DSA_W + HG_V_W
IN_SPLITS = (ML_QK_W, ML_QK_W, ML_V_W, ML_HEADS, ML_HEADS, ML_V_W,
             DSA_Q_RANK, DSA_W, DSA_W, IDX_DH, IDX_HEADS,
             HG_K_W, HG_K_W, HG_V_W, HG_V_W)

LANES = 128
VMEM_LIMIT = 56 * 1024 * 1024

OFF_ML_Q = 0
OFF_ML_K = 1024
OFF_DSA_K = 2048
OFF_ML_V = 3072
OFF_ML_O = 4608
OFF_HG_Q = 6144
OFF_HG_F = 7680
OFF_HG_I = 9216
OFF_HG_G = 10752
OFF_DSA_V = 12288
OFF_GATES = 13312
OFF_DSA_CQ = 13440
OFF_IDX = 13824
D_IN_PAD = 14336

INT_MIN = -2 ** 31
KEY_NEG_INF = -2139095041
NEG_BIG = -1e30


def _mxu(a):
    return a.astype(MXU_DTYPE)


def _dot(a, b):
    return jnp.dot(_mxu(a), _mxu(b), preferred_element_type=F32)


def _dot_nt(a, b):
    return lax.dot_general(_mxu(a), _mxu(b), (((1,), (1,)), ((), ())), preferred_element_type=F32)


def _dot_tn(a, b):
    return lax.dot_general(_mxu(a), _mxu(b), (((0,), (0,)), ((), ())), preferred_element_type=F32)


def _dot_f32(a, b):
    return jnp.dot(a, b, precision=lax.Precision.HIGHEST, preferred_element_type=F32)


def _sigmoid(x):
    return 1.0 / (1.0 + jnp.exp(-x))


def _log_sigmoid(x):
    return jnp.minimum(x, 0.0) - jnp.log1p(jnp.exp(-jnp.abs(x)))


def _params(*sem):
    return pltpu.CompilerParams(dimension_semantics=sem, vmem_limit_bytes=VMEM_LIMIT)


def _norm_matmul_kernel(x_ref, g_ref, w_ref, o_ref, h_ref, *, act):
    @pl.when(pl.program_id(1) == 0)
    def _():
        x = x_ref[...]
        ms = jnp.mean(x * x, axis=-1, keepdims=True)
        h_ref[...] = (x * lax.rsqrt(ms + EPS) * g_ref[...]).astype(h_ref.dtype)

    y = jnp.dot(h_ref[...], w_ref[...], preferred_element_type=F32)
    if act:
        y = jnp.square(jnp.maximum(y, 0.0))
    o_ref[...] = y.astype(o_ref.dtype)


def _norm_matmul(x, g, w, *, tm, tn, act, out_dtype):
    m, k = x.shape
    n = w.shape[1]
    return pl.pallas_call(
        functools.partial(_norm_matmul_kernel, act=act),
        grid=(m // tm, n // tn),
        in_specs=[pl.BlockSpec((tm, k), lambda i, j: (i, 0)),
                  pl.BlockSpec((1, k), lambda i, j: (0, 0)),
                  pl.BlockSpec((k, tn), lambda i, j: (0, j))],
        out_specs=pl.BlockSpec((tm, tn), lambda i, j: (i, j)),
        out_shape=jax.ShapeDtypeStruct((m, n), out_dtype),
        scratch_shapes=[pltpu.VMEM((tm, k), MXU_DTYPE)],
        compiler_params=_params("parallel", "arbitrary"),
        name="norm_matmul",
    )(x, g.reshape(1, k), w)


def _matmul_res_kernel(a_ref, w_ref, r_ref, o_ref):
    part = jnp.dot(a_ref[...], w_ref[...], preferred_element_type=F32)

    @pl.when(pl.program_id(2) == 0)
    def _():
        o_ref[...] = r_ref[...] + part

    @pl.when(pl.program_id(2) != 0)
    def _():
        o_ref[...] += part


def _matmul_res(a, w, res, *, tm, tn, tk):
    m, k = a.shape
    n = w.shape[1]
    return pl.pallas_call(
        _matmul_res_kernel,
        grid=(m // tm, n // tn, k // tk),
        in_specs=[pl.BlockSpec((tm, tk), lambda i, j, kk: (i, kk)),
                  pl.BlockSpec((tk, tn), lambda i, j, kk: (kk, j)),
                  pl.BlockSpec((tm, tn), lambda i, j, kk: (i, j))],
        out_specs=pl.BlockSpec((tm, tn), lambda i, j, kk: (i, j)),
        out_shape=jax.ShapeDtypeStruct((m, n), F32),
        compiler_params=_params("parallel", "parallel", "arbitrary"),
        name="matmul_res",
    )(a, w, res)


def _mix_out_kernel(ya_ref, yb_ref, yc_ref, w_ref, r_ref, o_ref):
    acc = r_ref[...]
    acc += jnp.dot(ya_ref[...], w_ref[0:ML_V_W, :], preferred_element_type=F32)
    acc += jnp.dot(yb_ref[...], w_ref[ML_V_W:ML_V_W + DSA_W, :], preferred_element_type=F32)
    acc += jnp.dot(yc_ref[...], w_ref[ML_V_W + DSA_W:D_MIX, :], preferred_element_type=F32)
    o_ref[...] = acc


def _mix_out(ya, yb, yc, w, res, *, tm, tn):
    m = ya.shape[0]
    n = w.shape[1]
    return pl.pallas_call(
        _mix_out_kernel,
        grid=(m // tm, n // tn),
        in_specs=[pl.BlockSpec((tm, ML_V_W), lambda i, j: (i, 0)),
                  pl.BlockSpec((tm, DSA_W), lambda i, j: (i, 0)),
                  pl.BlockSpec((tm, HG_V_W), lambda i, j: (i, 0)),
                  pl.BlockSpec((D_MIX, tn), lambda i, j: (0, j)),
                  pl.BlockSpec((tm, tn), lambda i, j: (i, j))],
        out_specs=pl.BlockSpec((tm, tn), lambda i, j: (i, j)),
        out_shape=jax.ShapeDtypeStruct((m, n), F32),
        compiler_params=_params("parallel", "parallel"),
        name="mix_out",
    )(ya, yb, yc, w, res)


def _final_norm_kernel(x_ref, g_ref, o_ref):
    x = x_ref[...]
    ms = jnp.mean(x * x, axis=-1, keepdims=True)
    o_ref[...] = x * lax.rsqrt(ms + EPS) * g_ref[...]


def _final_norm(x, g, *, tm):
    m, k = x.shape
    return pl.pallas_call(
        _final_norm_kernel,
        grid=(m // tm,),
        in_specs=[pl.BlockSpec((tm, k), lambda i: (i, 0)),
                  pl.BlockSpec((1, k), lambda i: (0, 0))],
        out_specs=pl.BlockSpec((tm, k), lambda i: (i, 0)),
        out_shape=jax.ShapeDtypeStruct((m, k), F32),
        compiler_params=_params("parallel"),
        name="final_norm",
    )(x, g.reshape(1, k))


def _mlstm_kernel(q_ref, k_ref, v_ref, o_ref, gt_ref, cw_ref, gb_ref, ng_ref, out_ref,
                  xbuf, c_ref, *, rows):
    head = pl.program_id(0)

    @pl.when(pl.program_id(1) == 0)
    def _():
        xbuf[0:8, :] = jnp.zeros((8, 2 * ML_DQK_PAD), F32)
        c_ref[...] = jnp.zeros_like(c_ref)

    xbuf[8:8 + rows, 0:ML_DQK_PAD] = q_ref[...]
    xbuf[8:8 + rows, ML_DQK_PAD:] = k_ref[...]
    cw = cw_ref[...]
    acc = xbuf[8:8 + rows, :] * cw[ML_CONV - 1:ML_CONV, :]
    for j in range(1, ML_CONV):
        acc += xbuf[8 - j:8 - j + rows, :] * cw[ML_CONV - 1 - j:ML_CONV - j, :]
    xbuf[0:8, :] = xbuf[rows:rows + 8, :]
    qk = acc * _sigmoid(acc)
    q = qk[:, 0:ML_DQK_PAD]
    k = qk[:, ML_DQK_PAD:] * (ML_DQK ** -0.5)

    capped = ML_GATE_CAP * jnp.tanh((gt_ref[...] + gb_ref[...]) * (1.0 / ML_GATE_CAP))
    lsig = _log_sigmoid(capped)
    r_i = lax.broadcasted_iota(jnp.int32, (rows, rows), 0)
    c_i = lax.broadcasted_iota(jnp.int32, (rows, rows), 1)
    causal = c_i <= r_i
    b_all = _dot_f32(jnp.where(causal, 1.0, 0.0), lsig)
    lane = lax.broadcasted_iota(jnp.int32, (rows, LANES), 1)
    ig_col = jnp.sum(jnp.where(lane == head, capped, 0.0), axis=1, keepdims=True)
    b_col = jnp.sum(jnp.where(lane == head + ML_HEADS, b_all, 0.0), axis=1, keepdims=True)
    sub = lax.broadcasted_iota(jnp.int32, (LANES, rows), 0)
    ig_row = jnp.sum(jnp.where(sub == head, capped.T, 0.0), axis=0, keepdims=True)
    b_row = jnp.sum(jnp.where(sub == head + ML_HEADS, b_all.T, 0.0), axis=0, keepdims=True)

    dmat = jnp.exp(jnp.where(causal, b_col - b_row + ig_row, -jnp.inf))
    s = _dot_nt(q, k) * dmat
    one_col = jnp.where(lane == 0, 1.0, 0.0)
    v_ext = jnp.concatenate([v_ref[...], one_col], axis=1)
    c_old = c_ref[...]
    num_ext = _dot(s, v_ext) + jnp.exp(b_col) * _dot(q, c_old)
    num = num_ext[:, 0:ML_DV]
    den = num_ext[:, ML_DV:ML_DV + 1]
    hh = num / jnp.maximum(jnp.abs(den), 1.0)

    b_last = b_col[rows - 1:rows, :]
    w_s = jnp.exp(b_last - b_col + ig_col)
    c_ref[...] = jnp.exp(b_last) * c_old + _dot_tn(k, w_s * v_ext)

    ms = jnp.mean(hh * hh, axis=-1, keepdims=True)
    y = hh * lax.rsqrt(ms + EPS) * ng_ref[...]
    out_ref[...] = (_sigmoid(o_ref[...]) * y).astype(out_ref.dtype)


def _mlstm(u, conv_w, gate_b, norm_g, *, rows):
    t = u.shape[0]
    qb, kb = OFF_ML_Q // ML_DQK_PAD, OFF_ML_K // ML_DQK_PAD
    vb, ob = OFF_ML_V // ML_DV, OFF_ML_O // ML_DV
    gb = OFF_GATES // LANES
    return pl.pallas_call(
        functools.partial(_mlstm_kernel, rows=rows),
        grid=(ML_HEADS, t // rows),
        in_specs=[pl.BlockSpec((rows, ML_DQK_PAD), lambda h, c: (c, qb + h)),
                  pl.BlockSpec((rows, ML_DQK_PAD), lambda h, c: (c, kb + h)),
                  pl.BlockSpec((rows, ML_DV), lambda h, c: (c, vb + h)),
                  pl.BlockSpec((rows, ML_DV), lambda h, c: (c, ob + h)),
                  pl.BlockSpec((rows, LANES), lambda h, c: (c, gb)),
                  pl.BlockSpec((None, ML_CONV, 2 * ML_DQK_PAD), lambda h, c: (h, 0, 0)),
                  pl.BlockSpec((1, LANES), lambda h, c: (0, 0)),
                  pl.BlockSpec((1, ML_DV), lambda h, c: (0, h))],
        out_specs=pl.BlockSpec((rows, ML_DV), lambda h, c: (c, h)),
        out_shape=jax.ShapeDtypeStruct((t, ML_V_W), MXU_DTYPE),
        scratch_shapes=[pltpu.VMEM((rows + 8, 2 * ML_DQK_PAD), F32),
                        pltpu.VMEM((ML_DQK_PAD, ML_DV + LANES), F32)],
        compiler_params=_params("parallel", "arbitrary"),
        name="mlstm",
    )(u, u, u, u, u, conv_w, gate_b, norm_g.reshape(1, ML_V_W))


HG_SUB = 16


def _hgrn2_kernel(q_ref, f_ref, i_ref, g_ref, lb_ref, ng_ref, out_ref, st_ref, o_scr, *, rows):
    @pl.when(pl.program_id(1) == 0)
    def _():
        st_ref[...] = jnp.zeros_like(st_ref)

    log_lb = lb_ref[0:1, :]
    log1m_lb = lb_ref[1:2, :]
    one_m_lb = lb_ref[2:3, :]
    r_i = lax.broadcasted_iota(jnp.int32, (HG_SUB, HG_SUB), 0)
    c_i = lax.broadcasted_iota(jnp.int32, (HG_SUB, HG_SUB), 1)
    tril = jnp.where(c_i <= r_i, 1.0, 0.0)
    s_idx = lax.broadcasted_iota(jnp.int32, (HG_SUB, 1), 0)

    def body(r, carry):
        sl = pl.ds(pl.multiple_of(r * HG_SUB, HG_SUB), HG_SUB)
        qp = q_ref[sl, :]
        q = qp * _sigmoid(qp)
        fp = f_ref[sl, :]
        v = i_ref[sl, :]
        gp = g_ref[sl, :]
        x2 = log1m_lb + _log_sigmoid(fp)
        log_f = jnp.maximum(log_lb, x2) + jnp.log1p(jnp.exp(-jnp.abs(log_lb - x2)))
        k = one_m_lb * _sigmoid(-fp)
        b = _dot_f32(tril, log_f)
        b_last = b[HG_SUB - 1:HG_SUB, :]

        for t in range(HG_SUB):
            n = 8 if t < 8 else HG_SUB
            diff = b[t:t + 1, :] - b[0:n, :]
            dec = jnp.exp(jnp.where(s_idx[0:n] <= t, diff, -jnp.inf))
            a_t = jnp.sum(q[t:t + 1, :] * k[0:n, :] * dec, axis=-1, keepdims=True)
            o_scr[t:t + 1, :] = jnp.sum(a_t * v[0:n, :], axis=0, keepdims=True)

        st = st_ref[...]
        o = o_scr[...] + _dot_nt(q * jnp.exp(b), st)
        st_ref[...] = st * jnp.exp(b_last) + _dot_tn(v, k * jnp.exp(b_last - b))

        ms = jnp.mean(o * o, axis=-1, keepdims=True)
        y = o * lax.rsqrt(ms + EPS) * ng_ref[...]
        out_ref[sl, :] = (y * (gp * _sigmoid(gp))).astype(out_ref.dtype)
        return carry

    lax.fori_loop(0, rows // HG_SUB, body, 0)


def _hgrn2(u, lb_tab, norm_g, *, rows):
    t = u.shape[0]
    qb, fb = OFF_HG_Q // HG_DK, OFF_HG_F // HG_DK
    ib, gb = OFF_HG_I // HG_DV, OFF_HG_G // HG_DV
    return pl.pallas_call(
        functools.partial(_hgrn2_kernel, rows=rows),
        grid=(HG_HEADS, t // rows),
        in_specs=[pl.BlockSpec((rows, HG_DK), lambda h, c: (c, qb + h)),
                  pl.BlockSpec((rows, HG_DK), lambda h, c: (c, fb + h)),
                  pl.BlockSpec((rows, HG_DV), lambda h, c: (c, ib + h)),
                  pl.BlockSpec((rows, HG_DV), lambda h, c: (c, gb + h)),
                  pl.BlockSpec((3, HG_DK), lambda h, c: (0, h)),
                  pl.BlockSpec((1, HG_DV), lambda h, c: (0, h))],
        out_specs=pl.BlockSpec((rows, HG_DV), lambda h, c: (c, h)),
        out_shape=jax.ShapeDtypeStruct((t, HG_V_W), MXU_DTYPE),
        scratch_shapes=[pltpu.VMEM((HG_DV, HG_DK), F32),
                        pltpu.VMEM((HG_SUB, HG_DV), F32)],
        compiler_params=_params("parallel", "arbitrary"),
        name="hgrn2",
    )(u, u, u, u, lb_tab, norm_g.reshape(1, HG_V_W))


def _rope_slab(u, cc, sa, sb, half):
    return u * cc + pltpu.roll(u, LANES - half, 1) * sa + pltpu.roll(u, half, 1) * sb


def _rope_coeffs(ang, half):
    lane = lax.broadcasted_iota(jnp.int32, ang.shape, 1)
    cos, sin = jnp.cos(ang), jnp.sin(ang)
    cc = jnp.where(lane < 2 * half, cos, 1.0)
    sa = jnp.where(lane < half, -sin, 0.0)
    sb = jnp.where((lane >= half) & (lane < 2 * half), sin, 0.0)
    return cc, sa, sb


def _dsa_prep_kernel(cq_ref, k_ref, v_ref, idx_ref, pos_ref, g_ref, w_ref, fr_ref,
                     q_out, qi_out, k_out, v_out, ki_out, wi_out):
    pos = pos_ref[...].astype(F32)
    cq = cq_ref[...]
    ms = jnp.mean(cq * cq, axis=-1, keepdims=True)
    hq = cq * lax.rsqrt(ms + EPS) * g_ref[...]
    q_all = _dot(hq, w_ref[...])

    half_a = DSA_DH // ROPE_FRACTION // 2
    half_i = IDX_DH // ROPE_FRACTION // 2
    ca = _rope_coeffs(pos * fr_ref[0:1, :], half_a)
    ci = _rope_coeffs(pos * fr_ref[1:2, :], half_i)

    kk = k_ref[...]
    for h in range(DSA_HEADS):
        sl = slice(h * DSA_DH, (h + 1) * DSA_DH)
        q_out[:, sl] = _rope_slab(q_all[:, sl], *ca, half_a).astype(q_out.dtype)
        k_out[:, sl] = _rope_slab(kk[:, sl], *ca, half_a).astype(k_out.dtype)
    for h in range(IDX_HEADS):
        sl = slice(h * IDX_SLOT, (h + 1) * IDX_SLOT)
        src = slice(DSA_W + h * IDX_SLOT, DSA_W + (h + 1) * IDX_SLOT)
        qi_out[:, sl] = _rope_slab(q_all[:, src], *ci, half_i).astype(qi_out.dtype)
    v_out[...] = v_ref[...].astype(v_out.dtype)

    idx = idx_ref[...]
    lane = lax.broadcasted_iota(jnp.int32, idx.shape, 1)
    ki = _rope_slab(jnp.where(lane < IDX_DH, idx, 0.0), *ci, half_i)
    ki_out[...] = ki.astype(ki_out.dtype)
    wi = pltpu.roll(idx, LANES - IDX_DH, 1)
    wi_out[...] = jnp.where(lane < IDX_HEADS, wi * (IDX_HEADS ** -0.5), 0.0)


def _dsa_prep(u, pos, q_norm_g, w_uq_pad, freqs, *, rows):
    t = u.shape[0]
    nq = DSA_W + IDX_HEADS * IDX_SLOT
    outs = (jax.ShapeDtypeStruct((t, DSA_W), MXU_DTYPE),
            jax.ShapeDtypeStruct((t, IDX_HEADS * IDX_SLOT), MXU_DTYPE),
            jax.ShapeDtypeStruct((t, DSA_W), MXU_DTYPE),
            jax.ShapeDtypeStruct((t, DSA_W), MXU_DTYPE),
            jax.ShapeDtypeStruct((t, IDX_SLOT), MXU_DTYPE),
            jax.ShapeDtypeStruct((t, LANES), F32))
    row_spec = lambda w, blk: pl.BlockSpec((rows, w), lambda i: (i, blk))
    return pl.pallas_call(
        _dsa_prep_kernel,
        grid=(t // rows,),
        in_specs=[row_spec(DSA_Q_RANK, OFF_DSA_CQ // DSA_Q_RANK),
                  row_spec(DSA_W, OFF_DSA_K // DSA_W),
                  row_spec(DSA_W, OFF_DSA_V // DSA_W),
                  row_spec(LANES, OFF_IDX // LANES),
                  pl.BlockSpec((rows, 1), lambda i: (i, 0)),
                  pl.BlockSpec((1, DSA_Q_RANK), lambda i: (0, 0)),
                  pl.BlockSpec((DSA_Q_RANK, nq), lambda i: (0, 0)),
                  pl.BlockSpec((2, LANES), lambda i: (0, 0))],
        out_specs=(row_spec(DSA_W, 0), row_spec(IDX_HEADS * IDX_SLOT, 0), row_spec(DSA_W, 0),
                   row_spec(DSA_W, 0), row_spec(IDX_SLOT, 0), row_spec(LANES, 0)),
        out_shape=outs,
        compiler_params=_params("parallel"),
        name="dsa_prep",
    )(u, u, u, u, pos, q_norm_g.reshape(1, DSA_Q_RANK), w_uq_pad, freqs)


IDX_TQ = 128


def _indexer_kernel(qi_ref, wi_ref, ki_ref, bias_ref, key_ref, *, topk):
    tq = IDX_TQ
    qb = pl.program_id(0)
    nkb = qb + 1
    w = wi_ref[...]
    row_chunk = (qb * tq + lax.broadcasted_iota(jnp.int32, (tq, 1), 0)) // CHUNK
    col_in_blk = lax.broadcasted_iota(jnp.int32, (1, tq), 1)

    def score_body(kb, carry):
        kt = ki_ref[pl.ds(pl.multiple_of(kb * tq, tq), tq), :]
        acc = jnp.zeros((tq, tq), F32)
        for h in range(IDX_HEADS):
            logits = _dot_nt(qi_ref[:, h * IDX_SLOT:(h + 1) * IDX_SLOT], kt) * (IDX_DH ** -0.5)
            acc += w[:, h:h + 1] * jnp.maximum(logits, 0.0)
        col_chunk = (kb * tq + col_in_blk) // CHUNK
        score = jnp.where(col_chunk <= row_chunk, acc, -jnp.inf)
        bits = pltpu.bitcast(score, jnp.int32)
        bits = jnp.where(bits == INT_MIN, 0, bits)
        key_ref[kb] = jnp.where(bits < 0, bits ^ 0x7FFFFFFF, bits)
        return carry

    lax.fori_loop(0, nkb, score_body, 0)

    def count_ge(cand):
        def body(kb, cnt):
            return cnt + jnp.where(key_ref[kb] >= cand, 1, 0)
        cnt = lax.fori_loop(0, nkb, body, jnp.zeros((tq, tq), jnp.int32))
        return jnp.sum(cnt, axis=1, keepdims=True)

    zero = jnp.zeros((tq, 1), jnp.int32)
    thr = jnp.where(count_ge(zero) >= topk, zero, INT_MIN)

    def bit_body(i, thr):
        cand = thr + jnp.left_shift(jnp.int32(1), 30 - i)
        return jnp.where(count_ge(cand) >= topk, cand, thr)

    thr = lax.fori_loop(0, 31, bit_body, thr)
    need = topk - count_ge(thr + 1)

    bias_ref[...] = jnp.full(bias_ref.shape, NEG_BIG, bias_ref.dtype)
    r_i = lax.broadcasted_iota(jnp.int32, (tq, tq), 0)
    c_i = lax.broadcasted_iota(jnp.int32, (tq, tq), 1)
    incl = jnp.where(r_i <= c_i, 1.0, 0.0).astype(MXU_DTYPE)

    def emit_body(kb, seen):
        key = key_ref[kb]
        eq = key == thr
        eq_f = jnp.where(eq, 1.0, 0.0)
        rank = seen + jnp.dot(eq_f.astype(MXU_DTYPE), incl, preferred_element_type=F32)
        take = (key > thr) | (eq & (rank <= need.astype(F32)))
        take = take & (key > KEY_NEG_INF)
        bias_ref[:, pl.ds(pl.multiple_of(kb * tq, tq), tq)] = jnp.where(take, 0.0, NEG_BIG).astype(bias_ref.dtype)
        return seen + jnp.sum(eq_f, axis=1, keepdims=True)

    lax.fori_loop(0, nkb, emit_body, jnp.zeros((tq, 1), F32))


def _indexer(qi, wi, ki, *, topk):
    t = qi.shape[0]
    return pl.pallas_call(
        functools.partial(_indexer_kernel, topk=topk),
        grid=(t // IDX_TQ,),
        in_specs=[pl.BlockSpec((IDX_TQ, IDX_HEADS * IDX_SLOT), lambda i: (i, 0)),
                  pl.BlockSpec((IDX_TQ, LANES), lambda i: (i, 0)),
                  pl.BlockSpec((t, IDX_SLOT), lambda i: (0, 0))],
        out_specs=pl.BlockSpec((IDX_TQ, t), lambda i: (i, 0)),
        out_shape=jax.ShapeDtypeStruct((t, t), BF16),
        scratch_shapes=[pltpu.VMEM((t // IDX_TQ, IDX_TQ, IDX_TQ), jnp.int32)],
        compiler_params=_params("parallel"),
        name="dsa_indexer",
    )(qi, wi, ki)


def _attn_kernel(q_ref, k_ref, v_ref, bias_ref, o_ref, m_ref, l_ref, acc_ref, *, tq, tk):
    qb, kb = pl.program_id(0), pl.program_id(1)
    last = ((qb + 1) * tq - 1) // tk

    @pl.when(kb == 0)
    def _():
        m_ref[...] = jnp.full(m_ref.shape, NEG_BIG, F32)
        l_ref[...] = jnp.zeros_like(l_ref)
        acc_ref[...] = jnp.zeros_like(acc_ref)

    @pl.when(kb <= last)
    def _():
        bias = bias_ref[...].astype(F32)
        for h in range(DSA_HEADS):
            sl = slice(h * DSA_DH, (h + 1) * DSA_DH)
            s = _dot_nt(q_ref[:, sl], k_ref[:, sl]) * (DSA_DH ** -0.5) + bias
            m_old = m_ref[h]
            m_new = jnp.maximum(m_old, jnp.max(s, axis=-1, keepdims=True))
            alpha = jnp.exp(m_old - m_new)
            p = jnp.exp(s - m_new)
            l_ref[h] = alpha * l_ref[h] + jnp.sum(p, axis=-1, keepdims=True)
            acc_ref[:, sl] = alpha * acc_ref[:, sl] + _dot(p, v_ref[:, sl])
            m_ref[h] = m_new

    @pl.when(kb == last)
    def _():
        for h in range(DSA_HEADS):
            sl = slice(h * DSA_DH, (h + 1) * DSA_DH)
            o_ref[:, sl] = (acc_ref[:, sl] / l_ref[h]).astype(o_ref.dtype)


def _attention(q, k, v, bias, *, tq, tk):
    t = q.shape[0]
    last = lambda i: ((i + 1) * tq - 1) // tk
    return pl.pallas_call(
        functools.partial(_attn_kernel, tq=tq, tk=tk),
        grid=(t // tq, t // tk),
        in_specs=[pl.BlockSpec((tq, DSA_W), lambda i, j: (i, 0)),
                  pl.BlockSpec((tk, DSA_W), lambda i, j: (jnp.minimum(j, last(i)), 0)),
                  pl.BlockSpec((tk, DSA_W), lambda i, j: (jnp.minimum(j, last(i)), 0)),
                  pl.BlockSpec((tq, tk), lambda i, j: (i, jnp.minimum(j, last(i))))],
        out_specs=pl.BlockSpec((tq, DSA_W), lambda i, j: (i, 0)),
        out_shape=jax.ShapeDtypeStruct((t, DSA_W), MXU_DTYPE),
        scratch_shapes=[pltpu.VMEM((DSA_HEADS, tq, 1), F32),
                        pltpu.VMEM((DSA_HEADS, tq, 1), F32),
                        pltpu.VMEM((tq, DSA_W), F32)],
        compiler_params=_params("parallel", "arbitrary"),
        name="dsa_attention",
    )(q, k, v, bias)


def _pad_heads(w, heads, width, padded):
    lead = w.shape[:-1]
    w = w.reshape(lead + (heads, width))
    w = jnp.pad(w, [(0, 0)] * len(lead) + [(0, 0), (0, padded - width)])
    return w.reshape(lead + (heads * padded,))


def _pack_w_in(w_in):
    parts, start = [], 0
    for width in IN_SPLITS:
        parts.append(w_in[..., start:start + width])
        start += width
    (ml_q, ml_k, ml_v, ml_i, ml_f, ml_o, dsa_cq, dsa_k, dsa_v, idx_k, idx_w,
     hg_q, hg_f, hg_i, hg_g) = parts
    zeros = lambda n: jnp.zeros(w_in.shape[:-1] + (n,), w_in.dtype)
    packed = jnp.concatenate([
        _pad_heads(ml_q, ML_HEADS, ML_DQK, ML_DQK_PAD),
        _pad_heads(ml_k, ML_HEADS, ML_DQK, ML_DQK_PAD),
        dsa_k, ml_v, ml_o, hg_q, hg_f, hg_i, hg_g, dsa_v,
        ml_i, ml_f, zeros(LANES - 2 * ML_HEADS),
        dsa_cq,
        idx_k, idx_w, zeros(LANES - IDX_DH - IDX_HEADS),
        zeros(D_IN_PAD - OFF_IDX - LANES)], axis=-1)
    return packed.astype(MXU_DTYPE)


def _rope_freqs():
    def lanes(d):
        rot = d // ROPE_FRACTION
        half = rot // 2
        inv = jnp.power(ROPE_THETA, -jnp.arange(half, dtype=F32) * (2.0 / rot))
        return jnp.concatenate([inv, inv, jnp.zeros((LANES - rot,), F32)])
    return jnp.stack([lanes(DSA_DH), lanes(IDX_DH)])


def kernel(x, positions, ln_mix_g, w_in, ml_conv_w, ml_gate_b, ml_norm_g, dsa_q_norm_g, dsa_w_uq,
           hg_lb_logits, hg_norm_g, w_out, ln_mlp_g, w_up, w_down, ln_final_g):
    bsz, t, d = x.shape
    assert bsz == 1 and t % 512 == 0 and d == D_MODEL
    depth = w_in.shape[0]
    topk = min(TOPK_MAX, t // 4)
    xs = x.reshape(t, d)
    pos = positions.reshape(t, 1)

    w_in_p = _pack_w_in(w_in)
    w_out_b = w_out.astype(MXU_DTYPE)
    w_up_b = w_up.astype(MXU_DTYPE)
    w_down_b = w_down.astype(MXU_DTYPE)
    conv_q = _pad_heads(ml_conv_w[..., :ML_QK_W], ML_HEADS, ML_DQK, ML_DQK_PAD)
    conv_k = _pad_heads(ml_conv_w[..., ML_QK_W:], ML_HEADS, ML_DQK, ML_DQK_PAD)
    conv_p = jnp.concatenate([conv_q.reshape(depth, ML_CONV, ML_HEADS, ML_DQK_PAD),
                              conv_k.reshape(depth, ML_CONV, ML_HEADS, ML_DQK_PAD)], axis=-1)
    conv_p = jnp.transpose(conv_p, (0, 2, 1, 3))
    gate_b = jnp.pad(ml_gate_b.reshape(depth, 1, 2 * ML_HEADS), ((0, 0), (0, 0), (0, LANES - 2 * ML_HEADS)))
    w_uq_p = jnp.concatenate([dsa_w_uq[..., :DSA_W],
                              _pad_heads(dsa_w_uq[..., DSA_W:], IDX_HEADS, IDX_DH, IDX_SLOT)],
                             axis=-1).astype(MXU_DTYPE)
    freqs = _rope_freqs()
    lb_cum = jnp.cumsum(jax.nn.softmax(hg_lb_logits.astype(F32), axis=0), axis=0)
    lb = lb_cum - lb_cum[:1]
    lb_tab = jnp.stack([jnp.log(lb), jnp.log1p(-lb), 1.0 - lb], axis=1)

    for layer in range(depth):
        u = _norm_matmul(xs, ln_mix_g[layer], w_in_p[layer], tm=512, tn=1024, act=False, out_dtype=F32)
        y_a = _mlstm(u, conv_p[layer], gate_b[layer], ml_norm_g[layer], rows=128)
        q_r, qi_r, k_r, v_b, ki_r, wi = _dsa_prep(u, pos, dsa_q_norm_g[layer], w_uq_p[layer], freqs, rows=256)
        bias = _indexer(qi_r, wi, ki_r, topk=topk)
        y_b = _attention(q_r, k_r, v_b, bias, tq=256, tk=512)
        y_c = _hgrn2(u, lb_tab[layer], hg_norm_g[layer], rows=256)
        xs = _mix_out(y_a, y_b, y_c, w_out_b[layer], xs, tm=512, tn=1024)
        a = _norm_matmul(xs, ln_mlp_g[layer], w_up_b[layer], tm=512, tn=1024, act=True, out_dtype=MXU_DTYPE)
        xs = _matmul_res(a, w_down_b[layer], xs, tm=1024, tn=1024, tk=2048)
    return _final_norm(xs, ln_final_g, tm=256).reshape(bsz, t, d)
```

```python
import functools

import jax
import jax.numpy as jnp
import numpy as np
from jax import lax
from jax.experimental import pallas as pl
from jax.experimental.pallas import tpu as pltpu

F32 = jnp.float32
BF16 = jnp.bfloat16
MXU_DTYPE = jnp.bfloat16

D_MODEL = 4096
DEPTH = 4
CHUNK = 64
EPS = 1e-6
ROPE_THETA = 500000.0
ROPE_FRACTION = 4
ML_HEADS = 4
ML_DV = 384
ML_DQK = ML_DV // 2
ML_DQK_PAD = 256
ML_CONV = 4
ML_GATE_CAP = 15.0
DSA_HEADS = 8
DSA_DH = 128
DSA_Q_RANK = 384
IDX_HEADS = 8
IDX_DH = 64
IDX_SLOT = 128
TOPK_MAX = 256
HG_HEADS = 12
HG_DK = 128
HG_DV = 128
D_FF = 4 * D_MODEL

ML_QK_W = ML_HEADS * ML_DQK
ML_V_W = ML_HEADS * ML_DV
DSA_W = DSA_HEADS * DSA_DH
IDX_W = IDX_HEADS * IDX_DH
HG_K_W = HG_HEADS * HG_DK
HG_V_W = HG_HEADS * HG_DV
D_MIX = ML_V_W + DSA_W + HG_V_W
IN_SPLITS = (ML_QK_W, ML_QK_W, ML_V_W, ML_HEADS, ML_HEADS, ML_V_W,
             DSA_Q_RANK, DSA_W, DSA_W, IDX_DH, IDX_HEADS,
             HG_K_W, HG_K_W, HG_V_W, HG_V_W)

LANES = 128
VMEM_LIMIT = 56 * 1024 * 1024

OFF_ML_Q = 0
OFF_ML_K = 1024
OFF_DSA_K = 2048
OFF_ML_V = 3072
OFF_ML_O = 4608
OFF_HG_Q = 6144
OFF_HG_F = 7680
OFF_HG_I = 9216
OFF_HG_G = 10752
OFF_DSA_V = 12288
OFF_GATES = 13312
OFF_DSA_CQ = 13440
OFF_IDX = 13824
D_IN_PAD = 14336

Q_SCALE = DSA_DH ** -0.5 * 1.4426950408889634

INT_MIN = -2 ** 31
KEY_NEG_INF = -2139095041
NEG_BIG = -1e30


def _mxu(a):
    return a.astype(MXU_DTYPE)


def _dot(a, b):
    return jnp.dot(_mxu(a), _mxu(b), preferred_element_type=F32)


def _dot_nt(a, b):
    return lax.dot_general(_mxu(a), _mxu(b), (((1,), (1,)), ((), ())), preferred_element_type=F32)


def _dot_tn(a, b):
    return lax.dot_general(_mxu(a), _mxu(b), (((0,), (0,)), ((), ())), preferred_element_type=F32)


def _dot_f32(a, b):
    return jnp.dot(a, b, precision=lax.Precision.HIGHEST, preferred_element_type=F32)


def _sigmoid(x):
    return 1.0 / (1.0 + jnp.exp(-x))


def _log_sigmoid(x):
    return jnp.minimum(x, 0.0) - jnp.log1p(jnp.exp(-jnp.abs(x)))


def _params(*sem):
    return pltpu.CompilerParams(dimension_semantics=sem, vmem_limit_bytes=VMEM_LIMIT)


def _norm_matmul_kernel(x_ref, g_ref, w_ref, o_ref, h_ref, *, act):
    @pl.when(pl.program_id(1) == 0)
    def _():
        x = x_ref[...]
        ms = jnp.mean(x * x, axis=-1, keepdims=True)
        h_ref[...] = (x * lax.rsqrt(ms + EPS) * g_ref[...]).astype(h_ref.dtype)

    y = jnp.dot(h_ref[...], w_ref[...], preferred_element_type=F32)
    if act:
        y = jnp.square(jnp.maximum(y, 0.0))
    o_ref[...] = y.astype(o_ref.dtype)


def _norm_matmul(x, g, w, *, tm, tn, act, out_dtype):
    m, k = x.shape
    n = w.shape[1]
    return pl.pallas_call(
        functools.partial(_norm_matmul_kernel, act=act),
        grid=(m // tm, n // tn),
        in_specs=[pl.BlockSpec((tm, k), lambda i, j: (i, 0)),
                  pl.BlockSpec((1, k), lambda i, j: (0, 0)),
                  pl.BlockSpec((k, tn), lambda i, j: (0, j))],
        out_specs=pl.BlockSpec((tm, tn), lambda i, j: (i, j)),
        out_shape=jax.ShapeDtypeStruct((m, n), out_dtype),
        scratch_shapes=[pltpu.VMEM((tm, k), MXU_DTYPE)],
        compiler_params=_params("parallel", "arbitrary"),
        name="norm_matmul",
    )(x, g.reshape(1, k), w)


def _matmul_res_kernel(a_ref, w_ref, r_ref, o_ref):
    part = jnp.dot(a_ref[...], w_ref[...], preferred_element_type=F32)

    @pl.when(pl.program_id(2) == 0)
    def _():
        o_ref[...] = r_ref[...] + part

    @pl.when(pl.program_id(2) != 0)
    def _():
        o_ref[...] += part


def _matmul_res(a, w, res, *, tm, tn, tk):
    m, k = a.shape
    n = w.shape[1]
    return pl.pallas_call(
        _matmul_res_kernel,
        grid=(m // tm, n // tn, k // tk),
        in_specs=[pl.BlockSpec((tm, tk), lambda i, j, kk: (i, kk)),
                  pl.BlockSpec((tk, tn), lambda i, j, kk: (kk, j)),
                  pl.BlockSpec((tm, tn), lambda i, j, kk: (i, j))],
        out_specs=pl.BlockSpec((tm, tn), lambda i, j, kk: (i, j)),
        out_shape=jax.ShapeDtypeStruct((m, n), F32),
        compiler_params=_params("parallel", "parallel", "arbitrary"),
        name="matmul_res",
    )(a, w, res)


def _mix_out_kernel(ya_ref, yb_ref, yc_ref, w_ref, r_ref, o_ref):
    acc = r_ref[...]
    acc += jnp.dot(ya_ref[...], w_ref[0:ML_V_W, :], preferred_element_type=F32)
    acc += jnp.dot(yb_ref[...], w_ref[ML_V_W:ML_V_W + DSA_W, :], preferred_element_type=F32)
    acc += jnp.dot(yc_ref[...], w_ref[ML_V_W + DSA_W:D_MIX, :], preferred_element_type=F32)
    o_ref[...] = acc


def _mix_out(ya, yb, yc, w, res, *, tm, tn):
    m = ya.shape[0]
    n = w.shape[1]
    return pl.pallas_call(
        _mix_out_kernel,
        grid=(m // tm, n // tn),
        in_specs=[pl.BlockSpec((tm, ML_V_W), lambda i, j: (i, 0)),
                  pl.BlockSpec((tm, DSA_W), lambda i, j: (i, 0)),
                  pl.BlockSpec((tm, HG_V_W), lambda i, j: (i, 0)),
                  pl.BlockSpec((D_MIX, tn), lambda i, j: (0, j)),
                  pl.BlockSpec((tm, tn), lambda i, j: (i, j))],
        out_specs=pl.BlockSpec((tm, tn), lambda i, j: (i, j)),
        out_shape=jax.ShapeDtypeStruct((m, n), F32),
        compiler_params=_params("parallel", "parallel"),
        name="mix_out",
    )(ya, yb, yc, w, res)


def _final_norm_kernel(x_ref, g_ref, o_ref):
    x = x_ref[...]
    ms = jnp.mean(x * x, axis=-1, keepdims=True)
    o_ref[...] = x * lax.rsqrt(ms + EPS) * g_ref[...]


def _final_norm(x, g, *, tm):
    m, k = x.shape
    return pl.pallas_call(
        _final_norm_kernel,
        grid=(m // tm,),
        in_specs=[pl.BlockSpec((tm, k), lambda i: (i, 0)),
                  pl.BlockSpec((1, k), lambda i: (0, 0))],
        out_specs=pl.BlockSpec((tm, k), lambda i: (i, 0)),
        out_shape=jax.ShapeDtypeStruct((m, k), F32),
        compiler_params=_params("parallel"),
        name="final_norm",
    )(x, g.reshape(1, k))


def _mlstm_kernel(q_ref, k_ref, v_ref, o_ref, gt_ref, cw_ref, gb_ref, ng_ref, out_ref,
                  xbuf, c_ref, *, rows):
    head = pl.program_id(0)

    @pl.when(pl.program_id(1) == 0)
    def _():
        xbuf[0:8, :] = jnp.zeros((8, 2 * ML_DQK_PAD), F32)
        c_ref[...] = jnp.zeros_like(c_ref)

    xbuf[8:8 + rows, 0:ML_DQK_PAD] = q_ref[...]
    xbuf[8:8 + rows, ML_DQK_PAD:] = k_ref[...]
    cw = cw_ref[...]
    acc = xbuf[8:8 + rows, :] * cw[ML_CONV - 1:ML_CONV, :]
    for j in range(1, ML_CONV):
        acc += xbuf[8 - j:8 - j + rows, :] * cw[ML_CONV - 1 - j:ML_CONV - j, :]
    xbuf[0:8, :] = xbuf[rows:rows + 8, :]
    qk = acc * _sigmoid(acc)
    q = qk[:, 0:ML_DQK_PAD]
    k = qk[:, ML_DQK_PAD:] * (ML_DQK ** -0.5)

    capped = ML_GATE_CAP * jnp.tanh((gt_ref[...] + gb_ref[...]) * (1.0 / ML_GATE_CAP))
    lsig = _log_sigmoid(capped)
    r_i = lax.broadcasted_iota(jnp.int32, (rows, rows), 0)
    c_i = lax.broadcasted_iota(jnp.int32, (rows, rows), 1)
    causal = c_i <= r_i
    b_all = _dot_f32(jnp.where(causal, 1.0, 0.0), lsig)
    lane = lax.broadcasted_iota(jnp.int32, (rows, LANES), 1)
    ig_col = jnp.sum(jnp.where(lane == head, capped, 0.0), axis=1, keepdims=True)
    b_col = jnp.sum(jnp.where(lane == head + ML_HEADS, b_all, 0.0), axis=1, keepdims=True)
    sub = lax.broadcasted_iota(jnp.int32, (LANES, rows), 0)
    ig_row = jnp.sum(jnp.where(sub == head, capped.T, 0.0), axis=0, keepdims=True)
    b_row = jnp.sum(jnp.where(sub == head + ML_HEADS, b_all.T, 0.0), axis=0, keepdims=True)

    dmat = jnp.exp(jnp.where(causal, b_col - b_row + ig_row, -jnp.inf))
    s = _dot_nt(q, k) * dmat
    one_col = jnp.where(lane == 0, 1.0, 0.0)
    v_ext = jnp.concatenate([v_ref[...], one_col], axis=1)
    c_old = c_ref[...]
    num_ext = _dot(s, v_ext) + jnp.exp(b_col) * _dot(q, c_old)
    num = num_ext[:, 0:ML_DV]
    den = num_ext[:, ML_DV:ML_DV + 1]
    hh = num / jnp.maximum(jnp.abs(den), 1.0)

    b_last = b_col[rows - 1:rows, :]
    w_s = jnp.exp(b_last - b_col + ig_col)
    c_ref[...] = jnp.exp(b_last) * c_old + _dot_tn(k, w_s * v_ext)

    ms = jnp.mean(hh * hh, axis=-1, keepdims=True)
    y = hh * lax.rsqrt(ms + EPS) * ng_ref[...]
    out_ref[...] = (_sigmoid(o_ref[...]) * y).astype(out_ref.dtype)


def _mlstm(u, conv_w, gate_b, norm_g, *, rows):
    t = u.shape[0]
    qb, kb = OFF_ML_Q // ML_DQK_PAD, OFF_ML_K // ML_DQK_PAD
    vb, ob = OFF_ML_V // ML_DV, OFF_ML_O // ML_DV
    gb = OFF_GATES // LANES
    return pl.pallas_call(
        functools.partial(_mlstm_kernel, rows=rows),
        grid=(ML_HEADS, t // rows),
        in_specs=[pl.BlockSpec((rows, ML_DQK_PAD), lambda h, c: (c, qb + h)),
                  pl.BlockSpec((rows, ML_DQK_PAD), lambda h, c: (c, kb + h)),
                  pl.BlockSpec((rows, ML_DV), lambda h, c: (c, vb + h)),
                  pl.BlockSpec((rows, ML_DV), lambda h, c: (c, ob + h)),
                  pl.BlockSpec((rows, LANES), lambda h, c: (c, gb)),
                  pl.BlockSpec((None, ML_CONV, 2 * ML_DQK_PAD), lambda h, c: (h, 0, 0)),
                  pl.BlockSpec((1, LANES), lambda h, c: (0, 0)),
                  pl.BlockSpec((1, ML_DV), lambda h, c: (0, h))],
        out_specs=pl.BlockSpec((rows, ML_DV), lambda h, c: (c, h)),
        out_shape=jax.ShapeDtypeStruct((t, ML_V_W), MXU_DTYPE),
        scratch_shapes=[pltpu.VMEM((rows + 8, 2 * ML_DQK_PAD), F32),
                        pltpu.VMEM((ML_DQK_PAD, ML_DV + LANES), F32)],
        compiler_params=_params("parallel", "arbitrary"),
        name="mlstm",
    )(u, u, u, u, u, conv_w, gate_b, norm_g.reshape(1, ML_V_W))


SUBLANES = 8


def _pair_level_table(rows):
    t = np.arange(rows)[:, None]
    s = np.arange(rows)[None, :]
    x = np.bitwise_xor(t, s)
    lvl = np.floor(np.log2(np.maximum(x, 1))).astype(np.int32)
    return jnp.asarray(np.where(t > s, lvl, -1).astype(np.int32))


def _block_sums(log_f, rows):
    sub = lax.broadcasted_iota(jnp.int32, log_f.shape, 0)
    groups = rows // SUBLANES

    def row_of_group(x, j):
        x3 = x.reshape(groups, SUBLANES, x.shape[-1])
        return jnp.broadcast_to(x3[:, j:j + 1, :], x3.shape).reshape(x.shape)

    c = log_f
    e = jnp.zeros_like(log_f)
    out = [(c, e)]
    odd = (sub & 1) == 1
    c, e = (c + jnp.where(odd, pltpu.roll(c, 1, 0), 0.0),
            e + jnp.where(odd, 0.0, pltpu.roll(c, rows - 1, 0)))
    out.append((c, e))
    r8 = sub & 7
    c, e = (c + jnp.where((r8 == 2) | (r8 == 3), row_of_group(c, 1),
                          jnp.where((r8 == 6) | (r8 == 7), row_of_group(c, 5), 0.0)),
            e + jnp.where((r8 == 0) | (r8 == 1), row_of_group(c, 3),
                          jnp.where((r8 == 4) | (r8 == 5), row_of_group(c, 7), 0.0)))
    out.append((c, e))
    c, e = (c + jnp.where(r8 >= 4, row_of_group(c, 3), 0.0),
            e + jnp.where(r8 < 4, row_of_group(c, 7), 0.0))
    out.append((c, e))
    m = SUBLANES
    while m < rows:
        cs, es = [], []
        for p in range(rows // (2 * m)):
            lo, mid, hi = 2 * m * p, 2 * m * p + m, 2 * m * (p + 1)
            cs += [c[lo:mid], c[mid:hi] + c[mid - 1:mid]]
            es += [e[lo:mid] + c[hi - 1:hi], e[mid:hi]]
        c, e = jnp.concatenate(cs, axis=0), jnp.concatenate(es, axis=0)
        out.append((c, e))
        m *= 2
    return out


def _hgrn2_kernel(q_ref, f_ref, i_ref, g_ref, lb_ref, ng_ref, lvl_ref, out_ref, st_ref, *, rows):
    @pl.when(pl.program_id(1) == 0)
    def _():
        st_ref[...] = jnp.zeros_like(st_ref)

    qp = q_ref[...]
    q = qp * _sigmoid(qp)
    fp = f_ref[...]
    v = i_ref[...]
    gp = g_ref[...]
    log_lb = lb_ref[0:1, :]
    x2 = lb_ref[1:2, :] + _log_sigmoid(fp)
    log_f = jnp.maximum(log_lb, x2) + jnp.log1p(jnp.exp(-jnp.abs(log_lb - x2)))
    k = lb_ref[2:3, :] * _sigmoid(-fp)

    sums = _block_sums(log_f, rows)
    lvl = lvl_ref[...]
    attn = jnp.zeros((rows, rows), F32)
    for level, (c_m, e_m) in enumerate(sums[:-1]):
        pairs = _dot_nt(q * jnp.exp(c_m), k * jnp.exp(e_m))
        attn = jnp.where(lvl == level, pairs, attn)
    b, after = sums[-1]

    st = st_ref[...]
    o = (_dot(attn, v) + jnp.sum(q * k, axis=-1, keepdims=True) * v
         + _dot_nt(q * jnp.exp(b), st))
    st_ref[...] = st * jnp.exp(b[rows - 1:rows, :]) + _dot_tn(v, k * jnp.exp(after))

    ms = jnp.mean(o * o, axis=-1, keepdims=True)
    y = o * lax.rsqrt(ms + EPS) * ng_ref[...]
    out_ref[...] = (y * (gp * _sigmoid(gp))).astype(out_ref.dtype)


def _hgrn2(u, lb_tab, norm_g, *, rows):
    t = u.shape[0]
    qb, fb = OFF_HG_Q // HG_DK, OFF_HG_F // HG_DK
    ib, gb = OFF_HG_I // HG_DV, OFF_HG_G // HG_DV
    return pl.pallas_call(
        functools.partial(_hgrn2_kernel, rows=rows),
        grid=(HG_HEADS, t // rows),
        in_specs=[pl.BlockSpec((rows, HG_DK), lambda h, c: (c, qb + h)),
                  pl.BlockSpec((rows, HG_DK), lambda h, c: (c, fb + h)),
                  pl.BlockSpec((rows, HG_DV), lambda h, c: (c, ib + h)),
                  pl.BlockSpec((rows, HG_DV), lambda h, c: (c, gb + h)),
                  pl.BlockSpec((3, HG_DK), lambda h, c: (0, h)),
                  pl.BlockSpec((1, HG_DV), lambda h, c: (0, h)),
                  pl.BlockSpec((rows, rows), lambda h, c: (0, 0))],
        out_specs=pl.BlockSpec((rows, HG_DV), lambda h, c: (c, h)),
        out_shape=jax.ShapeDtypeStruct((t, HG_V_W), MXU_DTYPE),
        scratch_shapes=[pltpu.VMEM((HG_DV, HG_DK), F32)],
        compiler_params=_params("parallel", "arbitrary"),
        name="hgrn2",
    )(u, u, u, u, lb_tab, norm_g.reshape(1, HG_V_W), _pair_level_table(rows))


def _rope_slab(u, cc, sa, sb, half):
    return u * cc + pltpu.roll(u, LANES - half, 1) * sa + pltpu.roll(u, half, 1) * sb


def _rope_coeffs(ang, half):
    lane = lax.broadcasted_iota(jnp.int32, ang.shape, 1)
    cos, sin = jnp.cos(ang), jnp.sin(ang)
    cc = jnp.where(lane < 2 * half, cos, 1.0)
    sa = jnp.where(lane < half, -sin, 0.0)
    sb = jnp.where((lane >= half) & (lane < 2 * half), sin, 0.0)
    return cc, sa, sb


def _dsa_prep_kernel(cq_ref, k_ref, v_ref, idx_ref, pos_ref, g_ref, w_ref, fr_ref,
                     q_out, qi_out, k_out, v_out, ki_out, wi_out):
    pos = pos_ref[...].astype(F32)
    cq = cq_ref[...]
    ms = jnp.mean(cq * cq, axis=-1, keepdims=True)
    hq = cq * lax.rsqrt(ms + EPS) * g_ref[...]
    q_all = _dot(hq, w_ref[...])

    half_a = DSA_DH // ROPE_FRACTION // 2
    half_i = IDX_DH // ROPE_FRACTION // 2
    ca = _rope_coeffs(pos * fr_ref[0:1, :], half_a)
    ci = _rope_coeffs(pos * fr_ref[1:2, :], half_i)

    kk = k_ref[...]
    for h in range(DSA_HEADS):
        sl = slice(h * DSA_DH, (h + 1) * DSA_DH)
        q_out[:, sl] = (_rope_slab(q_all[:, sl], *ca, half_a) * Q_SCALE).astype(q_out.dtype)
        k_out[:, sl] = _rope_slab(kk[:, sl], *ca, half_a).astype(k_out.dtype)
    for h in range(IDX_HEADS):
        src = slice(DSA_W + h * IDX_SLOT, DSA_W + (h + 1) * IDX_SLOT)
        qi_out[h] = _rope_slab(q_all[:, src], *ci, half_i).astype(qi_out.dtype)
    v_out[...] = v_ref[...].astype(v_out.dtype)

    idx = idx_ref[...]
    lane = lax.broadcasted_iota(jnp.int32, idx.shape, 1)
    ki = _rope_slab(jnp.where(lane < IDX_DH, idx, 0.0), *ci, half_i)
    ki_out[...] = ki.astype(ki_out.dtype)
    wi = pltpu.roll(idx, LANES - IDX_DH, 1) * (IDX_HEADS ** -0.5 * IDX_DH ** -0.5)
    wi_out[...] = jnp.where(lane < IDX_HEADS, wi, 0.0)


def _dsa_prep(u, pos, q_norm_g, w_uq_pad, freqs, *, rows):
    t = u.shape[0]
    nq = DSA_W + IDX_HEADS * IDX_SLOT
    outs = (jax.ShapeDtypeStruct((t, DSA_W), MXU_DTYPE),
            jax.ShapeDtypeStruct((IDX_HEADS, t, IDX_SLOT), MXU_DTYPE),
            jax.ShapeDtypeStruct((t, DSA_W), MXU_DTYPE),
            jax.ShapeDtypeStruct((t, DSA_W), MXU_DTYPE),
            jax.ShapeDtypeStruct((t, IDX_SLOT), MXU_DTYPE),
            jax.ShapeDtypeStruct((t, LANES), F32))
    row_spec = lambda w, blk: pl.BlockSpec((rows, w), lambda i: (i, blk))
    return pl.pallas_call(
        _dsa_prep_kernel,
        grid=(t // rows,),
        in_specs=[row_spec(DSA_Q_RANK, OFF_DSA_CQ // DSA_Q_RANK),
                  row_spec(DSA_W, OFF_DSA_K // DSA_W),
                  row_spec(DSA_W, OFF_DSA_V // DSA_W),
                  row_spec(LANES, OFF_IDX // LANES),
                  pl.BlockSpec((rows, 1), lambda i: (i, 0)),
                  pl.BlockSpec((1, DSA_Q_RANK), lambda i: (0, 0)),
                  pl.BlockSpec((DSA_Q_RANK, nq), lambda i: (0, 0)),
                  pl.BlockSpec((2, LANES), lambda i: (0, 0))],
        out_specs=(row_spec(DSA_W, 0),
                   pl.BlockSpec((IDX_HEADS, rows, IDX_SLOT), lambda i: (0, i, 0)),
                   row_spec(DSA_W, 0), row_spec(DSA_W, 0), row_spec(IDX_SLOT, 0), row_spec(LANES, 0)),
        out_shape=outs,
        compiler_params=_params("parallel"),
        name="dsa_prep",
    )(u, u, u, u, pos, q_norm_g.reshape(1, DSA_Q_RANK), w_uq_pad, freqs)


IDX_TQ = 256
IDX_TK = 512
IDX_SLAB = 128


def _indexer_kernel(qi_ref, wi_ref, ki_ref, incl_ref, bias_ref, key_ref, *, topk):
    tq, tk = IDX_TQ, IDX_TK
    qb = pl.program_id(0)
    nkb = ((qb + 1) * tq + tk - 1) // tk
    q_all = qi_ref[...].reshape(IDX_HEADS * tq, IDX_SLOT)
    w = wi_ref[...]
    w_col = jnp.concatenate([w[:, h:h + 1] for h in range(IDX_HEADS)], axis=0)
    row_chunk = (qb * tq + lax.broadcasted_iota(jnp.int32, (tq, 1), 0)) // CHUNK
    col_in_blk = lax.broadcasted_iota(jnp.int32, (1, tk), 1)

    def score_body(kb, carry):
        kt = ki_ref[pl.ds(pl.multiple_of(kb * tk, tk), tk), :]
        weighted = jnp.maximum(_dot_nt(q_all, kt), 0.0) * w_col
        parts = [weighted[h * tq:(h + 1) * tq] for h in range(IDX_HEADS)]
        while len(parts) > 1:
            parts = [a + b for a, b in zip(parts[0::2], parts[1::2])]
        col_chunk = (kb * tk + col_in_blk) // CHUNK
        score = jnp.where(col_chunk <= row_chunk, parts[0], -jnp.inf)
        bits = pltpu.bitcast(score, jnp.int32)
        bits = jnp.where(bits == INT_MIN, 0, bits)
        key_ref[kb] = jnp.where(bits < 0, bits ^ 0x7FFFFFFF, bits)
        return carry

    lax.fori_loop(0, nkb, score_body, 0)

    ones_mat = jnp.ones((LANES, LANES), MXU_DTYPE)

    def count_ge(cand):
        counts = []
        for r0 in range(0, tq, IDX_SLAB):
            cand_r = cand[r0:r0 + IDX_SLAB]

            def body(kb, cnt, r0=r0, cand_r=cand_r):
                for j in range(tk // LANES):
                    cnt += jnp.where(key_ref[kb, r0:r0 + IDX_SLAB, j * LANES:(j + 1) * LANES] >= cand_r, 1, 0)
                return cnt
            counts.append(lax.fori_loop(0, nkb, body, jnp.zeros((IDX_SLAB, LANES), jnp.int32)))
        cnt = jnp.concatenate(counts, axis=0)
        return jnp.dot(cnt.astype(F32).astype(MXU_DTYPE), ones_mat, preferred_element_type=F32)

    zero = jnp.zeros((tq, LANES), jnp.int32)
    thr_rep = jnp.where(count_ge(zero) >= topk, zero, INT_MIN)

    def bit_body(i, thr_rep):
        cand = thr_rep + jnp.left_shift(jnp.int32(1), 30 - i)
        return jnp.where(count_ge(cand) >= topk, cand, thr_rep)

    thr_rep = lax.fori_loop(0, 31, bit_body, thr_rep)
    need = topk - count_ge(thr_rep + 1)[:, 0:1]
    thr = thr_rep[:, 0:1]

    bias_ref[...] = jnp.full(bias_ref.shape, NEG_BIG, bias_ref.dtype)

    def emit_body(kb, seen):
        key = key_ref[kb]
        eq = key == thr
        eq_f = jnp.where(eq, 1.0, 0.0)
        rank = seen + jnp.dot(eq_f.astype(MXU_DTYPE), incl_ref[...], preferred_element_type=F32)
        take = ((key > thr) | (eq & (rank <= need))) & (key > KEY_NEG_INF)
        bias_ref[:, pl.ds(pl.multiple_of(kb * tk, tk), tk)] = jnp.where(take, 0.0, NEG_BIG).astype(bias_ref.dtype)
        return seen + jnp.sum(eq_f, axis=1, keepdims=True)

    lax.fori_loop(0, nkb, emit_body, jnp.zeros((tq, 1), F32))


def _indexer(qi, wi, ki, *, topk):
    t = ki.shape[0]
    incl = jnp.asarray(np.triu(np.ones((IDX_TK, IDX_TK), np.float32)), MXU_DTYPE)
    return pl.pallas_call(
        functools.partial(_indexer_kernel, topk=topk),
        grid=(t // IDX_TQ,),
        in_specs=[pl.BlockSpec((IDX_HEADS, IDX_TQ, IDX_SLOT), lambda i: (0, i, 0)),
                  pl.BlockSpec((IDX_TQ, LANES), lambda i: (i, 0)),
                  pl.BlockSpec((t, IDX_SLOT), lambda i: (0, 0)),
                  pl.BlockSpec((IDX_TK, IDX_TK), lambda i: (0, 0))],
        out_specs=pl.BlockSpec((IDX_TQ, t), lambda i: (i, 0)),
        out_shape=jax.ShapeDtypeStruct((t, t), BF16),
        scratch_shapes=[pltpu.VMEM((t // IDX_TK, IDX_TQ, IDX_TK), jnp.int32)],
        compiler_params=_params("parallel"),
        name="dsa_indexer",
    )(qi, wi, ki, incl)


def _attn_kernel(q_ref, k_ref, v_ref, bias_ref, o_ref, m_ref, l_ref, acc_ref, *, tq, tk):
    qb, kb = pl.program_id(0), pl.program_id(1)
    last = ((qb + 1) * tq - 1) // tk

    @pl.when(kb == 0)
    def _():
        m_ref[...] = jnp.full(m_ref.shape, NEG_BIG, F32)
        l_ref[...] = jnp.zeros_like(l_ref)
        acc_ref[...] = jnp.zeros_like(acc_ref)

    @pl.when(kb <= last)
    def _():
        bias = bias_ref[...].astype(F32)
        for h in range(DSA_HEADS):
            sl = slice(h * DSA_DH, (h + 1) * DSA_DH)
            s = _dot_nt(q_ref[:, sl], k_ref[:, sl]) + bias
            m_old = m_ref[h]
            m_new = jnp.maximum(m_old, jnp.max(s, axis=-1, keepdims=True))
            alpha = jnp.exp2(m_old - m_new)
            p = jnp.exp2(s - jnp.concatenate([m_new] * (tk // LANES), axis=1))
            l_ref[h] = alpha * l_ref[h] + jnp.sum(p, axis=-1, keepdims=True)
            acc_ref[:, sl] = alpha * acc_ref[:, sl] + _dot(p, v_ref[:, sl])
            m_ref[h] = m_new

    @pl.when(kb == last)
    def _():
        for h in range(DSA_HEADS):
            sl = slice(h * DSA_DH, (h + 1) * DSA_DH)
            o_ref[:, sl] = (acc_ref[:, sl] / l_ref[h]).astype(o_ref.dtype)


def _attention(q, k, v, bias, *, tq, tk):
    t = q.shape[0]
    last = lambda i: ((i + 1) * tq - 1) // tk
    return pl.pallas_call(
        functools.partial(_attn_kernel, tq=tq, tk=tk),
        grid=(t // tq, t // tk),
        in_specs=[pl.BlockSpec((tq, DSA_W), lambda i, j: (i, 0)),
                  pl.BlockSpec((tk, DSA_W), lambda i, j: (jnp.minimum(j, last(i)), 0)),
                  pl.BlockSpec((tk, DSA_W), lambda i, j: (jnp.minimum(j, last(i)), 0)),
                  pl.BlockSpec((tq, tk), lambda i, j: (i, jnp.minimum(j, last(i))))],
        out_specs=pl.BlockSpec((tq, DSA_W), lambda i, j: (i, 0)),
        out_shape=jax.ShapeDtypeStruct((t, DSA_W), MXU_DTYPE),
        scratch_shapes=[pltpu.VMEM((DSA_HEADS, tq, LANES), F32),
                        pltpu.VMEM((DSA_HEADS, tq, LANES), F32),
                        pltpu.VMEM((tq, DSA_W), F32)],
        compiler_params=_params("parallel", "arbitrary"),
        name="dsa_attention",
    )(q, k, v, bias)


def _pad_heads(w, heads, width, padded):
    lead = w.shape[:-1]
    w = w.reshape(lead + (heads, width))
    w = jnp.pad(w, [(0, 0)] * len(lead) + [(0, 0), (0, padded - width)])
    return w.reshape(lead + (heads * padded,))


def _pack_w_in(w_in):
    parts, start = [], 0
    for width in IN_SPLITS:
        parts.append(w_in[..., start:start + width])
        start += width
    (ml_q, ml_k, ml_v, ml_i, ml_f, ml_o, dsa_cq, dsa_k, dsa_v, idx_k, idx_w,
     hg_q, hg_f, hg_i, hg_g) = parts
    zeros = lambda n: jnp.zeros(w_in.shape[:-1] + (n,), w_in.dtype)
    packed = jnp.concatenate([
        _pad_heads(ml_q, ML_HEADS, ML_DQK, ML_DQK_PAD),
        _pad_heads(ml_k, ML_HEADS, ML_DQK, ML_DQK_PAD),
        dsa_k, ml_v, ml_o, hg_q, hg_f, hg_i, hg_g, dsa_v,
        ml_i, ml_f, zeros(LANES - 2 * ML_HEADS),
        dsa_cq,
        idx_k, idx_w, zeros(LANES - IDX_DH - IDX_HEADS),
        zeros(D_IN_PAD - OFF_IDX - LANES)], axis=-1)
    return packed.astype(MXU_DTYPE)


def _rope_freqs():
    def lanes(d):
        rot = d // ROPE_FRACTION
        half = rot // 2
        inv = jnp.power(ROPE_THETA, -jnp.arange(half, dtype=F32) * (2.0 / rot))
        return jnp.concatenate([inv, inv, jnp.zeros((LANES - rot,), F32)])
    return jnp.stack([lanes(DSA_DH), lanes(IDX_DH)])


def kernel(x, positions, ln_mix_g, w_in, ml_conv_w, ml_gate_b, ml_norm_g, dsa_q_norm_g, dsa_w_uq,
           hg_lb_logits, hg_norm_g, w_out, ln_mlp_g, w_up, w_down, ln_final_g):
    bsz, t, d = x.shape
    assert bsz == 1 and t % 512 == 0 and d == D_MODEL
    depth = w_in.shape[0]
    topk = min(TOPK_MAX, t // 4)
    xs = x.reshape(t, d)
    pos = positions.reshape(t, 1)

    w_in_p = _pack_w_in(w_in.astype(MXU_DTYPE))
    w_out_b = w_out.astype(MXU_DTYPE)
    w_up_b = w_up.astype(MXU_DTYPE)
    w_down_b = w_down.astype(MXU_DTYPE)
    conv_q = _pad_heads(ml_conv_w[..., :ML_QK_W], ML_HEADS, ML_DQK, ML_DQK_PAD)
    conv_k = _pad_heads(ml_conv_w[..., ML_QK_W:], ML_HEADS, ML_DQK, ML_DQK_PAD)
    conv_p = jnp.concatenate([conv_q.reshape(depth, ML_CONV, ML_HEADS, ML_DQK_PAD),
                              conv_k.reshape(depth, ML_CONV, ML_HEADS, ML_DQK_PAD)], axis=-1)
    conv_p = jnp.transpose(conv_p, (0, 2, 1, 3))
    gate_b = jnp.pad(ml_gate_b.reshape(depth, 1, 2 * ML_HEADS), ((0, 0), (0, 0), (0, LANES - 2 * ML_HEADS)))
    w_uq_p = jnp.concatenate([dsa_w_uq[..., :DSA_W],
                              _pad_heads(dsa_w_uq[..., DSA_W:], IDX_HEADS, IDX_DH, IDX_SLOT)],
                             axis=-1).astype(MXU_DTYPE)
    freqs = _rope_freqs()
    lb_cum = jnp.cumsum(jax.nn.softmax(hg_lb_logits.astype(F32), axis=0), axis=0)
    lb = lb_cum - lb_cum[:1]
    lb_tab = jnp.stack([jnp.log(lb), jnp.log1p(-lb), 1.0 - lb], axis=1)

    for layer in range(depth):
        u = _norm_matmul(xs, ln_mix_g[layer], w_in_p[layer], tm=512, tn=1024, act=False, out_dtype=F32)
        y_a = _mlstm(u, conv_p[layer], gate_b[layer], ml_norm_g[layer], rows=128)
        q_r, qi_r, k_r, v_b, ki_r, wi = _dsa_prep(u, pos, dsa_q_norm_g[layer], w_uq_p[layer], freqs, rows=256)
        bias = _indexer(qi_r, wi, ki_r, topk=topk)
        y_b = _attention(q_r, k_r, v_b, bias, tq=256, tk=512)
        y_c = _hgrn2(u, lb_tab[layer], hg_norm_g[layer], rows=256)
        xs = _mix_out(y_a, y_b, y_c, w_out_b[layer], xs, tm=512, tn=1024)
        a = _norm_matmul(xs, ln_mlp_g[layer], w_up_b[layer], tm=512, tn=1024, act=True, out_dtype=MXU_DTYPE)
        xs = _matmul_res(a, w_down_b[layer], xs, tm=1024, tn=1024, tk=2048)
    return _final_norm(xs, ln_final_g, tm=256).reshape(bsz, t, d)
```

```python
import functools

import jax
import jax.numpy as jnp
import numpy as np
from jax import lax
from jax.experimental import pallas as pl
from jax.experimental.pallas import tpu as pltpu

F32 = jnp.float32
BF16 = jnp.bfloat16
MXU_DTYPE = jnp.bfloat16

D_MODEL = 4096
DEPTH = 4
CHUNK = 64
EPS = 1e-6
ROPE_THETA = 500000.0
ROPE_FRACTION = 4
ML_HEADS = 4
ML_DV = 384
ML_DQK = ML_DV // 2
ML_DQK_PAD = 256
ML_CONV = 4
ML_GATE_CAP = 15.0
DSA_HEADS = 8
DSA_DH = 128
DSA_Q_RANK = 384
IDX_HEADS = 8
IDX_DH = 64
IDX_SLOT = 128
TOPK_MAX = 256
HG_HEADS = 12
HG_DK = 128
HG_DV = 128
D_FF = 4 * D_MODEL

ML_QK_W = ML_HEADS * ML_DQK
ML_V_W = ML_HEADS * ML_DV
DSA_W = DSA_HEADS * DSA_DH
IDX_W = IDX_HEADS * IDX_DH
HG_K_W = HG_HEADS * HG_DK
HG_V_W = HG_HEADS * HG_DV
D_MIX = ML_V_W + DSA_W + HG_V_W
IN_SPLITS = (ML_QK_W, ML_QK_W, ML_V_W, ML_HEADS, ML_HEADS, ML_V_W,
             DSA_Q_RANK, DSA_W, DSA_W, IDX_DH, IDX_HEADS,
             HG_K_W, HG_K_W, HG_V_W, HG_V_W)

LANES = 128
VMEM_LIMIT = 56 * 1024 * 1024

OFF_ML_Q = 0
OFF_ML_K = 1024
OFF_DSA_K = 2048
OFF_ML_V = 3072
OFF_ML_O = 4608
OFF_HG_Q = 6144
OFF_HG_F = 7680
OFF_HG_I = 9216
OFF_HG_G = 10752
OFF_DSA_V = 12288
OFF_GATES = 13312
OFF_DSA_CQ = 13440
OFF_IDX = 13824
D_IN_PAD = 14336

Q_SCALE = DSA_DH ** -0.5 * 1.4426950408889634

INT_MIN = -2 ** 31
KEY_NEG_INF = -2139095041
NEG_BIG = -1e30


def _mxu(a):
    return a.astype(MXU_DTYPE)


def _dot(a, b):
    return jnp.dot(_mxu(a), _mxu(b), preferred_element_type=F32)


def _dot_nt(a, b):
    return lax.dot_general(_mxu(a), _mxu(b), (((1,), (1,)), ((), ())), preferred_element_type=F32)


def _dot_tn(a, b):
    return lax.dot_general(_mxu(a), _mxu(b), (((0,), (0,)), ((), ())), preferred_element_type=F32)


def _dot_f32(a, b):
    return jnp.dot(a, b, precision=lax.Precision.HIGHEST, preferred_element_type=F32)


def _sigmoid(x):
    return 1.0 / (1.0 + jnp.exp(-x))


def _log_sigmoid(x):
    return jnp.minimum(x, 0.0) - jnp.log1p(jnp.exp(-jnp.abs(x)))


def _params(*sem):
    return pltpu.CompilerParams(dimension_semantics=sem, vmem_limit_bytes=VMEM_LIMIT)


def _norm_matmul_kernel(x_ref, g_ref, w_ref, o_ref, h_ref, *, act):
    @pl.when(pl.program_id(1) == 0)
    def _():
        x = x_ref[...]
        ms = jnp.mean(x * x, axis=-1, keepdims=True)
        h_ref[...] = (x * lax.rsqrt(ms + EPS) * g_ref[...]).astype(h_ref.dtype)

    y = jnp.dot(h_ref[...], w_ref[...], preferred_element_type=F32)
    if act:
        y = jnp.square(jnp.maximum(y, 0.0))
    o_ref[...] = y.astype(o_ref.dtype)


def _norm_matmul(x, g, w, *, tm, tn, act, out_dtype):
    m, k = x.shape
    n = w.shape[1]
    return pl.pallas_call(
        functools.partial(_norm_matmul_kernel, act=act),
        grid=(m // tm, n // tn),
        in_specs=[pl.BlockSpec((tm, k), lambda i, j: (i, 0)),
                  pl.BlockSpec((1, k), lambda i, j: (0, 0)),
                  pl.BlockSpec((k, tn), lambda i, j: (0, j))],
        out_specs=pl.BlockSpec((tm, tn), lambda i, j: (i, j)),
        out_shape=jax.ShapeDtypeStruct((m, n), out_dtype),
        scratch_shapes=[pltpu.VMEM((tm, k), MXU_DTYPE)],
        compiler_params=_params("parallel", "arbitrary"),
        name="norm_matmul",
    )(x, g.reshape(1, k), w)


def _matmul_res_kernel(a_ref, w_ref, r_ref, o_ref):
    part = jnp.dot(a_ref[...], w_ref[...], preferred_element_type=F32)

    @pl.when(pl.program_id(2) == 0)
    def _():
        o_ref[...] = r_ref[...] + part

    @pl.when(pl.program_id(2) != 0)
    def _():
        o_ref[...] += part


def _matmul_res(a, w, res, *, tm, tn, tk):
    m, k = a.shape
    n = w.shape[1]
    return pl.pallas_call(
        _matmul_res_kernel,
        grid=(m // tm, n // tn, k // tk),
        in_specs=[pl.BlockSpec((tm, tk), lambda i, j, kk: (i, kk)),
                  pl.BlockSpec((tk, tn), lambda i, j, kk: (kk, j)),
                  pl.BlockSpec((tm, tn), lambda i, j, kk: (i, j))],
        out_specs=pl.BlockSpec((tm, tn), lambda i, j, kk: (i, j)),
        out_shape=jax.ShapeDtypeStruct((m, n), F32),
        compiler_params=_params("parallel", "parallel", "arbitrary"),
        name="matmul_res",
    )(a, w, res)


def _mix_out_kernel(ya_ref, yb_ref, yc_ref, w_ref, r_ref, o_ref):
    acc = r_ref[...]
    acc += jnp.dot(ya_ref[...], w_ref[0:ML_V_W, :], preferred_element_type=F32)
    acc += jnp.dot(yb_ref[...], w_ref[ML_V_W:ML_V_W + DSA_W, :], preferred_element_type=F32)
    acc += jnp.dot(yc_ref[...], w_ref[ML_V_W + DSA_W:D_MIX, :], preferred_element_type=F32)
    o_ref[...] = acc


def _mix_out(ya, yb, yc, w, res, *, tm, tn):
    m = ya.shape[0]
    n = w.shape[1]
    return pl.pallas_call(
        _mix_out_kernel,
        grid=(m // tm, n // tn),
        in_specs=[pl.BlockSpec((tm, ML_V_W), lambda i, j: (i, 0)),
                  pl.BlockSpec((tm, DSA_W), lambda i, j: (i, 0)),
                  pl.BlockSpec((tm, HG_V_W), lambda i, j: (i, 0)),
                  pl.BlockSpec((D_MIX, tn), lambda i, j: (0, j)),
                  pl.BlockSpec((tm, tn), lambda i, j: (i, j))],
        out_specs=pl.BlockSpec((tm, tn), lambda i, j: (i, j)),
        out_shape=jax.ShapeDtypeStruct((m, n), F32),
        compiler_params=_params("parallel", "parallel"),
        name="mix_out",
    )(ya, yb, yc, w, res)


def _final_norm_kernel(x_ref, g_ref, o_ref):
    x = x_ref[...]
    ms = jnp.mean(x * x, axis=-1, keepdims=True)
    o_ref[...] = x * lax.rsqrt(ms + EPS) * g_ref[...]


def _final_norm(x, g, *, tm):
    m, k = x.shape
    return pl.pallas_call(
        _final_norm_kernel,
        grid=(m // tm,),
        in_specs=[pl.BlockSpec((tm, k), lambda i: (i, 0)),
                  pl.BlockSpec((1, k), lambda i: (0, 0))],
        out_specs=pl.BlockSpec((tm, k), lambda i: (i, 0)),
        out_shape=jax.ShapeDtypeStruct((m, k), F32),
        compiler_params=_params("parallel"),
        name="final_norm",
    )(x, g.reshape(1, k))


def _mlstm_kernel(q_ref, k_ref, v_ref, o_ref, gt_ref, cw_ref, gb_ref, ng_ref, out_ref,
                  xbuf, c_ref, *, rows):
    head = pl.program_id(0)

    @pl.when(pl.program_id(1) == 0)
    def _():
        xbuf[0:8, :] = jnp.zeros((8, 2 * ML_DQK_PAD), F32)
        c_ref[...] = jnp.zeros_like(c_ref)

    xbuf[8:8 + rows, 0:ML_DQK_PAD] = q_ref[...]
    xbuf[8:8 + rows, ML_DQK_PAD:] = k_ref[...]
    cw = cw_ref[...]
    acc = xbuf[8:8 + rows, :] * cw[ML_CONV - 1:ML_CONV, :]
    for j in range(1, ML_CONV):
        acc += xbuf[8 - j:8 - j + rows, :] * cw[ML_CONV - 1 - j:ML_CONV - j, :]
    xbuf[0:8, :] = xbuf[rows:rows + 8, :]
    qk = acc * _sigmoid(acc)
    q = qk[:, 0:ML_DQK_PAD]
    k = qk[:, ML_DQK_PAD:] * (ML_DQK ** -0.5)

    capped = ML_GATE_CAP * jnp.tanh((gt_ref[...] + gb_ref[...]) * (1.0 / ML_GATE_CAP))
    lsig = _log_sigmoid(capped)
    r_i = lax.broadcasted_iota(jnp.int32, (rows, rows), 0)
    c_i = lax.broadcasted_iota(jnp.int32, (rows, rows), 1)
    causal = c_i <= r_i
    b_all = _dot_f32(jnp.where(causal, 1.0, 0.0), lsig)
    lane = lax.broadcasted_iota(jnp.int32, (rows, LANES), 1)
    ig_col = jnp.sum(jnp.where(lane == head, capped, 0.0), axis=1, keepdims=True)
    b_col = jnp.sum(jnp.where(lane == head + ML_HEADS, b_all, 0.0), axis=1, keepdims=True)
    sub = lax.broadcasted_iota(jnp.int32, (LANES, rows), 0)
    ig_row = jnp.sum(jnp.where(sub == head, capped.T, 0.0), axis=0, keepdims=True)
    b_row = jnp.sum(jnp.where(sub == head + ML_HEADS, b_all.T, 0.0), axis=0, keepdims=True)

    dmat = jnp.exp(jnp.where(causal, b_col - b_row + ig_row, -jnp.inf))
    s = _dot_nt(q, k) * dmat
    one_col = jnp.where(lane == 0, 1.0, 0.0)
    v_ext = jnp.concatenate([v_ref[...], one_col], axis=1)
    c_old = c_ref[...]
    num_ext = _dot(s, v_ext) + jnp.exp(b_col) * _dot(q, c_old)
    num = num_ext[:, 0:ML_DV]
    den = num_ext[:, ML_DV:ML_DV + 1]
    hh = num / jnp.maximum(jnp.abs(den), 1.0)

    b_last = b_col[rows - 1:rows, :]
    w_s = jnp.exp(b_last - b_col + ig_col)
    c_ref[...] = jnp.exp(b_last) * c_old + _dot_tn(k, w_s * v_ext)

    ms = jnp.mean(hh * hh, axis=-1, keepdims=True)
    y = hh * lax.rsqrt(ms + EPS) * ng_ref[...]
    out_ref[...] = (_sigmoid(o_ref[...]) * y).astype(out_ref.dtype)


def _mlstm(u, conv_w, gate_b, norm_g, *, rows):
    t = u.shape[0]
    qb, kb = OFF_ML_Q // ML_DQK_PAD, OFF_ML_K // ML_DQK_PAD
    vb, ob = OFF_ML_V // ML_DV, OFF_ML_O // ML_DV
    gb = OFF_GATES // LANES
    return pl.pallas_call(
        functools.partial(_mlstm_kernel, rows=rows),
        grid=(ML_HEADS, t // rows),
        in_specs=[pl.BlockSpec((rows, ML_DQK_PAD), lambda h, c: (c, qb + h)),
                  pl.BlockSpec((rows, ML_DQK_PAD), lambda h, c: (c, kb + h)),
                  pl.BlockSpec((rows, ML_DV), lambda h, c: (c, vb + h)),
                  pl.BlockSpec((rows, ML_DV), lambda h, c: (c, ob + h)),
                  pl.BlockSpec((rows, LANES), lambda h, c: (c, gb)),
                  pl.BlockSpec((None, ML_CONV, 2 * ML_DQK_PAD), lambda h, c: (h, 0, 0)),
                  pl.BlockSpec((1, LANES), lambda h, c: (0, 0)),
                  pl.BlockSpec((1, ML_DV), lambda h, c: (0, h))],
        out_specs=pl.BlockSpec((rows, ML_DV), lambda h, c: (c, h)),
        out_shape=jax.ShapeDtypeStruct((t, ML_V_W), MXU_DTYPE),
        scratch_shapes=[pltpu.VMEM((rows + 8, 2 * ML_DQK_PAD), F32),
                        pltpu.VMEM((ML_DQK_PAD, ML_DV + LANES), F32)],
        compiler_params=_params("parallel", "arbitrary"),
        name="mlstm",
    )(u, u, u, u, u, conv_w, gate_b, norm_g.reshape(1, ML_V_W))


SUBLANES = 8


def _pair_level_table(rows):
    t = np.arange(rows)[:, None]
    s = np.arange(rows)[None, :]
    x = np.bitwise_xor(t, s)
    lvl = np.floor(np.log2(np.maximum(x, 1))).astype(np.int32)
    return jnp.asarray(np.where(t > s, lvl, -1).astype(np.int32))


def _block_sums(log_f, rows):
    sub = lax.broadcasted_iota(jnp.int32, log_f.shape, 0)
    groups = rows // SUBLANES

    def row_of_group(x, j):
        x3 = x.reshape(groups, SUBLANES, x.shape[-1])
        return jnp.broadcast_to(x3[:, j:j + 1, :], x3.shape).reshape(x.shape)

    c = log_f
    e = jnp.zeros_like(log_f)
    out = [(c, e)]
    odd = (sub & 1) == 1
    c, e = (c + jnp.where(odd, pltpu.roll(c, 1, 0), 0.0),
            e + jnp.where(odd, 0.0, pltpu.roll(c, rows - 1, 0)))
    out.append((c, e))
    r8 = sub & 7
    c, e = (c + jnp.where((r8 == 2) | (r8 == 3), row_of_group(c, 1),
                          jnp.where((r8 == 6) | (r8 == 7), row_of_group(c, 5), 0.0)),
            e + jnp.where((r8 == 0) | (r8 == 1), row_of_group(c, 3),
                          jnp.where((r8 == 4) | (r8 == 5), row_of_group(c, 7), 0.0)))
    out.append((c, e))
    c, e = (c + jnp.where(r8 >= 4, row_of_group(c, 3), 0.0),
            e + jnp.where(r8 < 4, row_of_group(c, 7), 0.0))
    out.append((c, e))
    m = SUBLANES
    while m < rows:
        cs, es = [], []
        for p in range(rows // (2 * m)):
            lo, mid, hi = 2 * m * p, 2 * m * p + m, 2 * m * (p + 1)
            cs += [c[lo:mid], c[mid:hi] + c[mid - 1:mid]]
            es += [e[lo:mid] + c[hi - 1:hi], e[mid:hi]]
        c, e = jnp.concatenate(cs, axis=0), jnp.concatenate(es, axis=0)
        out.append((c, e))
        m *= 2
    return out


def _hgrn2_kernel(q_ref, f_ref, i_ref, g_ref, lb_ref, ng_ref, lvl_ref, out_ref, st_ref, *, rows):
    @pl.when(pl.program_id(1) == 0)
    def _():
        st_ref[...] = jnp.zeros_like(st_ref)

    qp = q_ref[...]
    q = qp * _sigmoid(qp)
    fp = f_ref[...]
    v = i_ref[...]
    gp = g_ref[...]
    log_lb = lb_ref[0:1, :]
    x2 = lb_ref[1:2, :] + _log_sigmoid(fp)
    log_f = jnp.maximum(log_lb, x2) + jnp.log1p(jnp.exp(-jnp.abs(log_lb - x2)))
    k = lb_ref[2:3, :] * _sigmoid(-fp)

    sums = _block_sums(log_f, rows)
    lvl = lvl_ref[...]
    attn = jnp.zeros((rows, rows), F32)
    for level, (c_m, e_m) in enumerate(sums[:-1]):
        pairs = _dot_nt(q * jnp.exp(c_m), k * jnp.exp(e_m))
        attn = jnp.where(lvl == level, pairs, attn)
    b, after = sums[-1]

    st = st_ref[...]
    o = (_dot(attn, v) + jnp.sum(q * k, axis=-1, keepdims=True) * v
         + _dot_nt(q * jnp.exp(b), st))
    st_ref[...] = st * jnp.exp(b[rows - 1:rows, :]) + _dot_tn(v, k * jnp.exp(after))

    ms = jnp.mean(o * o, axis=-1, keepdims=True)
    y = o * lax.rsqrt(ms + EPS) * ng_ref[...]
    out_ref[...] = (y * (gp * _sigmoid(gp))).astype(out_ref.dtype)


def _hgrn2(u, lb_tab, norm_g, *, rows):
    t = u.shape[0]
    qb, fb = OFF_HG_Q // HG_DK, OFF_HG_F // HG_DK
    ib, gb = OFF_HG_I // HG_DV, OFF_HG_G // HG_DV
    return pl.pallas_call(
        functools.partial(_hgrn2_kernel, rows=rows),
        grid=(HG_HEADS, t // rows),
        in_specs=[pl.BlockSpec((rows, HG_DK), lambda h, c: (c, qb + h)),
                  pl.BlockSpec((rows, HG_DK), lambda h, c: (c, fb + h)),
                  pl.BlockSpec((rows, HG_DV), lambda h, c: (c, ib + h)),
                  pl.BlockSpec((rows, HG_DV), lambda h, c: (c, gb + h)),
                  pl.BlockSpec((3, HG_DK), lambda h, c: (0, h)),
                  pl.BlockSpec((1, HG_DV), lambda h, c: (0, h)),
                  pl.BlockSpec((rows, rows), lambda h, c: (0, 0))],
        out_specs=pl.BlockSpec((rows, HG_DV), lambda h, c: (c, h)),
        out_shape=jax.ShapeDtypeStruct((t, HG_V_W), MXU_DTYPE),
        scratch_shapes=[pltpu.VMEM((HG_DV, HG_DK), F32)],
        compiler_params=_params("parallel", "arbitrary"),
        name="hgrn2",
    )(u, u, u, u, lb_tab, norm_g.reshape(1, HG_V_W), _pair_level_table(rows))


def _rope_slab(u, cc, sa, sb, half):
    return u * cc + pltpu.roll(u, LANES - half, 1) * sa + pltpu.roll(u, half, 1) * sb


def _rope_coeffs(ang, half):
    lane = lax.broadcasted_iota(jnp.int32, ang.shape, 1)
    cos, sin = jnp.cos(ang), jnp.sin(ang)
    cc = jnp.where(lane < 2 * half, cos, 1.0)
    sa = jnp.where(lane < half, -sin, 0.0)
    sb = jnp.where((lane >= half) & (lane < 2 * half), sin, 0.0)
    return cc, sa, sb


def _dsa_prep_kernel(cq_ref, k_ref, v_ref, idx_ref, pos_ref, g_ref, w_ref, fr_ref,
                     q_out, qi_out, k_out, v_out, ki_out, wi_out):
    pos = pos_ref[...].astype(F32)
    cq = cq_ref[...]
    ms = jnp.mean(cq * cq, axis=-1, keepdims=True)
    hq = cq * lax.rsqrt(ms + EPS) * g_ref[...]
    q_all = _dot(hq, w_ref[...])

    half_a = DSA_DH // ROPE_FRACTION // 2
    half_i = IDX_DH // ROPE_FRACTION // 2
    ca = _rope_coeffs(pos * fr_ref[0:1, :], half_a)
    ci = _rope_coeffs(pos * fr_ref[1:2, :], half_i)

    kk = k_ref[...]
    for h in range(DSA_HEADS):
        sl = slice(h * DSA_DH, (h + 1) * DSA_DH)
        q_out[:, sl] = (_rope_slab(q_all[:, sl], *ca, half_a) * Q_SCALE).astype(q_out.dtype)
        k_out[:, sl] = _rope_slab(kk[:, sl], *ca, half_a).astype(k_out.dtype)
    for h in range(IDX_HEADS):
        src = slice(DSA_W + h * IDX_SLOT, DSA_W + (h + 1) * IDX_SLOT)
        qi_out[h] = _rope_slab(q_all[:, src], *ci, half_i).astype(qi_out.dtype)
    v_out[...] = v_ref[...].astype(v_out.dtype)

    idx = idx_ref[...]
    lane = lax.broadcasted_iota(jnp.int32, idx.shape, 1)
    ki = _rope_slab(jnp.where(lane < IDX_DH, idx, 0.0), *ci, half_i)
    ki_out[...] = ki.astype(ki_out.dtype)
    wi = pltpu.roll(idx, LANES - IDX_DH, 1) * (IDX_HEADS ** -0.5 * IDX_DH ** -0.5)
    wi_out[...] = jnp.where(lane < IDX_HEADS, wi, 0.0)


def _dsa_prep(u, pos, q_norm_g, w_uq_pad, freqs, *, rows):
    t = u.shape[0]
    nq = DSA_W + IDX_HEADS * IDX_SLOT
    outs = (jax.ShapeDtypeStruct((t, DSA_W), MXU_DTYPE),
            jax.ShapeDtypeStruct((IDX_HEADS, t, IDX_SLOT), MXU_DTYPE),
            jax.ShapeDtypeStruct((t, DSA_W), MXU_DTYPE),
            jax.ShapeDtypeStruct((t, DSA_W), MXU_DTYPE),
            jax.ShapeDtypeStruct((t, IDX_SLOT), MXU_DTYPE),
            jax.ShapeDtypeStruct((t, LANES), F32))
    row_spec = lambda w, blk: pl.BlockSpec((rows, w), lambda i: (i, blk))
    return pl.pallas_call(
        _dsa_prep_kernel,
        grid=(t // rows,),
        in_specs=[row_spec(DSA_Q_RANK, OFF_DSA_CQ // DSA_Q_RANK),
                  row_spec(DSA_W, OFF_DSA_K // DSA_W),
                  row_spec(DSA_W, OFF_DSA_V // DSA_W),
                  row_spec(LANES, OFF_IDX // LANES),
                  pl.BlockSpec((rows, 1), lambda i: (i, 0)),
                  pl.BlockSpec((1, DSA_Q_RANK), lambda i: (0, 0)),
                  pl.BlockSpec((DSA_Q_RANK, nq), lambda i: (0, 0)),
                  pl.BlockSpec((2, LANES), lambda i: (0, 0))],
        out_specs=(row_spec(DSA_W, 0),
                   pl.BlockSpec((IDX_HEADS, rows, IDX_SLOT), lambda i: (0, i, 0)),
                   row_spec(DSA_W, 0), row_spec(DSA_W, 0), row_spec(IDX_SLOT, 0), row_spec(LANES, 0)),
        out_shape=outs,
        compiler_params=_params("parallel"),
        name="dsa_prep",
    )(u, u, u, u, pos, q_norm_g.reshape(1, DSA_Q_RANK), w_uq_pad, freqs)


IDX_TQ = 256
IDX_TK = 512
HALF_BIAS = 1 << 15


def _indexer_kernel(qi_ref, wi_ref, ki_ref, incl_ref, bias_ref, key_ref, hi_ref, lo_ref, *, topk):
    tq, tk = IDX_TQ, IDX_TK
    qb = pl.program_id(0)
    nkb = ((qb + 1) * tq + tk - 1) // tk
    q_all = qi_ref[...].reshape(IDX_HEADS * tq, IDX_SLOT)
    w = wi_ref[...]
    w_col = jnp.concatenate([w[:, h:h + 1] for h in range(IDX_HEADS)], axis=0)
    row_chunk = (qb * tq + lax.broadcasted_iota(jnp.int32, (tq, 1), 0)) // CHUNK
    col_in_blk = lax.broadcasted_iota(jnp.int32, (1, tk), 1)

    def score_body(kb, carry):
        kt = ki_ref[pl.ds(pl.multiple_of(kb * tk, tk), tk), :]
        weighted = jnp.maximum(_dot_nt(q_all, kt), 0.0) * w_col
        parts = [weighted[h * tq:(h + 1) * tq] for h in range(IDX_HEADS)]
        while len(parts) > 1:
            parts = [a + b for a, b in zip(parts[0::2], parts[1::2])]
        col_chunk = (kb * tk + col_in_blk) // CHUNK
        score = jnp.where(col_chunk <= row_chunk, parts[0], -jnp.inf)
        bits = pltpu.bitcast(score, jnp.int32)
        bits = jnp.where(bits == INT_MIN, 0, bits)
        key = jnp.where(bits < 0, bits ^ 0x7FFFFFFF, bits)
        key_ref[kb] = key
        hi_ref[kb] = jnp.right_shift(key, 16).astype(jnp.int16)
        lo_ref[kb] = ((key & 0xFFFF) - HALF_BIAS).astype(jnp.int16)
        return carry

    lax.fori_loop(0, nkb, score_body, 0)

    ones_mat = jnp.ones((LANES, LANES), MXU_DTYPE)
    one16, zero16 = jnp.int16(1), jnp.int16(0)

    def count16(ref, cand, strict=False):
        cand16 = cand.astype(jnp.int16)

        def body(kb, cnt):
            for j in range(tk // LANES):
                blk = ref[kb, :, j * LANES:(j + 1) * LANES]
                cnt += jnp.where(blk > cand16 if strict else blk >= cand16, one16, zero16)
            return cnt
        cnt = lax.fori_loop(0, nkb, body, jnp.zeros((tq, LANES), jnp.int16))
        return jnp.dot(cnt.astype(jnp.int32).astype(F32).astype(MXU_DTYPE), ones_mat, preferred_element_type=F32)

    def bisect16(ref, base_count):
        zero = jnp.zeros((tq, LANES), jnp.int32)
        v = jnp.where(base_count + count16(ref, zero) >= topk, zero, -HALF_BIAS)

        def bit_body(i, v):
            cand = v + jnp.left_shift(jnp.int32(1), 14 - i)
            return jnp.where(base_count + count16(ref, cand) >= topk, cand, v)
        return lax.fori_loop(0, 15, bit_body, v)

    thr_hi = bisect16(hi_ref, 0.0)
    above = count16(hi_ref, thr_hi, strict=True)
    thr_hi16 = thr_hi.astype(jnp.int16)

    def bucket_body(kb, carry):
        for j in range(tk // LANES):
            sl = slice(j * LANES, (j + 1) * LANES)
            lo_ref[kb, :, sl] = jnp.where(hi_ref[kb, :, sl] == thr_hi16, lo_ref[kb, :, sl], jnp.int16(-HALF_BIAS))
        return carry

    lax.fori_loop(0, nkb, bucket_body, 0)
    thr_lo = bisect16(lo_ref, above)
    need = topk - (above + count16(lo_ref, thr_lo, strict=True))[:, 0:1]
    thr = (jnp.left_shift(thr_hi, 16) | (thr_lo + HALF_BIAS))[:, 0:1]

    bias_ref[...] = jnp.full(bias_ref.shape, NEG_BIG, bias_ref.dtype)

    def emit_body(kb, seen):
        key = key_ref[kb]
        eq = key == thr
        eq_f = jnp.where(eq, 1.0, 0.0)
        rank = seen + jnp.dot(eq_f.astype(MXU_DTYPE), incl_ref[...], preferred_element_type=F32)
        take = ((key > thr) | (eq & (rank <= need))) & (key > KEY_NEG_INF)
        bias_ref[:, pl.ds(pl.multiple_of(kb * tk, tk), tk)] = jnp.where(take, 0.0, NEG_BIG).astype(bias_ref.dtype)
        return seen + jnp.sum(eq_f, axis=1, keepdims=True)

    lax.fori_loop(0, nkb, emit_body, jnp.zeros((tq, 1), F32))


def _indexer(qi, wi, ki, *, topk):
    t = ki.shape[0]
    incl = jnp.asarray(np.triu(np.ones((IDX_TK, IDX_TK), np.float32)), MXU_DTYPE)
    return pl.pallas_call(
        functools.partial(_indexer_kernel, topk=topk),
        grid=(t // IDX_TQ,),
        in_specs=[pl.BlockSpec((IDX_HEADS, IDX_TQ, IDX_SLOT), lambda i: (0, i, 0)),
                  pl.BlockSpec((IDX_TQ, LANES), lambda i: (i, 0)),
                  pl.BlockSpec((t, IDX_SLOT), lambda i: (0, 0)),
                  pl.BlockSpec((IDX_TK, IDX_TK), lambda i: (0, 0))],
        out_specs=pl.BlockSpec((IDX_TQ, t), lambda i: (i, 0)),
        out_shape=jax.ShapeDtypeStruct((t, t), BF16),
        scratch_shapes=[pltpu.VMEM((t // IDX_TK, IDX_TQ, IDX_TK), jnp.int32),
                        pltpu.VMEM((t // IDX_TK, IDX_TQ, IDX_TK), jnp.int16),
                        pltpu.VMEM((t // IDX_TK, IDX_TQ, IDX_TK), jnp.int16)],
        compiler_params=_params("parallel"),
        name="dsa_indexer",
    )(qi, wi, ki, incl)


def _attn_kernel(q_ref, k_ref, v_ref, bias_ref, o_ref, m_ref, l_ref, acc_ref, *, tq, tk):
    qb, kb = pl.program_id(0), pl.program_id(1)
    last = ((qb + 1) * tq - 1) // tk

    @pl.when(kb == 0)
    def _():
        m_ref[...] = jnp.full(m_ref.shape, NEG_BIG, F32)
        l_ref[...] = jnp.zeros_like(l_ref)
        acc_ref[...] = jnp.zeros_like(acc_ref)

    @pl.when(kb <= last)
    def _():
        bias = bias_ref[...].astype(F32)
        for h in range(DSA_HEADS):
            sl = slice(h * DSA_DH, (h + 1) * DSA_DH)
            s = _dot_nt(q_ref[:, sl], k_ref[:, sl]) + bias
            m_old = m_ref[h]
            m_new = jnp.maximum(m_old, jnp.max(s, axis=-1, keepdims=True))
            alpha = jnp.exp2(m_old - m_new)
            p = jnp.exp2(s - jnp.concatenate([m_new] * (tk // LANES), axis=1))
            l_ref[h] = alpha * l_ref[h] + jnp.sum(p, axis=-1, keepdims=True)
            acc_ref[:, sl] = alpha * acc_ref[:, sl] + _dot(p, v_ref[:, sl])
            m_ref[h] = m_new

    @pl.when(kb == last)
    def _():
        for h in range(DSA_HEADS):
            sl = slice(h * DSA_DH, (h + 1) * DSA_DH)
            o_ref[:, sl] = (acc_ref[:, sl] / l_ref[h]).astype(o_ref.dtype)


def _attention(q, k, v, bias, *, tq, tk):
    t = q.shape[0]
    last = lambda i: ((i + 1) * tq - 1) // tk
    return pl.pallas_call(
        functools.partial(_attn_kernel, tq=tq, tk=tk),
        grid=(t // tq, t // tk),
        in_specs=[pl.BlockSpec((tq, DSA_W), lambda i, j: (i, 0)),
                  pl.BlockSpec((tk, DSA_W), lambda i, j: (jnp.minimum(j, last(i)), 0)),
                  pl.BlockSpec((tk, DSA_W), lambda i, j: (jnp.minimum(j, last(i)), 0)),
                  pl.BlockSpec((tq, tk), lambda i, j: (i, jnp.minimum(j, last(i))))],
        out_specs=pl.BlockSpec((tq, DSA_W), lambda i, j: (i, 0)),
        out_shape=jax.ShapeDtypeStruct((t, DSA_W), MXU_DTYPE),
        scratch_shapes=[pltpu.VMEM((DSA_HEADS, tq, LANES), F32),
                        pltpu.VMEM((DSA_HEADS, tq, LANES), F32),
                        pltpu.VMEM((tq, DSA_W), F32)],
        compiler_params=_params("parallel", "arbitrary"),
        name="dsa_attention",
    )(q, k, v, bias)


def _pad_heads(w, heads, width, padded):
    lead = w.shape[:-1]
    w = w.reshape(lead + (heads, width))
    w = jnp.pad(w, [(0, 0)] * len(lead) + [(0, 0), (0, padded - width)])
    return w.reshape(lead + (heads * padded,))


def _pack_plan():
    src, start = {}, 0
    names = ("ml_q", "ml_k", "ml_v", "ml_i", "ml_f", "ml_o", "dsa_cq", "dsa_k", "dsa_v", "idx_k", "idx_w",
             "hg_q", "hg_f", "hg_i", "hg_g")
    for name, width in zip(names, IN_SPLITS):
        src[name] = start
        start += width
    copies, zeros = [], []
    for h in range(ML_HEADS):
        for dst0, name in ((OFF_ML_Q, "ml_q"), (OFF_ML_K, "ml_k")):
            copies.append((dst0 + h * ML_DQK_PAD, src[name] + h * ML_DQK, ML_DQK))
            zeros.append((dst0 + h * ML_DQK_PAD + ML_DQK, ML_DQK_PAD - ML_DQK))
    copies += [(OFF_DSA_K, src["dsa_k"], DSA_W), (OFF_ML_V, src["ml_v"], ML_V_W), (OFF_ML_O, src["ml_o"], ML_V_W),
               (OFF_HG_Q, src["hg_q"], HG_K_W), (OFF_HG_F, src["hg_f"], HG_K_W),
               (OFF_HG_I, src["hg_i"], HG_V_W), (OFF_HG_G, src["hg_g"], HG_V_W),
               (OFF_DSA_V, src["dsa_v"], DSA_W),
               (OFF_GATES, src["ml_i"], 2 * ML_HEADS),
               (OFF_DSA_CQ, src["dsa_cq"], DSA_Q_RANK),
               (OFF_IDX, src["idx_k"], IDX_DH + IDX_HEADS)]
    zeros += [(OFF_GATES + 2 * ML_HEADS, LANES - 2 * ML_HEADS),
              (OFF_IDX + IDX_DH + IDX_HEADS, D_IN_PAD - OFF_IDX - IDX_DH - IDX_HEADS)]
    return copies, zeros


def _pack_kernel(w_ref, o_ref):
    copies, zeros = _pack_plan()
    rows = o_ref.shape[0]
    for dst, src, width in copies:
        o_ref[:, dst:dst + width] = w_ref[:, src:src + width].astype(o_ref.dtype)
    for dst, width in zeros:
        o_ref[:, dst:dst + width] = jnp.zeros((rows, width), o_ref.dtype)


def _pack_w_in(w_in, *, rows):
    depth, d, n = w_in.shape
    return pl.pallas_call(
        _pack_kernel,
        grid=(depth, d // rows),
        in_specs=[pl.BlockSpec((None, rows, n), lambda l, i: (l, i, 0))],
        out_specs=pl.BlockSpec((None, rows, D_IN_PAD), lambda l, i: (l, i, 0)),
        out_shape=jax.ShapeDtypeStruct((depth, d, D_IN_PAD), MXU_DTYPE),
        compiler_params=_params("parallel", "parallel"),
        name="pack_w_in",
    )(w_in)


def _rope_freqs():
    def lanes(d):
        rot = d // ROPE_FRACTION
        half = rot // 2
        inv = jnp.power(ROPE_THETA, -jnp.arange(half, dtype=F32) * (2.0 / rot))
        return jnp.concatenate([inv, inv, jnp.zeros((LANES - rot,), F32)])
    return jnp.stack([lanes(DSA_DH), lanes(IDX_DH)])


def kernel(x, positions, ln_mix_g, w_in, ml_conv_w, ml_gate_b, ml_norm_g, dsa_q_norm_g, dsa_w_uq,
           hg_lb_logits, hg_norm_g, w_out, ln_mlp_g, w_up, w_down, ln_final_g):
    bsz, t, d = x.shape
    assert bsz == 1 and t % 512 == 0 and d == D_MODEL
    depth = w_in.shape[0]
    topk = min(TOPK_MAX, t // 4)
    xs = x.reshape(t, d)
    pos = positions.reshape(t, 1)

    w_in_p = _pack_w_in(w_in, rows=128)
    w_out_b = w_out.astype(MXU_DTYPE)
    w_up_b = w_up.astype(MXU_DTYPE)
    w_down_b = w_down.astype(MXU_DTYPE)
    conv_q = _pad_heads(ml_conv_w[..., :ML_QK_W], ML_HEADS, ML_DQK, ML_DQK_PAD)
    conv_k = _pad_heads(ml_conv_w[..., ML_QK_W:], ML_HEADS, ML_DQK, ML_DQK_PAD)
    conv_p = jnp.concatenate([conv_q.reshape(depth, ML_CONV, ML_HEADS, ML_DQK_PAD),
                              conv_k.reshape(depth, ML_CONV, ML_HEADS, ML_DQK_PAD)], axis=-1)
    conv_p = jnp.transpose(conv_p, (0, 2, 1, 3))
    gate_b = jnp.pad(ml_gate_b.reshape(depth, 1, 2 * ML_HEADS), ((0, 0), (0, 0), (0, LANES - 2 * ML_HEADS)))
    w_uq_p = jnp.concatenate([dsa_w_uq[..., :DSA_W],
                              _pad_heads(dsa_w_uq[..., DSA_W:], IDX_HEADS, IDX_DH, IDX_SLOT)],
                             axis=-1).astype(MXU_DTYPE)
    freqs = _rope_freqs()
    lb_cum = jnp.cumsum(jax.nn.softmax(hg_lb_logits.astype(F32), axis=0), axis=0)
    lb = lb_cum - lb_cum[:1]
    lb_tab = jnp.stack([jnp.log(lb), jnp.log1p(-lb), 1.0 - lb], axis=1)

    for layer in range(depth):
        u = _norm_matmul(xs, ln_mix_g[layer], w_in_p[layer], tm=512, tn=1024, act=False, out_dtype=F32)
        y_a = _mlstm(u, conv_p[layer], gate_b[layer], ml_norm_g[layer], rows=256)
        q_r, qi_r, k_r, v_b, ki_r, wi = _dsa_prep(u, pos, dsa_q_norm_g[layer], w_uq_p[layer], freqs, rows=256)
        bias = _indexer(qi_r, wi, ki_r, topk=topk)
        y_b = _attention(q_r, k_r, v_b, bias, tq=512, tk=1024)
        y_c = _hgrn2(u, lb_tab[layer], hg_norm_g[layer], rows=256)
        xs = _mix_out(y_a, y_b, y_c, w_out_b[layer], xs, tm=512, tn=1024)
        a = _norm_matmul(xs, ln_mlp_g[layer], w_up_b[layer], tm=512, tn=1024, act=True, out_dtype=MXU_DTYPE)
        xs = _matmul_res(a, w_down_b[layer], xs, tm=1024, tn=1024, tk=2048)
    return _final_norm(xs, ln_final_g, tm=256).reshape(bsz, t, d)
```

```python
import functools

import jax
import jax.numpy as jnp
import numpy as np
from jax import lax
from jax.experimental import pallas as pl
from jax.experimental.pallas import tpu as pltpu

F32 = jnp.float32
BF16 = jnp.bfloat16
MXU_DTYPE = jnp.bfloat16

D_MODEL = 4096
DEPTH = 4
CHUNK = 64
EPS = 1e-6
ROPE_THETA = 500000.0
ROPE_FRACTION = 4
ML_HEADS = 4
ML_DV = 384
ML_DQK = ML_DV // 2
ML_DQK_PAD = 256
ML_CONV = 4
ML_GATE_CAP = 15.0
DSA_HEADS = 8
DSA_DH = 128
DSA_Q_RANK = 384
IDX_HEADS = 8
IDX_DH = 64
IDX_SLOT = 128
TOPK_MAX = 256
HG_HEADS = 12
HG_DK = 128
HG_DV = 128
D_FF = 4 * D_MODEL

ML_QK_W = ML_HEADS * ML_DQK
ML_V_W = ML_HEADS * ML_DV
DSA_W = DSA_HEADS * DSA_DH
IDX_W = IDX_HEADS * IDX_DH
HG_K_W = HG_HEADS * HG_DK
HG_V_W = HG_HEADS * HG_DV
D_MIX = ML_V_W + DSA_W + HG_V_W
IN_SPLITS = (ML_QK_W, ML_QK_W, ML_V_W, ML_HEADS, ML_HEADS, ML_V_W,
             DSA_Q_RANK, DSA_W, DSA_W, IDX_DH, IDX_HEADS,
             HG_K_W, HG_K_W, HG_V_W, HG_V_W)

LANES = 128
VMEM_LIMIT = 56 * 1024 * 1024

OFF_ML_Q = 0
OFF_ML_K = 1024
OFF_DSA_K = 2048
OFF_ML_V = 3072
OFF_ML_O = 4608
OFF_HG_Q = 6144
OFF_HG_F = 7680
OFF_HG_I = 9216
OFF_HG_G = 10752
OFF_DSA_V = 12288
OFF_GATES = 13312
OFF_DSA_CQ = 13440
OFF_IDX = 13824
D_IN_PAD = 14336

Q_SCALE = DSA_DH ** -0.5 * 1.4426950408889634

INT_MIN = -2 ** 31
KEY_NEG_INF = -2139095041
NEG_BIG = -1e30


def _mxu(a):
    return a.astype(MXU_DTYPE)


def _dot(a, b):
    return jnp.dot(_mxu(a), _mxu(b), preferred_element_type=F32)


def _dot_nt(a, b):
    return lax.dot_general(_mxu(a), _mxu(b), (((1,), (1,)), ((), ())), preferred_element_type=F32)


def _dot_tn(a, b):
    return lax.dot_general(_mxu(a), _mxu(b), (((0,), (0,)), ((), ())), preferred_element_type=F32)


def _dot_f32(a, b):
    return jnp.dot(a, b, precision=lax.Precision.HIGHEST, preferred_element_type=F32)


def _sigmoid(x):
    return 1.0 / (1.0 + jnp.exp(-x))


def _log_sigmoid(x):
    return jnp.minimum(x, 0.0) - jnp.log1p(jnp.exp(-jnp.abs(x)))


def _params(*sem):
    return pltpu.CompilerParams(dimension_semantics=sem, vmem_limit_bytes=VMEM_LIMIT)


def _norm_matmul_kernel(x_ref, g_ref, w_ref, o_ref, h_ref, *, act):
    @pl.when(pl.program_id(1) == 0)
    def _():
        x = x_ref[...]
        ms = jnp.mean(x * x, axis=-1, keepdims=True)
        h_ref[...] = (x * lax.rsqrt(ms + EPS) * g_ref[...]).astype(h_ref.dtype)

    y = jnp.dot(h_ref[...], w_ref[...], preferred_element_type=F32)
    if act:
        y = jnp.square(jnp.maximum(y, 0.0))
    o_ref[...] = y.astype(o_ref.dtype)


def _norm_matmul(x, g, w, layer, *, tm, tn, act, out_dtype):
    m, k = x.shape
    n = w.shape[2]
    return pl.pallas_call(
        functools.partial(_norm_matmul_kernel, act=act),
        grid=(m // tm, n // tn),
        in_specs=[pl.BlockSpec((tm, k), lambda i, j: (i, 0)),
                  pl.BlockSpec((1, k), lambda i, j: (0, 0)),
                  pl.BlockSpec((None, k, tn), lambda i, j: (layer, 0, j))],
        out_specs=pl.BlockSpec((tm, tn), lambda i, j: (i, j)),
        out_shape=jax.ShapeDtypeStruct((m, n), out_dtype),
        scratch_shapes=[pltpu.VMEM((tm, k), MXU_DTYPE)],
        compiler_params=_params("parallel", "arbitrary"),
        name="norm_matmul",
    )(x, g.reshape(1, k), w)


def _matmul_res_kernel(a_ref, w_ref, r_ref, o_ref):
    part = jnp.dot(a_ref[...], w_ref[...], preferred_element_type=F32)

    @pl.when(pl.program_id(2) == 0)
    def _():
        o_ref[...] = r_ref[...] + part

    @pl.when(pl.program_id(2) != 0)
    def _():
        o_ref[...] += part


def _matmul_res(a, w, layer, res, *, tm, tn, tk):
    m, k = a.shape
    n = w.shape[2]
    return pl.pallas_call(
        _matmul_res_kernel,
        grid=(m // tm, n // tn, k // tk),
        in_specs=[pl.BlockSpec((tm, tk), lambda i, j, kk: (i, kk)),
                  pl.BlockSpec((None, tk, tn), lambda i, j, kk: (layer, kk, j)),
                  pl.BlockSpec((tm, tn), lambda i, j, kk: (i, j))],
        out_specs=pl.BlockSpec((tm, tn), lambda i, j, kk: (i, j)),
        out_shape=jax.ShapeDtypeStruct((m, n), F32),
        compiler_params=_params("parallel", "parallel", "arbitrary"),
        name="matmul_res",
    )(a, w, res)


def _mix_out_kernel(ya_ref, yb_ref, yc_ref, w_ref, r_ref, o_ref):
    acc = r_ref[...]
    acc += jnp.dot(ya_ref[...], w_ref[0:ML_V_W, :], preferred_element_type=F32)
    acc += jnp.dot(yb_ref[...], w_ref[ML_V_W:ML_V_W + DSA_W, :], preferred_element_type=F32)
    acc += jnp.dot(yc_ref[...], w_ref[ML_V_W + DSA_W:D_MIX, :], preferred_element_type=F32)
    o_ref[...] = acc


def _mix_out(ya, yb, yc, w, layer, res, *, tm, tn):
    m = ya.shape[0]
    n = w.shape[2]
    return pl.pallas_call(
        _mix_out_kernel,
        grid=(m // tm, n // tn),
        in_specs=[pl.BlockSpec((tm, ML_V_W), lambda i, j: (i, 0)),
                  pl.BlockSpec((tm, DSA_W), lambda i, j: (i, 0)),
                  pl.BlockSpec((tm, HG_V_W), lambda i, j: (i, 0)),
                  pl.BlockSpec((None, D_MIX, tn), lambda i, j: (layer, 0, j)),
                  pl.BlockSpec((tm, tn), lambda i, j: (i, j))],
        out_specs=pl.BlockSpec((tm, tn), lambda i, j: (i, j)),
        out_shape=jax.ShapeDtypeStruct((m, n), F32),
        compiler_params=_params("parallel", "parallel"),
        name="mix_out",
    )(ya, yb, yc, w, res)


def _final_norm_kernel(x_ref, g_ref, o_ref):
    x = x_ref[...]
    ms = jnp.mean(x * x, axis=-1, keepdims=True)
    o_ref[...] = x * lax.rsqrt(ms + EPS) * g_ref[...]


def _final_norm(x, g, *, tm):
    m, k = x.shape
    return pl.pallas_call(
        _final_norm_kernel,
        grid=(m // tm,),
        in_specs=[pl.BlockSpec((tm, k), lambda i: (i, 0)),
                  pl.BlockSpec((1, k), lambda i: (0, 0))],
        out_specs=pl.BlockSpec((tm, k), lambda i: (i, 0)),
        out_shape=jax.ShapeDtypeStruct((m, k), F32),
        compiler_params=_params("parallel"),
        name="final_norm",
    )(x, g.reshape(1, k))


def _mlstm_kernel(q_ref, k_ref, v_ref, o_ref, gt_ref, cw_ref, gb_ref, ng_ref, out_ref,
                  xbuf, c_ref, *, rows):
    head = pl.program_id(0)

    @pl.when(pl.program_id(1) == 0)
    def _():
        xbuf[0:8, :] = jnp.zeros((8, 2 * ML_DQK_PAD), F32)
        c_ref[...] = jnp.zeros_like(c_ref)

    xbuf[8:8 + rows, 0:ML_DQK_PAD] = q_ref[...]
    xbuf[8:8 + rows, ML_DQK_PAD:] = k_ref[...]
    cw = cw_ref[...]
    acc = xbuf[8:8 + rows, :] * cw[ML_CONV - 1:ML_CONV, :]
    for j in range(1, ML_CONV):
        acc += xbuf[8 - j:8 - j + rows, :] * cw[ML_CONV - 1 - j:ML_CONV - j, :]
    xbuf[0:8, :] = xbuf[rows:rows + 8, :]
    qk = acc * _sigmoid(acc)
    q = qk[:, 0:ML_DQK_PAD]
    k = qk[:, ML_DQK_PAD:] * (ML_DQK ** -0.5)

    capped = ML_GATE_CAP * jnp.tanh((gt_ref[...] + gb_ref[...]) * (1.0 / ML_GATE_CAP))
    lsig = _log_sigmoid(capped)
    r_i = lax.broadcasted_iota(jnp.int32, (rows, rows), 0)
    c_i = lax.broadcasted_iota(jnp.int32, (rows, rows), 1)
    causal = c_i <= r_i
    b_all = _dot_f32(jnp.where(causal, 1.0, 0.0), lsig)
    lane = lax.broadcasted_iota(jnp.int32, (rows, LANES), 1)
    ig_col = jnp.sum(jnp.where(lane == head, capped, 0.0), axis=1, keepdims=True)
    b_col = jnp.sum(jnp.where(lane == head + ML_HEADS, b_all, 0.0), axis=1, keepdims=True)
    sub = lax.broadcasted_iota(jnp.int32, (LANES, rows), 0)
    ig_row = jnp.sum(jnp.where(sub == head, capped.T, 0.0), axis=0, keepdims=True)
    b_row = jnp.sum(jnp.where(sub == head + ML_HEADS, b_all.T, 0.0), axis=0, keepdims=True)

    dmat = jnp.exp(jnp.where(causal, b_col - b_row + ig_row, -jnp.inf))
    s = _dot_nt(q, k) * dmat
    one_col = jnp.where(lane == 0, 1.0, 0.0)
    v_ext = jnp.concatenate([v_ref[...], one_col], axis=1)
    c_old = c_ref[...]
    num_ext = _dot(s, v_ext) + jnp.exp(b_col) * _dot(q, c_old)
    num = num_ext[:, 0:ML_DV]
    den = num_ext[:, ML_DV:ML_DV + 1]
    hh = num / jnp.maximum(jnp.abs(den), 1.0)

    b_last = b_col[rows - 1:rows, :]
    w_s = jnp.exp(b_last - b_col + ig_col)
    c_ref[...] = jnp.exp(b_last) * c_old + _dot_tn(k, w_s * v_ext)

    ms = jnp.mean(hh * hh, axis=-1, keepdims=True)
    y = hh * lax.rsqrt(ms + EPS) * ng_ref[...]
    out_ref[...] = (_sigmoid(o_ref[...]) * y).astype(out_ref.dtype)


def _mlstm(u, conv_w, gate_b, norm_g, *, rows):
    t = u.shape[0]
    qb, kb = OFF_ML_Q // ML_DQK_PAD, OFF_ML_K // ML_DQK_PAD
    vb, ob = OFF_ML_V // ML_DV, OFF_ML_O // ML_DV
    gb = OFF_GATES // LANES
    return pl.pallas_call(
        functools.partial(_mlstm_kernel, rows=rows),
        grid=(ML_HEADS, t // rows),
        in_specs=[pl.BlockSpec((rows, ML_DQK_PAD), lambda h, c: (c, qb + h)),
                  pl.BlockSpec((rows, ML_DQK_PAD), lambda h, c: (c, kb + h)),
                  pl.BlockSpec((rows, ML_DV), lambda h, c: (c, vb + h)),
                  pl.BlockSpec((rows, ML_DV), lambda h, c: (c, ob + h)),
                  pl.BlockSpec((rows, LANES), lambda h, c: (c, gb)),
                  pl.BlockSpec((None, ML_CONV, 2 * ML_DQK_PAD), lambda h, c: (h, 0, 0)),
                  pl.BlockSpec((1, LANES), lambda h, c: (0, 0)),
                  pl.BlockSpec((1, ML_DV), lambda h, c: (0, h))],
        out_specs=pl.BlockSpec((rows, ML_DV), lambda h, c: (c, h)),
        out_shape=jax.ShapeDtypeStruct((t, ML_V_W), MXU_DTYPE),
        scratch_shapes=[pltpu.VMEM((rows + 8, 2 * ML_DQK_PAD), F32),
                        pltpu.VMEM((ML_DQK_PAD, ML_DV + LANES), F32)],
        compiler_params=_params("parallel", "arbitrary"),
        name="mlstm",
    )(u, u, u, u, u, conv_w, gate_b, norm_g.reshape(1, ML_V_W))


SUBLANES = 8


def _pair_level_table(rows):
    t = np.arange(rows)[:, None]
    s = np.arange(rows)[None, :]
    x = np.bitwise_xor(t, s)
    lvl = np.floor(np.log2(np.maximum(x, 1))).astype(np.int32)
    return jnp.asarray(np.where(t > s, lvl, -1).astype(np.int32))


def _block_sums(log_f, rows):
    sub = lax.broadcasted_iota(jnp.int32, log_f.shape, 0)
    groups = rows // SUBLANES

    def row_of_group(x, j):
        x3 = x.reshape(groups, SUBLANES, x.shape[-1])
        return jnp.broadcast_to(x3[:, j:j + 1, :], x3.shape).reshape(x.shape)

    c = log_f
    e = jnp.zeros_like(log_f)
    out = [(c, e)]
    odd = (sub & 1) == 1
    c, e = (c + jnp.where(odd, pltpu.roll(c, 1, 0), 0.0),
            e + jnp.where(odd, 0.0, pltpu.roll(c, rows - 1, 0)))
    out.append((c, e))
    r8 = sub & 7
    c, e = (c + jnp.where((r8 == 2) | (r8 == 3), row_of_group(c, 1),
                          jnp.where((r8 == 6) | (r8 == 7), row_of_group(c, 5), 0.0)),
            e + jnp.where((r8 == 0) | (r8 == 1), row_of_group(c, 3),
                          jnp.where((r8 == 4) | (r8 == 5), row_of_group(c, 7), 0.0)))
    out.append((c, e))
    c, e = (c + jnp.where(r8 >= 4, row_of_group(c, 3), 0.0),
            e + jnp.where(r8 < 4, row_of_group(c, 7), 0.0))
    out.append((c, e))
    m = SUBLANES
    while m < rows:
        cs, es = [], []
        for p in range(rows // (2 * m)):
            lo, mid, hi = 2 * m * p, 2 * m * p + m, 2 * m * (p + 1)
            cs += [c[lo:mid], c[mid:hi] + c[mid - 1:mid]]
            es += [e[lo:mid] + c[hi - 1:hi], e[mid:hi]]
        c, e = jnp.concatenate(cs, axis=0), jnp.concatenate(es, axis=0)
        out.append((c, e))
        m *= 2
    return out


def _hgrn2_kernel(q_ref, f_ref, i_ref, g_ref, lb_ref, ng_ref, lvl_ref, out_ref, st_ref, *, rows):
    @pl.when(pl.program_id(1) == 0)
    def _():
        st_ref[...] = jnp.zeros_like(st_ref)

    qp = q_ref[...]
    q = qp * _sigmoid(qp)
    fp = f_ref[...]
    v = i_ref[...]
    gp = g_ref[...]
    log_lb = lb_ref[0:1, :]
    x2 = lb_ref[1:2, :] + _log_sigmoid(fp)
    log_f = jnp.maximum(log_lb, x2) + jnp.log1p(jnp.exp(-jnp.abs(log_lb - x2)))
    k = lb_ref[2:3, :] * _sigmoid(-fp)

    sums = _block_sums(log_f, rows)
    lvl = lvl_ref[...]
    attn = jnp.zeros((rows, rows), F32)
    for level, (c_m, e_m) in enumerate(sums[:-1]):
        pairs = _dot_nt(q * jnp.exp(c_m), k * jnp.exp(e_m))
        attn = jnp.where(lvl == level, pairs, attn)
    b, after = sums[-1]

    st = st_ref[...]
    o = (_dot(attn, v) + jnp.sum(q * k, axis=-1, keepdims=True) * v
         + _dot_nt(q * jnp.exp(b), st))
    st_ref[...] = st * jnp.exp(b[rows - 1:rows, :]) + _dot_tn(v, k * jnp.exp(after))

    ms = jnp.mean(o * o, axis=-1, keepdims=True)
    y = o * lax.rsqrt(ms + EPS) * ng_ref[...]
    out_ref[...] = (y * (gp * _sigmoid(gp))).astype(out_ref.dtype)


def _hgrn2(u, lb_tab, norm_g, *, rows):
    t = u.shape[0]
    qb, fb = OFF_HG_Q // HG_DK, OFF_HG_F // HG_DK
    ib, gb = OFF_HG_I // HG_DV, OFF_HG_G // HG_DV
    return pl.pallas_call(
        functools.partial(_hgrn2_kernel, rows=rows),
        grid=(HG_HEADS, t // rows),
        in_specs=[pl.BlockSpec((rows, HG_DK), lambda h, c: (c, qb + h)),
                  pl.BlockSpec((rows, HG_DK), lambda h, c: (c, fb + h)),
                  pl.BlockSpec((rows, HG_DV), lambda h, c: (c, ib + h)),
                  pl.BlockSpec((rows, HG_DV), lambda h, c: (c, gb + h)),
                  pl.BlockSpec((3, HG_DK), lambda h, c: (0, h)),
                  pl.BlockSpec((1, HG_DV), lambda h, c: (0, h)),
                  pl.BlockSpec((rows, rows), lambda h, c: (0, 0))],
        out_specs=pl.BlockSpec((rows, HG_DV), lambda h, c: (c, h)),
        out_shape=jax.ShapeDtypeStruct((t, HG_V_W), MXU_DTYPE),
        scratch_shapes=[pltpu.VMEM((HG_DV, HG_DK), F32)],
        compiler_params=_params("parallel", "arbitrary"),
        name="hgrn2",
    )(u, u, u, u, lb_tab, norm_g.reshape(1, HG_V_W), _pair_level_table(rows))


def _rope_slab(u, cc, sa, sb, half):
    return u * cc + pltpu.roll(u, LANES - half, 1) * sa + pltpu.roll(u, half, 1) * sb


def _rope_coeffs(ang, half):
    lane = lax.broadcasted_iota(jnp.int32, ang.shape, 1)
    cos, sin = jnp.cos(ang), jnp.sin(ang)
    cc = jnp.where(lane < 2 * half, cos, 1.0)
    sa = jnp.where(lane < half, -sin, 0.0)
    sb = jnp.where((lane >= half) & (lane < 2 * half), sin, 0.0)
    return cc, sa, sb


def _dsa_prep_kernel(cq_ref, k_ref, v_ref, idx_ref, pos_ref, g_ref, w_ref, fr_ref,
                     q_out, qi_out, k_out, v_out, ki_out, wi_out):
    pos = pos_ref[...].astype(F32)
    cq = cq_ref[...]
    ms = jnp.mean(cq * cq, axis=-1, keepdims=True)
    hq = cq * lax.rsqrt(ms + EPS) * g_ref[...]
    q_all = _dot(hq, w_ref[...])

    half_a = DSA_DH // ROPE_FRACTION // 2
    half_i = IDX_DH // ROPE_FRACTION // 2
    ca = _rope_coeffs(pos * fr_ref[0:1, :], half_a)
    ci = _rope_coeffs(pos * fr_ref[1:2, :], half_i)

    kk = k_ref[...]
    for h in range(DSA_HEADS):
        sl = slice(h * DSA_DH, (h + 1) * DSA_DH)
        q_out[:, sl] = (_rope_slab(q_all[:, sl], *ca, half_a) * Q_SCALE).astype(q_out.dtype)
        k_out[:, sl] = _rope_slab(kk[:, sl], *ca, half_a).astype(k_out.dtype)
    for h in range(IDX_HEADS):
        src = slice(DSA_W + h * IDX_SLOT, DSA_W + (h + 1) * IDX_SLOT)
        qi_out[h] = _rope_slab(q_all[:, src], *ci, half_i).astype(qi_out.dtype)
    v_out[...] = v_ref[...].astype(v_out.dtype)

    idx = idx_ref[...]
    lane = lax.broadcasted_iota(jnp.int32, idx.shape, 1)
    ki = _rope_slab(jnp.where(lane < IDX_DH, idx, 0.0), *ci, half_i)
    ki_out[...] = ki.astype(ki_out.dtype)
    wi = pltpu.roll(idx, LANES - IDX_DH, 1) * (IDX_HEADS ** -0.5 * IDX_DH ** -0.5)
    wi_out[...] = jnp.where(lane < IDX_HEADS, wi, 0.0)


def _dsa_prep(u, pos, q_norm_g, w_uq_pad, freqs, *, rows):
    t = u.shape[0]
    nq = DSA_W + IDX_HEADS * IDX_SLOT
    outs = (jax.ShapeDtypeStruct((t, DSA_W), MXU_DTYPE),
            jax.ShapeDtypeStruct((IDX_HEADS, t, IDX_SLOT), MXU_DTYPE),
            jax.ShapeDtypeStruct((t, DSA_W), MXU_DTYPE),
            jax.ShapeDtypeStruct((t, DSA_W), MXU_DTYPE),
            jax.ShapeDtypeStruct((t, IDX_SLOT), MXU_DTYPE),
            jax.ShapeDtypeStruct((t, LANES), F32))
    row_spec = lambda w, blk: pl.BlockSpec((rows, w), lambda i: (i, blk))
    return pl.pallas_call(
        _dsa_prep_kernel,
        grid=(t // rows,),
        in_specs=[row_spec(DSA_Q_RANK, OFF_DSA_CQ // DSA_Q_RANK),
                  row_spec(DSA_W, OFF_DSA_K // DSA_W),
                  row_spec(DSA_W, OFF_DSA_V // DSA_W),
                  row_spec(LANES, OFF_IDX // LANES),
                  pl.BlockSpec((rows, 1), lambda i: (i, 0)),
                  pl.BlockSpec((1, DSA_Q_RANK), lambda i: (0, 0)),
                  pl.BlockSpec((DSA_Q_RANK, nq), lambda i: (0, 0)),
                  pl.BlockSpec((2, LANES), lambda i: (0, 0))],
        out_specs=(row_spec(DSA_W, 0),
                   pl.BlockSpec((IDX_HEADS, rows, IDX_SLOT), lambda i: (0, i, 0)),
                   row_spec(DSA_W, 0), row_spec(DSA_W, 0), row_spec(IDX_SLOT, 0), row_spec(LANES, 0)),
        out_shape=outs,
        compiler_params=_params("parallel"),
        name="dsa_prep",
    )(u, u, u, u, pos, q_norm_g.reshape(1, DSA_Q_RANK), w_uq_pad, freqs)


IDX_TQ = 256
IDX_TK = 512
IDX_SLAB = 128


def _indexer_kernel(qi_ref, wi_ref, ki_ref, incl_ref, bias_ref, key_ref, *, topk):
    tq, tk = IDX_TQ, IDX_TK
    qb = pl.program_id(0)
    nkb = ((qb + 1) * tq + tk - 1) // tk
    q_all = qi_ref[...].reshape(IDX_HEADS * tq, IDX_SLOT)
    w = wi_ref[...]
    w_col = jnp.concatenate([w[:, h:h + 1] for h in range(IDX_HEADS)], axis=0)
    row_chunk = (qb * tq + lax.broadcasted_iota(jnp.int32, (tq, 1), 0)) // CHUNK
    col_in_blk = lax.broadcasted_iota(jnp.int32, (1, tk), 1)

    def score_body(kb, carry):
        kt = ki_ref[pl.ds(pl.multiple_of(kb * tk, tk), tk), :]
        weighted = jnp.maximum(_dot_nt(q_all, kt), 0.0) * w_col
        parts = [weighted[h * tq:(h + 1) * tq] for h in range(IDX_HEADS)]
        while len(parts) > 1:
            parts = [a + b for a, b in zip(parts[0::2], parts[1::2])]
        col_chunk = (kb * tk + col_in_blk) // CHUNK
        score = jnp.where(col_chunk <= row_chunk, parts[0], -jnp.inf)
        bits = pltpu.bitcast(score, jnp.int32)
        bits = jnp.where(bits == INT_MIN, 0, bits)
        key_ref[kb] = jnp.where(bits < 0, bits ^ 0x7FFFFFFF, bits)
        return carry

    lax.fori_loop(0, nkb, score_body, 0)

    ones_mat = jnp.ones((LANES, LANES), MXU_DTYPE)

    def count_ge(cand):
        counts = []
        for r0 in range(0, tq, IDX_SLAB):
            cand_r = cand[r0:r0 + IDX_SLAB]

            def body(kb, cnt, r0=r0, cand_r=cand_r):
                for j in range(tk // LANES):
                    cnt += jnp.where(key_ref[kb, r0:r0 + IDX_SLAB, j * LANES:(j + 1) * LANES] >= cand_r, 1, 0)
                return cnt
            counts.append(lax.fori_loop(0, nkb, body, jnp.zeros((IDX_SLAB, LANES), jnp.int32)))
        cnt = jnp.concatenate(counts, axis=0)
        return jnp.dot(cnt.astype(F32).astype(MXU_DTYPE), ones_mat, preferred_element_type=F32)

    zero = jnp.zeros((tq, LANES), jnp.int32)
    thr_rep = jnp.where(count_ge(zero) >= topk, zero, INT_MIN)

    def bit_body(i, thr_rep):
        cand = thr_rep + jnp.left_shift(jnp.int32(1), 30 - i)
        return jnp.where(count_ge(cand) >= topk, cand, thr_rep)

    thr_rep = lax.fori_loop(0, 31, bit_body, thr_rep)
    need = topk - count_ge(thr_rep + 1)[:, 0:1]
    thr = thr_rep[:, 0:1]

    bias_ref[...] = jnp.full(bias_ref.shape, NEG_BIG, bias_ref.dtype)

    def emit_body(kb, seen):
        key = key_ref[kb]
        eq = key == thr
        eq_f = jnp.where(eq, 1.0, 0.0)
        rank = seen + jnp.dot(eq_f.astype(MXU_DTYPE), incl_ref[...], preferred_element_type=F32)
        take = ((key > thr) | (eq & (rank <= need))) & (key > KEY_NEG_INF)
        bias_ref[:, pl.ds(pl.multiple_of(kb * tk, tk), tk)] = jnp.where(take, 0.0, NEG_BIG).astype(bias_ref.dtype)
        return seen + jnp.sum(eq_f, axis=1, keepdims=True)

    lax.fori_loop(0, nkb, emit_body, jnp.zeros((tq, 1), F32))


def _indexer(qi, wi, ki, *, topk):
    t = ki.shape[0]
    incl = jnp.asarray(np.triu(np.ones((IDX_TK, IDX_TK), np.float32)), MXU_DTYPE)
    return pl.pallas_call(
        functools.partial(_indexer_kernel, topk=topk),
        grid=(t // IDX_TQ,),
        in_specs=[pl.BlockSpec((IDX_HEADS, IDX_TQ, IDX_SLOT), lambda i: (0, i, 0)),
                  pl.BlockSpec((IDX_TQ, LANES), lambda i: (i, 0)),
                  pl.BlockSpec((t, IDX_SLOT), lambda i: (0, 0)),
                  pl.BlockSpec((IDX_TK, IDX_TK), lambda i: (0, 0))],
        out_specs=pl.BlockSpec((IDX_TQ, t), lambda i: (i, 0)),
        out_shape=jax.ShapeDtypeStruct((t, t), BF16),
        scratch_shapes=[pltpu.VMEM((t // IDX_TK, IDX_TQ, IDX_TK), jnp.int32)],
        compiler_params=_params("parallel"),
        name="dsa_indexer",
    )(qi, wi, ki, incl)


def _attn_kernel(q_ref, k_ref, v_ref, bias_ref, o_ref, m_ref, l_ref, acc_ref, *, tq, tk):
    qb, kb = pl.program_id(0), pl.program_id(1)
    last = ((qb + 1) * tq - 1) // tk

    @pl.when(kb == 0)
    def _():
        m_ref[...] = jnp.full(m_ref.shape, NEG_BIG, F32)
        l_ref[...] = jnp.zeros_like(l_ref)
        acc_ref[...] = jnp.zeros_like(acc_ref)

    @pl.when(kb <= last)
    def _():
        bias = bias_ref[...].astype(F32)
        for h in range(DSA_HEADS):
            sl = slice(h * DSA_DH, (h + 1) * DSA_DH)
            s = _dot_nt(q_ref[:, sl], k_ref[:, sl]) + bias
            m_old = m_ref[h]
            m_new = jnp.maximum(m_old, jnp.max(s, axis=-1, keepdims=True))
            alpha = jnp.exp2(m_old - m_new)
            p = jnp.exp2(s - jnp.concatenate([m_new] * (tk // LANES), axis=1))
            l_ref[h] = alpha * l_ref[h] + jnp.sum(p, axis=-1, keepdims=True)
            acc_ref[:, sl] = alpha * acc_ref[:, sl] + _dot(p, v_ref[:, sl])
            m_ref[h] = m_new

    @pl.when(kb == last)
    def _():
        for h in range(DSA_HEADS):
            sl = slice(h * DSA_DH, (h + 1) * DSA_DH)
            o_ref[:, sl] = (acc_ref[:, sl] / l_ref[h]).astype(o_ref.dtype)


def _attention(q, k, v, bias, *, tq, tk):
    t = q.shape[0]
    last = lambda i: ((i + 1) * tq - 1) // tk
    return pl.pallas_call(
        functools.partial(_attn_kernel, tq=tq, tk=tk),
        grid=(t // tq, t // tk),
        in_specs=[pl.BlockSpec((tq, DSA_W), lambda i, j: (i, 0)),
                  pl.BlockSpec((tk, DSA_W), lambda i, j: (jnp.minimum(j, last(i)), 0)),
                  pl.BlockSpec((tk, DSA_W), lambda i, j: (jnp.minimum(j, last(i)), 0)),
                  pl.BlockSpec((tq, tk), lambda i, j: (i, jnp.minimum(j, last(i))))],
        out_specs=pl.BlockSpec((tq, DSA_W), lambda i, j: (i, 0)),
        out_shape=jax.ShapeDtypeStruct((t, DSA_W), MXU_DTYPE),
        scratch_shapes=[pltpu.VMEM((DSA_HEADS, tq, LANES), F32),
                        pltpu.VMEM((DSA_HEADS, tq, LANES), F32),
                        pltpu.VMEM((tq, DSA_W), F32)],
        compiler_params=_params("parallel", "arbitrary"),
        name="dsa_attention",
    )(q, k, v, bias)


def _pad_heads(w, heads, width, padded):
    lead = w.shape[:-1]
    w = w.reshape(lead + (heads, width))
    w = jnp.pad(w, [(0, 0)] * len(lead) + [(0, 0), (0, padded - width)])
    return w.reshape(lead + (heads * padded,))


def _pack_plan():
    src, start = {}, 0
    names = ("ml_q", "ml_k", "ml_v", "ml_i", "ml_f", "ml_o", "dsa_cq", "dsa_k", "dsa_v", "idx_k", "idx_w",
             "hg_q", "hg_f", "hg_i", "hg_g")
    for name, width in zip(names, IN_SPLITS):
        src[name] = start
        start += width
    copies = []
    for h in range(ML_HEADS):
        for dst0, name in ((OFF_ML_Q, "ml_q"), (OFF_ML_K, "ml_k")):
            copies.append((dst0 + h * ML_DQK_PAD, src[name] + h * ML_DQK, ML_DQK))
    copies += [(OFF_DSA_K, src["dsa_k"], DSA_W), (OFF_ML_V, src["ml_v"], ML_V_W), (OFF_ML_O, src["ml_o"], ML_V_W),
               (OFF_HG_Q, src["hg_q"], HG_K_W), (OFF_HG_F, src["hg_f"], HG_K_W),
               (OFF_HG_I, src["hg_i"], HG_V_W), (OFF_HG_G, src["hg_g"], HG_V_W),
               (OFF_DSA_V, src["dsa_v"], DSA_W),
               (OFF_GATES, src["ml_i"], 2 * ML_HEADS),
               (OFF_DSA_CQ, src["dsa_cq"], DSA_Q_RANK),
               (OFF_IDX, src["idx_k"], IDX_DH + IDX_HEADS)]
    return copies


def _pack_kernel(wt_ref, o_ref):
    kt = o_ref.shape[0]
    for dst, src, width in _pack_plan():
        span = -(-width // LANES) * LANES
        seg = wt_ref[src:src + span, :].T
        if span != width:
            lane = lax.broadcasted_iota(jnp.int32, seg.shape, 1)
            seg = jnp.where(lane < width, seg, 0.0)
        o_ref[:, dst:dst + span] = seg.astype(o_ref.dtype)
    tail = OFF_IDX + LANES
    o_ref[:, tail:] = jnp.zeros((kt, D_IN_PAD - tail), o_ref.dtype)


def _pack_w_in(w_in, *, kt):
    depth, d, n = w_in.shape
    wt = jnp.swapaxes(w_in, 1, 2)
    return pl.pallas_call(
        _pack_kernel,
        grid=(depth, d // kt),
        in_specs=[pl.BlockSpec((None, n, kt), lambda l, i: (l, 0, i))],
        out_specs=pl.BlockSpec((None, kt, D_IN_PAD), lambda l, i: (l, i, 0)),
        out_shape=jax.ShapeDtypeStruct((depth, d, D_IN_PAD), MXU_DTYPE),
        compiler_params=_params("parallel", "parallel"),
        name="pack_w_in",
    )(wt)


def _rope_freqs():
    def lanes(d):
        rot = d // ROPE_FRACTION
        half = rot // 2
        inv = jnp.power(ROPE_THETA, -jnp.arange(half, dtype=F32) * (2.0 / rot))
        return jnp.concatenate([inv, inv, jnp.zeros((LANES - rot,), F32)])
    return jnp.stack([lanes(DSA_DH), lanes(IDX_DH)])


def kernel(x, positions, ln_mix_g, w_in, ml_conv_w, ml_gate_b, ml_norm_g, dsa_q_norm_g, dsa_w_uq,
           hg_lb_logits, hg_norm_g, w_out, ln_mlp_g, w_up, w_down, ln_final_g):
    bsz, t, d = x.shape
    assert bsz == 1 and t % 512 == 0 and d == D_MODEL
    depth = w_in.shape[0]
    topk = min(TOPK_MAX, t // 4)
    xs = x.reshape(t, d)
    pos = positions.reshape(t, 1)

    w_in_p = _pack_w_in(w_in, kt=256)
    w_out_b = w_out.astype(MXU_DTYPE)
    w_up_b = w_up.astype(MXU_DTYPE)
    w_down_b = w_down.astype(MXU_DTYPE)
    conv_q = _pad_heads(ml_conv_w[..., :ML_QK_W], ML_HEADS, ML_DQK, ML_DQK_PAD)
    conv_k = _pad_heads(ml_conv_w[..., ML_QK_W:], ML_HEADS, ML_DQK, ML_DQK_PAD)
    conv_p = jnp.concatenate([conv_q.reshape(depth, ML_CONV, ML_HEADS, ML_DQK_PAD),
                              conv_k.reshape(depth, ML_CONV, ML_HEADS, ML_DQK_PAD)], axis=-1)
    conv_p = jnp.transpose(conv_p, (0, 2, 1, 3))
    gate_b = jnp.pad(ml_gate_b.reshape(depth, 1, 2 * ML_HEADS), ((0, 0), (0, 0), (0, LANES - 2 * ML_HEADS)))
    w_uq_p = jnp.concatenate([dsa_w_uq[..., :DSA_W],
                              _pad_heads(dsa_w_uq[..., DSA_W:], IDX_HEADS, IDX_DH, IDX_SLOT)],
                             axis=-1).astype(MXU_DTYPE)
    freqs = _rope_freqs()
    lb_cum = jnp.cumsum(jax.nn.softmax(hg_lb_logits.astype(F32), axis=0), axis=0)
    lb = lb_cum - lb_cum[:1]
    lb_tab = jnp.stack([jnp.log(lb), jnp.log1p(-lb), 1.0 - lb], axis=1)

    for layer in range(depth):
        u = _norm_matmul(xs, ln_mix_g[layer], w_in_p, layer, tm=512, tn=1024, act=False, out_dtype=F32)
        y_a = _mlstm(u, conv_p[layer], gate_b[layer], ml_norm_g[layer], rows=256)
        q_r, qi_r, k_r, v_b, ki_r, wi = _dsa_prep(u, pos, dsa_q_norm_g[layer], w_uq_p[layer], freqs, rows=256)
        bias = _indexer(qi_r, wi, ki_r, topk=topk)
        y_b = _attention(q_r, k_r, v_b, bias, tq=512, tk=1024)
        y_c = _hgrn2(u, lb_tab[layer], hg_norm_g[layer], rows=256)
        xs = _mix_out(y_a, y_b, y_c, w_out_b, layer, xs, tm=512, tn=1024)
        a = _norm_matmul(xs, ln_mlp_g[layer], w_up_b, layer, tm=512, tn=1024, act=True, out_dtype=MXU_DTYPE)
        xs = _matmul_res(a, w_down_b, layer, xs, tm=1024, tn=1024, tk=2048)
    return _final_norm(xs, ln_final_g, tm=256).reshape(bsz, t, d)
```

```python
import functools

import jax
import jax.numpy as jnp
import numpy as np
from jax import lax
from jax.experimental import pallas as pl
from jax.experimental.pallas import tpu as pltpu

F32 = jnp.float32
BF16 = jnp.bfloat16
MXU_DTYPE = jnp.bfloat16

D_MODEL = 4096
DEPTH = 4
CHUNK = 64
EPS = 1e-6
ROPE_THETA = 500000.0
ROPE_FRACTION = 4
ML_HEADS = 4
ML_DV = 384
ML_DQK = ML_DV // 2
ML_DQK_PAD = 256
ML_CONV = 4
ML_GATE_CAP = 15.0
DSA_HEADS = 8
DSA_DH = 128
DSA_Q_RANK = 384
IDX_HEADS = 8
IDX_DH = 64
IDX_SLOT = 128
TOPK_MAX = 256
HG_HEADS = 12
HG_DK = 128
HG_DV = 128
D_FF = 4 * D_MODEL

ML_QK_W = ML_HEADS * ML_DQK
ML_V_W = ML_HEADS * ML_DV
DSA_W = DSA_HEADS * DSA_DH
IDX_W = IDX_HEADS * IDX_DH
HG_K_W = HG_HEADS * HG_DK
HG_V_W = HG_HEADS * HG_DV
D_MIX = ML_V_W + DSA_W + HG_V_W
IN_SPLITS = (ML_QK_W, ML_QK_W, ML_V_W, ML_HEADS, ML_HEADS, ML_V_W,
             DSA_Q_RANK, DSA_W, DSA_W, IDX_DH, IDX_HEADS,
             HG_K_W, HG_K_W, HG_V_W, HG_V_W)

LANES = 128
VMEM_LIMIT = 56 * 1024 * 1024

OFF_ML_Q = 0
OFF_ML_K = 1024
OFF_DSA_K = 2048
OFF_ML_V = 3072
OFF_ML_O = 4608
OFF_HG_Q = 6144
OFF_HG_F = 7680
OFF_HG_I = 9216
OFF_HG_G = 10752
OFF_DSA_V = 12288
OFF_GATES = 13312
OFF_DSA_CQ = 13440
OFF_IDX = 13824
D_IN_PAD = 14336

LOG2E = 1.4426950408889634
Q_SCALE = DSA_DH ** -0.5 * LOG2E

INT_MIN = -2 ** 31
KEY_NEG_INF = -2139095041
NEG_BIG = -1e30


def _mxu(a):
    return a.astype(MXU_DTYPE)


def _dot(a, b):
    return jnp.dot(_mxu(a), _mxu(b), preferred_element_type=F32)


def _dot_nt(a, b):
    return lax.dot_general(_mxu(a), _mxu(b), (((1,), (1,)), ((), ())), preferred_element_type=F32)


def _dot_tn(a, b):
    return lax.dot_general(_mxu(a), _mxu(b), (((0,), (0,)), ((), ())), preferred_element_type=F32)


def _dot_f32(a, b):
    return jnp.dot(a, b, precision=lax.Precision.HIGHEST, preferred_element_type=F32)


def _sigmoid(x):
    return 1.0 / (1.0 + jnp.exp(-x))


def _log_sigmoid(x):
    return jnp.minimum(x, 0.0) - jnp.log1p(jnp.exp(-jnp.abs(x)))


def _params(*sem):
    return pltpu.CompilerParams(dimension_semantics=sem, vmem_limit_bytes=VMEM_LIMIT)


def _norm_matmul_kernel(x_ref, g_ref, w_ref, o_ref, h_ref, *, act):
    @pl.when(pl.program_id(1) == 0)
    def _():
        x = x_ref[...]
        ms = jnp.mean(x * x, axis=-1, keepdims=True)
        h_ref[...] = (x * lax.rsqrt(ms + EPS) * g_ref[...]).astype(h_ref.dtype)

    y = jnp.dot(h_ref[...], w_ref[...], preferred_element_type=F32)
    if act:
        y = jnp.square(jnp.maximum(y, 0.0))
    o_ref[...] = y.astype(o_ref.dtype)


def _norm_matmul(x, g, w, layer, *, tm, tn, act, out_dtype):
    m, k = x.shape
    n = w.shape[2]
    return pl.pallas_call(
        functools.partial(_norm_matmul_kernel, act=act),
        grid=(m // tm, n // tn),
        in_specs=[pl.BlockSpec((tm, k), lambda i, j: (i, 0)),
                  pl.BlockSpec((1, k), lambda i, j: (0, 0)),
                  pl.BlockSpec((None, k, tn), lambda i, j: (layer, 0, j))],
        out_specs=pl.BlockSpec((tm, tn), lambda i, j: (i, j)),
        out_shape=jax.ShapeDtypeStruct((m, n), out_dtype),
        scratch_shapes=[pltpu.VMEM((tm, k), MXU_DTYPE)],
        compiler_params=_params("parallel", "arbitrary"),
        name="norm_matmul",
    )(x, g.reshape(1, k), w)


def _matmul_res_kernel(a_ref, w_ref, r_ref, o_ref):
    part = jnp.dot(a_ref[...], w_ref[...], preferred_element_type=F32)

    @pl.when(pl.program_id(2) == 0)
    def _():
        o_ref[...] = r_ref[...] + part

    @pl.when(pl.program_id(2) != 0)
    def _():
        o_ref[...] += part


def _matmul_res(a, w, layer, res, *, tm, tn, tk):
    m, k = a.shape
    n = w.shape[2]
    return pl.pallas_call(
        _matmul_res_kernel,
        grid=(m // tm, n // tn, k // tk),
        in_specs=[pl.BlockSpec((tm, tk), lambda i, j, kk: (i, kk)),
                  pl.BlockSpec((None, tk, tn), lambda i, j, kk: (layer, kk, j)),
                  pl.BlockSpec((tm, tn), lambda i, j, kk: (i, j))],
        out_specs=pl.BlockSpec((tm, tn), lambda i, j, kk: (i, j)),
        out_shape=jax.ShapeDtypeStruct((m, n), F32),
        compiler_params=_params("parallel", "parallel", "arbitrary"),
        name="matmul_res",
    )(a, w, res)


def _mix_out_kernel(ya_ref, yb_ref, yc_ref, w_ref, r_ref, o_ref):
    acc = r_ref[...]
    acc += jnp.dot(ya_ref[...], w_ref[0:ML_V_W, :], preferred_element_type=F32)
    acc += jnp.dot(yb_ref[...], w_ref[ML_V_W:ML_V_W + DSA_W, :], preferred_element_type=F32)
    acc += jnp.dot(yc_ref[...], w_ref[ML_V_W + DSA_W:D_MIX, :], preferred_element_type=F32)
    o_ref[...] = acc


def _mix_out(ya, yb, yc, w, layer, res, *, tm, tn):
    m = ya.shape[0]
    n = w.shape[2]
    return pl.pallas_call(
        _mix_out_kernel,
        grid=(m // tm, n // tn),
        in_specs=[pl.BlockSpec((tm, ML_V_W), lambda i, j: (i, 0)),
                  pl.BlockSpec((tm, DSA_W), lambda i, j: (i, 0)),
                  pl.BlockSpec((tm, HG_V_W), lambda i, j: (i, 0)),
                  pl.BlockSpec((None, D_MIX, tn), lambda i, j: (layer, 0, j)),
                  pl.BlockSpec((tm, tn), lambda i, j: (i, j))],
        out_specs=pl.BlockSpec((tm, tn), lambda i, j: (i, j)),
        out_shape=jax.ShapeDtypeStruct((m, n), F32),
        compiler_params=_params("parallel", "parallel"),
        name="mix_out",
    )(ya, yb, yc, w, res)


def _final_norm_kernel(x_ref, g_ref, o_ref):
    x = x_ref[...]
    ms = jnp.mean(x * x, axis=-1, keepdims=True)
    o_ref[...] = x * lax.rsqrt(ms + EPS) * g_ref[...]


def _final_norm(x, g, *, tm):
    m, k = x.shape
    return pl.pallas_call(
        _final_norm_kernel,
        grid=(m // tm,),
        in_specs=[pl.BlockSpec((tm, k), lambda i: (i, 0)),
                  pl.BlockSpec((1, k), lambda i: (0, 0))],
        out_specs=pl.BlockSpec((tm, k), lambda i: (i, 0)),
        out_shape=jax.ShapeDtypeStruct((m, k), F32),
        compiler_params=_params("parallel"),
        name="final_norm",
    )(x, g.reshape(1, k))


def _mlstm_kernel(q_ref, k_ref, v_ref, o_ref, gt_ref, cw_ref, gb_ref, ng_ref, out_ref,
                  xbuf, c_ref, *, rows):
    head = pl.program_id(0)

    @pl.when(pl.program_id(1) == 0)
    def _():
        xbuf[0:8, :] = jnp.zeros((8, 2 * ML_DQK_PAD), F32)
        c_ref[...] = jnp.zeros_like(c_ref)

    xbuf[8:8 + rows, 0:ML_DQK_PAD] = q_ref[...]
    xbuf[8:8 + rows, ML_DQK_PAD:] = k_ref[...]
    cw = cw_ref[...]
    acc = xbuf[8:8 + rows, :] * cw[ML_CONV - 1:ML_CONV, :]
    for j in range(1, ML_CONV):
        acc += xbuf[8 - j:8 - j + rows, :] * cw[ML_CONV - 1 - j:ML_CONV - j, :]
    xbuf[0:8, :] = xbuf[rows:rows + 8, :]
    qk = acc * _sigmoid(acc)
    q = qk[:, 0:ML_DQK_PAD]
    k = qk[:, ML_DQK_PAD:] * (ML_DQK ** -0.5)

    capped = ML_GATE_CAP * jnp.tanh((gt_ref[...] + gb_ref[...]) * (1.0 / ML_GATE_CAP))
    lsig = _log_sigmoid(capped)
    r_i = lax.broadcasted_iota(jnp.int32, (rows, rows), 0)
    c_i = lax.broadcasted_iota(jnp.int32, (rows, rows), 1)
    causal = c_i <= r_i
    b_all = _dot_f32(jnp.where(causal, 1.0, 0.0), lsig)
    lane = lax.broadcasted_iota(jnp.int32, (rows, LANES), 1)
    ig_col = jnp.sum(jnp.where(lane == head, capped, 0.0), axis=1, keepdims=True)
    b_col = jnp.sum(jnp.where(lane == head + ML_HEADS, b_all, 0.0), axis=1, keepdims=True)
    sub = lax.broadcasted_iota(jnp.int32, (LANES, rows), 0)
    ig_row = jnp.sum(jnp.where(sub == head, capped.T, 0.0), axis=0, keepdims=True)
    b_row = jnp.sum(jnp.where(sub == head + ML_HEADS, b_all.T, 0.0), axis=0, keepdims=True)

    dmat = jnp.exp(jnp.where(causal, b_col - b_row + ig_row, -jnp.inf))
    s = _dot_nt(q, k) * dmat
    one_col = jnp.where(lane == 0, 1.0, 0.0)
    v_ext = jnp.concatenate([v_ref[...], one_col], axis=1)
    c_old = c_ref[...]
    num_ext = _dot(s, v_ext) + jnp.exp(b_col) * _dot(q, c_old)
    num = num_ext[:, 0:ML_DV]
    den = num_ext[:, ML_DV:ML_DV + 1]
    hh = num / jnp.maximum(jnp.abs(den), 1.0)

    b_last = b_col[rows - 1:rows, :]
    w_s = jnp.exp(b_last - b_col + ig_col)
    c_ref[...] = jnp.exp(b_last) * c_old + _dot_tn(k, w_s * v_ext)

    ms = jnp.mean(hh * hh, axis=-1, keepdims=True)
    y = hh * lax.rsqrt(ms + EPS) * ng_ref[...]
    out_ref[...] = (_sigmoid(o_ref[...]) * y).astype(out_ref.dtype)


def _mlstm(u, conv_w, gate_b, norm_g, *, rows):
    t = u.shape[0]
    qb, kb = OFF_ML_Q // ML_DQK_PAD, OFF_ML_K // ML_DQK_PAD
    vb, ob = OFF_ML_V // ML_DV, OFF_ML_O // ML_DV
    gb = OFF_GATES // LANES
    return pl.pallas_call(
        functools.partial(_mlstm_kernel, rows=rows),
        grid=(ML_HEADS, t // rows),
        in_specs=[pl.BlockSpec((rows, ML_DQK_PAD), lambda h, c: (c, qb + h)),
                  pl.BlockSpec((rows, ML_DQK_PAD), lambda h, c: (c, kb + h)),
                  pl.BlockSpec((rows, ML_DV), lambda h, c: (c, vb + h)),
                  pl.BlockSpec((rows, ML_DV), lambda h, c: (c, ob + h)),
                  pl.BlockSpec((rows, LANES), lambda h, c: (c, gb)),
                  pl.BlockSpec((None, ML_CONV, 2 * ML_DQK_PAD), lambda h, c: (h, 0, 0)),
                  pl.BlockSpec((1, LANES), lambda h, c: (0, 0)),
                  pl.BlockSpec((1, ML_DV), lambda h, c: (0, h))],
        out_specs=pl.BlockSpec((rows, ML_DV), lambda h, c: (c, h)),
        out_shape=jax.ShapeDtypeStruct((t, ML_V_W), MXU_DTYPE),
        scratch_shapes=[pltpu.VMEM((rows + 8, 2 * ML_DQK_PAD), F32),
                        pltpu.VMEM((ML_DQK_PAD, ML_DV + LANES), F32)],
        compiler_params=_params("parallel", "arbitrary"),
        name="mlstm",
    )(u, u, u, u, u, conv_w, gate_b, norm_g.reshape(1, ML_V_W))


SUBLANES = 8


def _pair_level_table(rows):
    t = np.arange(rows)[:, None]
    s = np.arange(rows)[None, :]
    x = np.bitwise_xor(t, s)
    lvl = np.floor(np.log2(np.maximum(x, 1))).astype(np.int32)
    return jnp.asarray(np.where(t > s, lvl, -1).astype(np.int32))


def _block_sums(log_f, rows):
    sub = lax.broadcasted_iota(jnp.int32, log_f.shape, 0)
    groups = rows // SUBLANES

    def row_of_group(x, j):
        x3 = x.reshape(groups, SUBLANES, x.shape[-1])
        return jnp.broadcast_to(x3[:, j:j + 1, :], x3.shape).reshape(x.shape)

    c = log_f
    e = jnp.zeros_like(log_f)
    out = [(c, e)]
    odd = (sub & 1) == 1
    c, e = (c + jnp.where(odd, pltpu.roll(c, 1, 0), 0.0),
            e + jnp.where(odd, 0.0, pltpu.roll(c, rows - 1, 0)))
    out.append((c, e))
    r8 = sub & 7
    c, e = (c + jnp.where((r8 == 2) | (r8 == 3), row_of_group(c, 1),
                          jnp.where((r8 == 6) | (r8 == 7), row_of_group(c, 5), 0.0)),
            e + jnp.where((r8 == 0) | (r8 == 1), row_of_group(c, 3),
                          jnp.where((r8 == 4) | (r8 == 5), row_of_group(c, 7), 0.0)))
    out.append((c, e))
    c, e = (c + jnp.where(r8 >= 4, row_of_group(c, 3), 0.0),
            e + jnp.where(r8 < 4, row_of_group(c, 7), 0.0))
    out.append((c, e))
    m = SUBLANES
    while m < rows:
        cs, es = [], []
        for p in range(rows // (2 * m)):
            lo, mid, hi = 2 * m * p, 2 * m * p + m, 2 * m * (p + 1)
            cs += [c[lo:mid], c[mid:hi] + c[mid - 1:mid]]
            es += [e[lo:mid] + c[hi - 1:hi], e[mid:hi]]
        c, e = jnp.concatenate(cs, axis=0), jnp.concatenate(es, axis=0)
        out.append((c, e))
        m *= 2
    return out


def _hgrn2_kernel(q_ref, f_ref, i_ref, g_ref, lb_ref, ng_ref, lvl_ref, out_ref, st_ref, *, rows):
    @pl.when(pl.program_id(1) == 0)
    def _():
        st_ref[...] = jnp.zeros_like(st_ref)

    qp = q_ref[...]
    q = qp * _sigmoid(qp)
    fp = f_ref[...]
    v = i_ref[...]
    gp = g_ref[...]
    log_lb = lb_ref[0:1, :]
    x2 = lb_ref[1:2, :] + _log_sigmoid(fp)
    log_f = jnp.maximum(log_lb, x2) + jnp.log1p(jnp.exp(-jnp.abs(log_lb - x2)))
    k = lb_ref[2:3, :] * _sigmoid(-fp)

    sums = _block_sums(log_f * LOG2E, rows)
    lvl = lvl_ref[...]
    tiles = rows // LANES
    tile_rows = [slice(r * LANES, (r + 1) * LANES) for r in range(tiles)]
    diag = [jnp.zeros((LANES, LANES), F32) for _ in range(tiles)]
    o_tiles = [None] * tiles
    for level, (c_m, e_m) in enumerate(sums[:-1]):
        m = 1 << level
        qh = q * jnp.exp2(c_m)
        kh = k * jnp.exp2(e_m)
        if m < LANES:
            for r, sl in enumerate(tile_rows):
                diag[r] = jnp.where(lvl == level, _dot_nt(qh[sl], kh[sl]), diag[r])
        else:
            for p in range(rows // (2 * m)):
                lo, mid = 2 * m * p, 2 * m * p + m
                for r in range(mid // LANES, (mid + m) // LANES):
                    part = _dot(_dot_nt(qh[tile_rows[r]], kh[lo:mid]), v[lo:mid])
                    o_tiles[r] = part if o_tiles[r] is None else o_tiles[r] + part
    for r, sl in enumerate(tile_rows):
        part = _dot(diag[r], v[sl])
        o_tiles[r] = part if o_tiles[r] is None else o_tiles[r] + part
    b, after = sums[-1]

    st = st_ref[...]
    o = (jnp.concatenate(o_tiles, axis=0) + jnp.sum(q * k, axis=-1, keepdims=True) * v
         + _dot_nt(q * jnp.exp2(b), st))
    st_ref[...] = st * jnp.exp2(b[rows - 1:rows, :]) + _dot_tn(v, k * jnp.exp2(after))

    ms = jnp.mean(o * o, axis=-1, keepdims=True)
    y = o * lax.rsqrt(ms + EPS) * ng_ref[...]
    out_ref[...] = (y * (gp * _sigmoid(gp))).astype(out_ref.dtype)


def _hgrn2(u, lb_tab, norm_g, *, rows):
    t = u.shape[0]
    qb, fb = OFF_HG_Q // HG_DK, OFF_HG_F // HG_DK
    ib, gb = OFF_HG_I // HG_DV, OFF_HG_G // HG_DV
    return pl.pallas_call(
        functools.partial(_hgrn2_kernel, rows=rows),
        grid=(HG_HEADS, t // rows),
        in_specs=[pl.BlockSpec((rows, HG_DK), lambda h, c: (c, qb + h)),
                  pl.BlockSpec((rows, HG_DK), lambda h, c: (c, fb + h)),
                  pl.BlockSpec((rows, HG_DV), lambda h, c: (c, ib + h)),
                  pl.BlockSpec((rows, HG_DV), lambda h, c: (c, gb + h)),
                  pl.BlockSpec((3, HG_DK), lambda h, c: (0, h)),
                  pl.BlockSpec((1, HG_DV), lambda h, c: (0, h)),
                  pl.BlockSpec((LANES, LANES), lambda h, c: (0, 0))],
        out_specs=pl.BlockSpec((rows, HG_DV), lambda h, c: (c, h)),
        out_shape=jax.ShapeDtypeStruct((t, HG_V_W), MXU_DTYPE),
        scratch_shapes=[pltpu.VMEM((HG_DV, HG_DK), F32)],
        compiler_params=_params("parallel", "arbitrary"),
        name="hgrn2",
    )(u, u, u, u, lb_tab, norm_g.reshape(1, HG_V_W), _pair_level_table(LANES))


def _rope_slab(u, cc, sa, sb, half):
    return u * cc + pltpu.roll(u, LANES - half, 1) * sa + pltpu.roll(u, half, 1) * sb


def _rope_coeffs(ang, half):
    lane = lax.broadcasted_iota(jnp.int32, ang.shape, 1)
    cos, sin = jnp.cos(ang), jnp.sin(ang)
    cc = jnp.where(lane < 2 * half, cos, 1.0)
    sa = jnp.where(lane < half, -sin, 0.0)
    sb = jnp.where((lane >= half) & (lane < 2 * half), sin, 0.0)
    return cc, sa, sb


def _dsa_prep_kernel(cq_ref, k_ref, v_ref, idx_ref, pos_ref, g_ref, w_ref, fr_ref,
                     q_out, qi_out, k_out, v_out, ki_out, wi_out):
    pos = pos_ref[...].astype(F32)
    cq = cq_ref[...]
    ms = jnp.mean(cq * cq, axis=-1, keepdims=True)
    hq = cq * lax.rsqrt(ms + EPS) * g_ref[...]
    q_all = _dot(hq, w_ref[...])

    half_a = DSA_DH // ROPE_FRACTION // 2
    half_i = IDX_DH // ROPE_FRACTION // 2
    ca = _rope_coeffs(pos * fr_ref[0:1, :], half_a)
    ci = _rope_coeffs(pos * fr_ref[1:2, :], half_i)

    kk = k_ref[...]
    for h in range(DSA_HEADS):
        sl = slice(h * DSA_DH, (h + 1) * DSA_DH)
        q_out[:, sl] = (_rope_slab(q_all[:, sl], *ca, half_a) * Q_SCALE).astype(q_out.dtype)
        k_out[:, sl] = _rope_slab(kk[:, sl], *ca, half_a).astype(k_out.dtype)
    for h in range(IDX_HEADS):
        src = slice(DSA_W + h * IDX_SLOT, DSA_W + (h + 1) * IDX_SLOT)
        qi_out[h] = _rope_slab(q_all[:, src], *ci, half_i).astype(qi_out.dtype)
    v_out[...] = v_ref[...].astype(v_out.dtype)

    idx = idx_ref[...]
    lane = lax.broadcasted_iota(jnp.int32, idx.shape, 1)
    ki = _rope_slab(jnp.where(lane < IDX_DH, idx, 0.0), *ci, half_i)
    ki_out[...] = ki.astype(ki_out.dtype)
    wi = pltpu.roll(idx, LANES - IDX_DH, 1) * (IDX_HEADS ** -0.5 * IDX_DH ** -0.5)
    wi_out[...] = jnp.where(lane < IDX_HEADS, wi, 0.0)


def _dsa_prep(u, pos, q_norm_g, w_uq_pad, freqs, *, rows):
    t = u.shape[0]
    nq = DSA_W + IDX_HEADS * IDX_SLOT
    outs = (jax.ShapeDtypeStruct((t, DSA_W), MXU_DTYPE),
            jax.ShapeDtypeStruct((IDX_HEADS, t, IDX_SLOT), MXU_DTYPE),
            jax.ShapeDtypeStruct((t, DSA_W), MXU_DTYPE),
            jax.ShapeDtypeStruct((t, DSA_W), MXU_DTYPE),
            jax.ShapeDtypeStruct((t, IDX_SLOT), MXU_DTYPE),
            jax.ShapeDtypeStruct((t, LANES), F32))
    row_spec = lambda w, blk: pl.BlockSpec((rows, w), lambda i: (i, blk))
    return pl.pallas_call(
        _dsa_prep_kernel,
        grid=(t // rows,),
        in_specs=[row_spec(DSA_Q_RANK, OFF_DSA_CQ // DSA_Q_RANK),
                  row_spec(DSA_W, OFF_DSA_K // DSA_W),
                  row_spec(DSA_W, OFF_DSA_V // DSA_W),
                  row_spec(LANES, OFF_IDX // LANES),
                  pl.BlockSpec((rows, 1), lambda i: (i, 0)),
                  pl.BlockSpec((1, DSA_Q_RANK), lambda i: (0, 0)),
                  pl.BlockSpec((DSA_Q_RANK, nq), lambda i: (0, 0)),
                  pl.BlockSpec((2, LANES), lambda i: (0, 0))],
        out_specs=(row_spec(DSA_W, 0),
                   pl.BlockSpec((IDX_HEADS, rows, IDX_SLOT), lambda i: (0, i, 0)),
                   row_spec(DSA_W, 0), row_spec(DSA_W, 0), row_spec(IDX_SLOT, 0), row_spec(LANES, 0)),
        out_shape=outs,
        compiler_params=_params("parallel"),
        name="dsa_prep",
    )(u, u, u, u, pos, q_norm_g.reshape(1, DSA_Q_RANK), w_uq_pad, freqs)


IDX_TQ = 256
IDX_TK = 512
IDX_SLAB = 128


def _indexer_kernel(qi_ref, wi_ref, ki_ref, incl_ref, bias_ref, key_ref, *, topk):
    tq, tk = IDX_TQ, IDX_TK
    qb = pl.program_id(0)
    nkb = ((qb + 1) * tq + tk - 1) // tk
    q_all = qi_ref[...].reshape(IDX_HEADS * tq, IDX_SLOT)
    w = wi_ref[...]
    w_col = jnp.concatenate([w[:, h:h + 1] for h in range(IDX_HEADS)], axis=0)
    row_chunk = (qb * tq + lax.broadcasted_iota(jnp.int32, (tq, 1), 0)) // CHUNK
    col_in_blk = lax.broadcasted_iota(jnp.int32, (1, tk), 1)

    def score_body(kb, carry):
        kt = ki_ref[pl.ds(pl.multiple_of(kb * tk, tk), tk), :]
        weighted = jnp.maximum(_dot_nt(q_all, kt), 0.0) * w_col
        parts = [weighted[h * tq:(h + 1) * tq] for h in range(IDX_HEADS)]
        while len(parts) > 1:
            parts = [a + b for a, b in zip(parts[0::2], parts[1::2])]
        col_chunk = (kb * tk + col_in_blk) // CHUNK
        score = jnp.where(col_chunk <= row_chunk, parts[0], -jnp.inf)
        bits = pltpu.bitcast(score, jnp.int32)
        bits = jnp.where(bits == INT_MIN, 0, bits)
        key_ref[kb] = jnp.where(bits < 0, bits ^ 0x7FFFFFFF, bits)
        return carry

    lax.fori_loop(0, nkb, score_body, 0)

    ones_mat = jnp.ones((LANES, LANES), MXU_DTYPE)

    def count_ge(cand):
        counts = []
        for r0 in range(0, tq, IDX_SLAB):
            cand_r = cand[r0:r0 + IDX_SLAB]

            def body(kb, cnt, r0=r0, cand_r=cand_r):
                for j in range(tk // LANES):
                    cnt += jnp.where(key_ref[kb, r0:r0 + IDX_SLAB, j * LANES:(j + 1) * LANES] >= cand_r, 1, 0)
                return cnt
            counts.append(lax.fori_loop(0, nkb, body, jnp.zeros((IDX_SLAB, LANES), jnp.int32)))
        cnt = jnp.concatenate(counts, axis=0)
        return jnp.dot(cnt.astype(F32).astype(MXU_DTYPE), ones_mat, preferred_element_type=F32)

    zero = jnp.zeros((tq, LANES), jnp.int32)
    cnt0 = count_ge(zero)
    thr_rep = jnp.where(cnt0 >= topk, zero, INT_MIN)
    at_thr = jnp.where(cnt0 >= topk, cnt0, (nkb * tk).astype(F32))

    def bit_body(i, carry):
        thr_rep, at_thr = carry
        cand = thr_rep + jnp.left_shift(jnp.int32(1), 30 - i)
        cnt = count_ge(cand)
        ok = cnt >= topk
        return jnp.where(ok, cand, thr_rep), jnp.where(ok, cnt, at_thr)

    thr_rep, at_thr = lax.fori_loop(0, 31, bit_body, (thr_rep, at_thr))
    thr = thr_rep[:, 0:1]
    exact = jnp.all(at_thr == topk)

    bias_ref[...] = jnp.full(bias_ref.shape, NEG_BIG, bias_ref.dtype)

    @pl.when(exact)
    def _():
        def emit_body(kb, carry):
            key = key_ref[kb]
            take = (key >= thr) & (key > KEY_NEG_INF)
            bias_ref[:, pl.ds(pl.multiple_of(kb * tk, tk), tk)] = jnp.where(take, 0.0, NEG_BIG).astype(bias_ref.dtype)
            return carry

        lax.fori_loop(0, nkb, emit_body, 0)

    @pl.when(jnp.logical_not(exact))
    def _():
        need = topk - count_ge(thr_rep + 1)[:, 0:1]

        def emit_body(kb, seen):
            key = key_ref[kb]
            eq = key == thr
            eq_f = jnp.where(eq, 1.0, 0.0)
            rank = seen + jnp.dot(eq_f.astype(MXU_DTYPE), incl_ref[...], preferred_element_type=F32)
            take = ((key > thr) | (eq & (rank <= need))) & (key > KEY_NEG_INF)
            bias_ref[:, pl.ds(pl.multiple_of(kb * tk, tk), tk)] = jnp.where(take, 0.0, NEG_BIG).astype(bias_ref.dtype)
            return seen + jnp.sum(eq_f, axis=1, keepdims=True)

        lax.fori_loop(0, nkb, emit_body, jnp.zeros((tq, 1), F32))


def _indexer(qi, wi, ki, *, topk):
    t = ki.shape[0]
    incl = jnp.asarray(np.triu(np.ones((IDX_TK, IDX_TK), np.float32)), MXU_DTYPE)
    return pl.pallas_call(
        functools.partial(_indexer_kernel, topk=topk),
        grid=(t // IDX_TQ,),
        in_specs=[pl.BlockSpec((IDX_HEADS, IDX_TQ, IDX_SLOT), lambda i: (0, i, 0)),
                  pl.BlockSpec((IDX_TQ, LANES), lambda i: (i, 0)),
                  pl.BlockSpec((t, IDX_SLOT), lambda i: (0, 0)),
                  pl.BlockSpec((IDX_TK, IDX_TK), lambda i: (0, 0))],
        out_specs=pl.BlockSpec((IDX_TQ, t), lambda i: (i, 0)),
        out_shape=jax.ShapeDtypeStruct((t, t), BF16),
        scratch_shapes=[pltpu.VMEM((t // IDX_TK, IDX_TQ, IDX_TK), jnp.int32)],
        compiler_params=_params("parallel"),
        name="dsa_indexer",
    )(qi, wi, ki, incl)


def _attn_kernel(q_ref, k_ref, v_ref, bias_ref, o_ref, m_ref, l_ref, acc_ref, *, tq, tk):
    qb, kb = pl.program_id(0), pl.program_id(1)
    last = ((qb + 1) * tq - 1) // tk

    @pl.when(kb == 0)
    def _():
        m_ref[...] = jnp.full(m_ref.shape, NEG_BIG, F32)
        l_ref[...] = jnp.zeros_like(l_ref)
        acc_ref[...] = jnp.zeros_like(acc_ref)

    @pl.when(kb <= last)
    def _():
        bias = bias_ref[...].astype(F32)
        ones = jnp.ones((tk, LANES), v_ref.dtype)
        heads = [slice(h * DSA_DH, (h + 1) * DSA_DH) for h in range(DSA_HEADS)]
        qk = _dot_nt(q_ref[:, heads[0]], k_ref[:, heads[0]])
        for h, sl in enumerate(heads):
            s = qk + bias
            if h + 1 < DSA_HEADS:
                qk = _dot_nt(q_ref[:, heads[h + 1]], k_ref[:, heads[h + 1]])
            m_old = m_ref[h]
            m_new = jnp.maximum(m_old, jnp.max(s, axis=-1, keepdims=True))
            alpha = jnp.exp2(m_old - m_new)
            p = jnp.exp2(s - jnp.concatenate([m_new] * (tk // LANES), axis=1))
            pv = _dot(p, jnp.concatenate([v_ref[:, sl], ones], axis=1))
            l_ref[h] = alpha * l_ref[h] + pv[:, DSA_DH:]
            acc_ref[:, sl] = alpha * acc_ref[:, sl] + pv[:, :DSA_DH]
            m_ref[h] = m_new

    @pl.when(kb == last)
    def _():
        for h in range(DSA_HEADS):
            sl = slice(h * DSA_DH, (h + 1) * DSA_DH)
            o_ref[:, sl] = (acc_ref[:, sl] / l_ref[h]).astype(o_ref.dtype)


def _attention(q, k, v, bias, *, tq, tk):
    t = q.shape[0]
    last = lambda i: ((i + 1) * tq - 1) // tk
    return pl.pallas_call(
        functools.partial(_attn_kernel, tq=tq, tk=tk),
        grid=(t // tq, t // tk),
        in_specs=[pl.BlockSpec((tq, DSA_W), lambda i, j: (i, 0)),
                  pl.BlockSpec((tk, DSA_W), lambda i, j: (jnp.minimum(j, last(i)), 0)),
                  pl.BlockSpec((tk, DSA_W), lambda i, j: (jnp.minimum(j, last(i)), 0)),
                  pl.BlockSpec((tq, tk), lambda i, j: (i, jnp.minimum(j, last(i))))],
        out_specs=pl.BlockSpec((tq, DSA_W), lambda i, j: (i, 0)),
        out_shape=jax.ShapeDtypeStruct((t, DSA_W), MXU_DTYPE),
        scratch_shapes=[pltpu.VMEM((DSA_HEADS, tq, LANES), F32),
                        pltpu.VMEM((DSA_HEADS, tq, LANES), F32),
                        pltpu.VMEM((tq, DSA_W), F32)],
        compiler_params=_params("parallel", "arbitrary"),
        name="dsa_attention",
    )(q, k, v, bias)


def _pad_heads(w, heads, width, padded):
    lead = w.shape[:-1]
    w = w.reshape(lead + (heads, width))
    w = jnp.pad(w, [(0, 0)] * len(lead) + [(0, 0), (0, padded - width)])
    return w.reshape(lead + (heads * padded,))


def _pack_plan():
    src, start = {}, 0
    names = ("ml_q", "ml_k", "ml_v", "ml_i", "ml_f", "ml_o", "dsa_cq", "dsa_k", "dsa_v", "idx_k", "idx_w",
             "hg_q", "hg_f", "hg_i", "hg_g")
    for name, width in zip(names, IN_SPLITS):
        src[name] = start
        start += width
    copies = []
    for h in range(ML_HEADS):
        for dst0, name in ((OFF_ML_Q, "ml_q"), (OFF_ML_K, "ml_k")):
            copies.append((dst0 + h * ML_DQK_PAD, src[name] + h * ML_DQK, ML_DQK))
    copies += [(OFF_DSA_K, src["dsa_k"], DSA_W), (OFF_ML_V, src["ml_v"], ML_V_W), (OFF_ML_O, src["ml_o"], ML_V_W),
               (OFF_HG_Q, src["hg_q"], HG_K_W), (OFF_HG_F, src["hg_f"], HG_K_W),
               (OFF_HG_I, src["hg_i"], HG_V_W), (OFF_HG_G, src["hg_g"], HG_V_W),
               (OFF_DSA_V, src["dsa_v"], DSA_W),
               (OFF_GATES, src["ml_i"], 2 * ML_HEADS),
               (OFF_DSA_CQ, src["dsa_cq"], DSA_Q_RANK),
               (OFF_IDX, src["idx_k"], IDX_DH + IDX_HEADS)]
    return copies


def _pack_kernel(wt_ref, o_ref):
    kt = o_ref.shape[0]
    for dst, src, width in _pack_plan():
        span = -(-width // LANES) * LANES
        seg = wt_ref[src:src + span, :].T
        if span != width:
            lane = lax.broadcasted_iota(jnp.int32, seg.shape, 1)
            seg = jnp.where(lane < width, seg, 0.0)
        o_ref[:, dst:dst + span] = seg.astype(o_ref.dtype)
    tail = OFF_IDX + LANES
    o_ref[:, tail:] = jnp.zeros((kt, D_IN_PAD - tail), o_ref.dtype)


def _pack_w_in(w_in, *, kt):
    depth, d, n = w_in.shape
    wt = jnp.swapaxes(w_in, 1, 2)
    return pl.pallas_call(
        _pack_kernel,
        grid=(depth, d // kt),
        in_specs=[pl.BlockSpec((None, n, kt), lambda l, i: (l, 0, i))],
        out_specs=pl.BlockSpec((None, kt, D_IN_PAD), lambda l, i: (l, i, 0)),
        out_shape=jax.ShapeDtypeStruct((depth, d, D_IN_PAD), MXU_DTYPE),
        compiler_params=_params("parallel", "parallel"),
        name="pack_w_in",
    )(wt)


def _rope_freqs():
    def lanes(d):
        rot = d // ROPE_FRACTION
        half = rot // 2
        inv = jnp.power(ROPE_THETA, -jnp.arange(half, dtype=F32) * (2.0 / rot))
        return jnp.concatenate([inv, inv, jnp.zeros((LANES - rot,), F32)])
    return jnp.stack([lanes(DSA_DH), lanes(IDX_DH)])


def kernel(x, positions, ln_mix_g, w_in, ml_conv_w, ml_gate_b, ml_norm_g, dsa_q_norm_g, dsa_w_uq,
           hg_lb_logits, hg_norm_g, w_out, ln_mlp_g, w_up, w_down, ln_final_g):
    bsz, t, d = x.shape
    assert bsz == 1 and t % 512 == 0 and d == D_MODEL
    depth = w_in.shape[0]
    topk = min(TOPK_MAX, t // 4)
    xs = x.reshape(t, d)
    pos = positions.reshape(t, 1)

    w_in_p = _pack_w_in(w_in, kt=256)
    w_out_b = w_out.astype(MXU_DTYPE)
    w_up_b = w_up.astype(MXU_DTYPE)
    w_down_b = w_down.astype(MXU_DTYPE)
    conv_q = _pad_heads(ml_conv_w[..., :ML_QK_W], ML_HEADS, ML_DQK, ML_DQK_PAD)
    conv_k = _pad_heads(ml_conv_w[..., ML_QK_W:], ML_HEADS, ML_DQK, ML_DQK_PAD)
    conv_p = jnp.concatenate([conv_q.reshape(depth, ML_CONV, ML_HEADS, ML_DQK_PAD),
                              conv_k.reshape(depth, ML_CONV, ML_HEADS, ML_DQK_PAD)], axis=-1)
    conv_p = jnp.transpose(conv_p, (0, 2, 1, 3))
    gate_b = jnp.pad(ml_gate_b.reshape(depth, 1, 2 * ML_HEADS), ((0, 0), (0, 0), (0, LANES - 2 * ML_HEADS)))
    w_uq_p = jnp.concatenate([dsa_w_uq[..., :DSA_W],
                              _pad_heads(dsa_w_uq[..., DSA_W:], IDX_HEADS, IDX_DH, IDX_SLOT)],
                             axis=-1).astype(MXU_DTYPE)
    freqs = _rope_freqs()
    lb_cum = jnp.cumsum(jax.nn.softmax(hg_lb_logits.astype(F32), axis=0), axis=0)
    lb = lb_cum - lb_cum[:1]
    lb_tab = jnp.stack([jnp.log(lb), jnp.log1p(-lb), 1.0 - lb], axis=1)

    for layer in range(depth):
        u = _norm_matmul(xs, ln_mix_g[layer], w_in_p, layer, tm=512, tn=1024, act=False, out_dtype=F32)
        y_a = _mlstm(u, conv_p[layer], gate_b[layer], ml_norm_g[layer], rows=256)
        q_r, qi_r, k_r, v_b, ki_r, wi = _dsa_prep(u, pos, dsa_q_norm_g[layer], w_uq_p[layer], freqs, rows=256)
        bias = _indexer(qi_r, wi, ki_r, topk=topk)
        y_b = _attention(q_r, k_r, v_b, bias, tq=512, tk=1024)
        y_c = _hgrn2(u, lb_tab[layer], hg_norm_g[layer], rows=256)
        xs = _mix_out(y_a, y_b, y_c, w_out_b, layer, xs, tm=512, tn=1024)
        a = _norm_matmul(xs, ln_mlp_g[layer], w_up_b, layer, tm=512, tn=1024, act=True, out_dtype=MXU_DTYPE)
        xs = _matmul_res(a, w_down_b, layer, xs, tm=1024, tn=1024, tk=2048)
    return _final_norm(xs, ln_final_g, tm=256).reshape(bsz, t, d)
```

```python
import functools

import jax
import jax.numpy as jnp
import numpy as np
from jax import lax
from jax.experimental import pallas as pl
from jax.experimental.pallas import tpu as pltpu

F32 = jnp.float32
BF16 = jnp.bfloat16
MXU_DTYPE = jnp.bfloat16

D_MODEL = 4096
DEPTH = 4
CHUNK = 64
EPS = 1e-6
ROPE_THETA = 500000.0
ROPE_FRACTION = 4
ML_HEADS = 4
ML_DV = 384
ML_DQK = ML_DV // 2
ML_DQK_PAD = 256
ML_CONV = 4
ML_GATE_CAP = 15.0
DSA_HEADS = 8
DSA_DH = 128
DSA_Q_RANK = 384
IDX_HEADS = 8
IDX_DH = 64
IDX_SLOT = 128
TOPK_MAX = 256
HG_HEADS = 12
HG_DK = 128
HG_DV = 128
D_FF = 4 * D_MODEL

ML_QK_W = ML_HEADS * ML_DQK
ML_V_W = ML_HEADS * ML_DV
DSA_W = DSA_HEADS * DSA_DH
IDX_W = IDX_HEADS * IDX_DH
HG_K_W = HG_HEADS * HG_DK
HG_V_W = HG_HEADS * HG_DV
D_MIX = ML_V_W + DSA_W + HG_V_W
IN_SPLITS = (ML_QK_W, ML_QK_W, ML_V_W, ML_HEADS, ML_HEADS, ML_V_W,
             DSA_Q_RANK, DSA_W, DSA_W, IDX_DH, IDX_HEADS,
             HG_K_W, HG_K_W, HG_V_W, HG_V_W)

LANES = 128
VMEM_LIMIT = 56 * 1024 * 1024

OFF_ML_Q = 0
OFF_ML_K = 1024
OFF_DSA_K = 2048
OFF_ML_V = 3072
OFF_ML_O = 4608
OFF_HG_Q = 6144
OFF_HG_F = 7680
OFF_HG_I = 9216
OFF_HG_G = 10752
OFF_DSA_V = 12288
OFF_GATES = 13312
OFF_DSA_CQ = 13440
OFF_IDX = 13824
D_IN_PAD = 14336

LOG2E = 1.4426950408889634
Q_SCALE = DSA_DH ** -0.5 * LOG2E

INT_MIN = -2 ** 31
KEY_NEG_INF = -2139095041
NEG_BIG = -1e30


def _mxu(a):
    return a.astype(MXU_DTYPE)


def _dot(a, b):
    return jnp.dot(_mxu(a), _mxu(b), preferred_element_type=F32)


def _dot_nt(a, b):
    return lax.dot_general(_mxu(a), _mxu(b), (((1,), (1,)), ((), ())), preferred_element_type=F32)


def _dot_tn(a, b):
    return lax.dot_general(_mxu(a), _mxu(b), (((0,), (0,)), ((), ())), preferred_element_type=F32)


def _dot_f32(a, b):
    return jnp.dot(a, b, precision=lax.Precision.HIGHEST, preferred_element_type=F32)


def _sigmoid(x):
    return 1.0 / (1.0 + jnp.exp(-x))


def _log_sigmoid(x):
    return jnp.minimum(x, 0.0) - jnp.log1p(jnp.exp(-jnp.abs(x)))


def _params(*sem):
    return pltpu.CompilerParams(dimension_semantics=sem, vmem_limit_bytes=VMEM_LIMIT)


BF16_SUBLANES = 16


def _mixer_call(kernel_fn, *, grid, in_specs, out_spec, out_shape, scratch_shapes, name, operands, cast=None):
    params = _params("parallel", "arbitrary")
    if cast is None:
        return pl.pallas_call(kernel_fn, grid=grid, in_specs=in_specs, out_specs=out_spec, out_shape=out_shape,
                              scratch_shapes=scratch_shapes, compiler_params=params, name=name)(*operands)
    w, layer = cast
    rows, cols = w.shape[1:]
    blocks = grid[0] * grid[1]
    while rows % blocks or (rows // blocks) % BF16_SUBLANES:
        blocks -= 1
    step = lambda a, b: jnp.minimum(a * grid[1] + b, blocks - 1)
    n_in = len(in_specs)

    def body(*refs):
        w_ref, w_out = refs[n_in], refs[n_in + 2]
        w_out[...] = w_ref[...].astype(w_out.dtype)
        kernel_fn(*refs[:n_in], refs[n_in + 1], *refs[n_in + 3:])

    return pl.pallas_call(
        body, grid=grid,
        in_specs=in_specs + [pl.BlockSpec((None, rows // blocks, cols), lambda a, b: (layer, step(a, b), 0))],
        out_specs=(out_spec, pl.BlockSpec((rows // blocks, cols), lambda a, b: (step(a, b), 0))),
        out_shape=(out_shape, jax.ShapeDtypeStruct((rows, cols), MXU_DTYPE)),
        scratch_shapes=scratch_shapes, compiler_params=params, name=name)(*operands, w)


def _norm_matmul_kernel(x_ref, g_ref, w_ref, o_ref, h_ref, *, act):
    @pl.when(pl.program_id(1) == 0)
    def _():
        x = x_ref[...]
        ms = jnp.mean(x * x, axis=-1, keepdims=True)
        h_ref[...] = (x * lax.rsqrt(ms + EPS) * g_ref[...]).astype(h_ref.dtype)

    y = jnp.dot(h_ref[...], w_ref[...], preferred_element_type=F32)
    if act:
        y = jnp.square(jnp.maximum(y, 0.0))
    o_ref[...] = y.astype(o_ref.dtype)


def _norm_matmul(x, g, w, layer, *, tm, tn, act, out_dtype):
    m, k = x.shape
    n = w.shape[2]
    return pl.pallas_call(
        functools.partial(_norm_matmul_kernel, act=act),
        grid=(m // tm, n // tn),
        in_specs=[pl.BlockSpec((tm, k), lambda i, j: (i, 0)),
                  pl.BlockSpec((1, k), lambda i, j: (0, 0)),
                  pl.BlockSpec((None, k, tn), lambda i, j: (layer, 0, j))],
        out_specs=pl.BlockSpec((tm, tn), lambda i, j: (i, j)),
        out_shape=jax.ShapeDtypeStruct((m, n), out_dtype),
        scratch_shapes=[pltpu.VMEM((tm, k), MXU_DTYPE)],
        compiler_params=_params("parallel", "arbitrary"),
        name="norm_matmul",
    )(x, g.reshape(1, k), w)


def _matmul_res_kernel(a_ref, w_ref, r_ref, o_ref):
    part = jnp.dot(a_ref[...], w_ref[...], preferred_element_type=F32)

    @pl.when(pl.program_id(2) == 0)
    def _():
        o_ref[...] = r_ref[...] + part

    @pl.when(pl.program_id(2) != 0)
    def _():
        o_ref[...] += part


def _matmul_res(a, w, layer, res, *, tm, tn, tk):
    m, k = a.shape
    n = w.shape[2]
    return pl.pallas_call(
        _matmul_res_kernel,
        grid=(m // tm, n // tn, k // tk),
        in_specs=[pl.BlockSpec((tm, tk), lambda i, j, kk: (i, kk)),
                  pl.BlockSpec((None, tk, tn), lambda i, j, kk: (layer, kk, j)),
                  pl.BlockSpec((tm, tn), lambda i, j, kk: (i, j))],
        out_specs=pl.BlockSpec((tm, tn), lambda i, j, kk: (i, j)),
        out_shape=jax.ShapeDtypeStruct((m, n), F32),
        compiler_params=_params("parallel", "parallel", "arbitrary"),
        name="matmul_res",
    )(a, w, res)


def _mix_out_kernel(ya_ref, yb_ref, yc_ref, w_ref, r_ref, o_ref):
    acc = r_ref[...]
    acc += jnp.dot(ya_ref[...], w_ref[0:ML_V_W, :], preferred_element_type=F32)
    acc += jnp.dot(yb_ref[...], w_ref[ML_V_W:ML_V_W + DSA_W, :], preferred_element_type=F32)
    acc += jnp.dot(yc_ref[...], w_ref[ML_V_W + DSA_W:D_MIX, :], preferred_element_type=F32)
    o_ref[...] = acc


def _mix_out(ya, yb, yc, w, layer, res, *, tm, tn):
    m = ya.shape[0]
    n = w.shape[2]
    return pl.pallas_call(
        _mix_out_kernel,
        grid=(m // tm, n // tn),
        in_specs=[pl.BlockSpec((tm, ML_V_W), lambda i, j: (i, 0)),
                  pl.BlockSpec((tm, DSA_W), lambda i, j: (i, 0)),
                  pl.BlockSpec((tm, HG_V_W), lambda i, j: (i, 0)),
                  pl.BlockSpec((None, D_MIX, tn), lambda i, j: (layer, 0, j)),
                  pl.BlockSpec((tm, tn), lambda i, j: (i, j))],
        out_specs=pl.BlockSpec((tm, tn), lambda i, j: (i, j)),
        out_shape=jax.ShapeDtypeStruct((m, n), F32),
        compiler_params=_params("parallel", "parallel"),
        name="mix_out",
    )(ya, yb, yc, w, res)


def _final_norm_kernel(x_ref, g_ref, o_ref):
    x = x_ref[...]
    ms = jnp.mean(x * x, axis=-1, keepdims=True)
    o_ref[...] = x * lax.rsqrt(ms + EPS) * g_ref[...]


def _final_norm(x, g, *, tm):
    m, k = x.shape
    return pl.pallas_call(
        _final_norm_kernel,
        grid=(m // tm,),
        in_specs=[pl.BlockSpec((tm, k), lambda i: (i, 0)),
                  pl.BlockSpec((1, k), lambda i: (0, 0))],
        out_specs=pl.BlockSpec((tm, k), lambda i: (i, 0)),
        out_shape=jax.ShapeDtypeStruct((m, k), F32),
        compiler_params=_params("parallel"),
        name="final_norm",
    )(x, g.reshape(1, k))


def _mlstm_kernel(q_ref, k_ref, v_ref, o_ref, gt_ref, cw_ref, gb_ref, ng_ref, out_ref,
                  xbuf, c_ref, *, rows):
    head = pl.program_id(0)

    @pl.when(pl.program_id(1) == 0)
    def _():
        xbuf[0:8, :] = jnp.zeros((8, 2 * ML_DQK_PAD), F32)
        c_ref[...] = jnp.zeros_like(c_ref)

    xbuf[8:8 + rows, 0:ML_DQK_PAD] = q_ref[...]
    xbuf[8:8 + rows, ML_DQK_PAD:] = k_ref[...]
    cw = cw_ref[...]
    acc = xbuf[8:8 + rows, :] * cw[ML_CONV - 1:ML_CONV, :]
    for j in range(1, ML_CONV):
        acc += xbuf[8 - j:8 - j + rows, :] * cw[ML_CONV - 1 - j:ML_CONV - j, :]
    xbuf[0:8, :] = xbuf[rows:rows + 8, :]
    qk = acc * _sigmoid(acc)
    q = qk[:, 0:ML_DQK_PAD]
    k = qk[:, ML_DQK_PAD:] * (ML_DQK ** -0.5)

    capped = ML_GATE_CAP * jnp.tanh((gt_ref[...] + gb_ref[...]) * (1.0 / ML_GATE_CAP))
    lsig = _log_sigmoid(capped)
    r_i = lax.broadcasted_iota(jnp.int32, (rows, rows), 0)
    c_i = lax.broadcasted_iota(jnp.int32, (rows, rows), 1)
    causal = c_i <= r_i
    b_all = _dot_f32(jnp.where(causal, 1.0, 0.0), lsig)
    lane = lax.broadcasted_iota(jnp.int32, (rows, LANES), 1)
    ig_col = jnp.sum(jnp.where(lane == head, capped, 0.0), axis=1, keepdims=True)
    b_col = jnp.sum(jnp.where(lane == head + ML_HEADS, b_all, 0.0), axis=1, keepdims=True)
    sub = lax.broadcasted_iota(jnp.int32, (LANES, rows), 0)
    ig_row = jnp.sum(jnp.where(sub == head, capped.T, 0.0), axis=0, keepdims=True)
    b_row = jnp.sum(jnp.where(sub == head + ML_HEADS, b_all.T, 0.0), axis=0, keepdims=True)

    dmat = jnp.exp(jnp.where(causal, b_col - b_row + ig_row, -jnp.inf))
    s = _dot_nt(q, k) * dmat
    one_col = jnp.where(lane == 0, 1.0, 0.0)
    v_ext = jnp.concatenate([v_ref[...], one_col], axis=1)
    c_old = c_ref[...]
    num_ext = _dot(s, v_ext) + jnp.exp(b_col) * _dot(q, c_old)
    num = num_ext[:, 0:ML_DV]
    den = num_ext[:, ML_DV:ML_DV + 1]
    hh = num / jnp.maximum(jnp.abs(den), 1.0)

    b_last = b_col[rows - 1:rows, :]
    w_s = jnp.exp(b_last - b_col + ig_col)
    c_ref[...] = jnp.exp(b_last) * c_old + _dot_tn(k, w_s * v_ext)

    ms = jnp.mean(hh * hh, axis=-1, keepdims=True)
    y = hh * lax.rsqrt(ms + EPS) * ng_ref[...]
    out_ref[...] = (_sigmoid(o_ref[...]) * y).astype(out_ref.dtype)


def _mlstm(u, conv_w, gate_b, norm_g, *, rows, cast=None):
    t = u.shape[0]
    qb, kb = OFF_ML_Q // ML_DQK_PAD, OFF_ML_K // ML_DQK_PAD
    vb, ob = OFF_ML_V // ML_DV, OFF_ML_O // ML_DV
    gb = OFF_GATES // LANES
    return _mixer_call(
        functools.partial(_mlstm_kernel, rows=rows),
        grid=(ML_HEADS, t // rows),
        in_specs=[pl.BlockSpec((rows, ML_DQK_PAD), lambda h, c: (c, qb + h)),
                  pl.BlockSpec((rows, ML_DQK_PAD), lambda h, c: (c, kb + h)),
                  pl.BlockSpec((rows, ML_DV), lambda h, c: (c, vb + h)),
                  pl.BlockSpec((rows, ML_DV), lambda h, c: (c, ob + h)),
                  pl.BlockSpec((rows, LANES), lambda h, c: (c, gb)),
                  pl.BlockSpec((None, ML_CONV, 2 * ML_DQK_PAD), lambda h, c: (h, 0, 0)),
                  pl.BlockSpec((1, LANES), lambda h, c: (0, 0)),
                  pl.BlockSpec((1, ML_DV), lambda h, c: (0, h))],
        out_spec=pl.BlockSpec((rows, ML_DV), lambda h, c: (c, h)),
        out_shape=jax.ShapeDtypeStruct((t, ML_V_W), MXU_DTYPE),
        scratch_shapes=[pltpu.VMEM((rows + 8, 2 * ML_DQK_PAD), F32),
                        pltpu.VMEM((ML_DQK_PAD, ML_DV + LANES), F32)],
        name="mlstm",
        operands=(u, u, u, u, u, conv_w, gate_b, norm_g.reshape(1, ML_V_W)),
        cast=cast)


SUBLANES = 8


def _pair_level_table(rows):
    t = np.arange(rows)[:, None]
    s = np.arange(rows)[None, :]
    x = np.bitwise_xor(t, s)
    lvl = np.floor(np.log2(np.maximum(x, 1))).astype(np.int32)
    return jnp.asarray(np.where(t > s, lvl, -1).astype(np.int32))


def _block_sums(log_f, rows):
    sub = lax.broadcasted_iota(jnp.int32, log_f.shape, 0)
    groups = rows // SUBLANES

    def row_of_group(x, j):
        x3 = x.reshape(groups, SUBLANES, x.shape[-1])
        return jnp.broadcast_to(x3[:, j:j + 1, :], x3.shape).reshape(x.shape)

    c = log_f
    e = jnp.zeros_like(log_f)
    out = [(c, e)]
    odd = (sub & 1) == 1
    c, e = (c + jnp.where(odd, pltpu.roll(c, 1, 0), 0.0),
            e + jnp.where(odd, 0.0, pltpu.roll(c, rows - 1, 0)))
    out.append((c, e))
    r8 = sub & 7
    c, e = (c + jnp.where((r8 == 2) | (r8 == 3), row_of_group(c, 1),
                          jnp.where((r8 == 6) | (r8 == 7), row_of_group(c, 5), 0.0)),
            e + jnp.where((r8 == 0) | (r8 == 1), row_of_group(c, 3),
                          jnp.where((r8 == 4) | (r8 == 5), row_of_group(c, 7), 0.0)))
    out.append((c, e))
    c, e = (c + jnp.where(r8 >= 4, row_of_group(c, 3), 0.0),
            e + jnp.where(r8 < 4, row_of_group(c, 7), 0.0))
    out.append((c, e))
    m = SUBLANES
    while m < rows:
        cs, es = [], []
        for p in range(rows // (2 * m)):
            lo, mid, hi = 2 * m * p, 2 * m * p + m, 2 * m * (p + 1)
            cs += [c[lo:mid], c[mid:hi] + c[mid - 1:mid]]
            es += [e[lo:mid] + c[hi - 1:hi], e[mid:hi]]
        c, e = jnp.concatenate(cs, axis=0), jnp.concatenate(es, axis=0)
        out.append((c, e))
        m *= 2
    return out


def _hgrn2_kernel(q_ref, f_ref, i_ref, g_ref, lb_ref, ng_ref, lvl_ref, out_ref, st_ref, *, rows):
    @pl.when(pl.program_id(1) == 0)
    def _():
        st_ref[...] = jnp.zeros_like(st_ref)

    qp = q_ref[...]
    q = qp * _sigmoid(qp)
    fp = f_ref[...]
    v = i_ref[...]
    gp = g_ref[...]
    log_lb = lb_ref[0:1, :]
    x2 = lb_ref[1:2, :] + _log_sigmoid(fp)
    log_f = jnp.maximum(log_lb, x2) + jnp.log1p(jnp.exp(-jnp.abs(log_lb - x2)))
    k = lb_ref[2:3, :] * _sigmoid(-fp)

    sums = _block_sums(log_f * LOG2E, rows)
    lvl = lvl_ref[...]
    tiles = rows // LANES
    tile_rows = [slice(r * LANES, (r + 1) * LANES) for r in range(tiles)]
    diag = [jnp.zeros((LANES, LANES), F32) for _ in range(tiles)]
    o_tiles = [None] * tiles
    for level, (c_m, e_m) in enumerate(sums[:-1]):
        m = 1 << level
        qh = q * jnp.exp2(c_m)
        kh = k * jnp.exp2(e_m)
        if m < LANES:
            for r, sl in enumerate(tile_rows):
                diag[r] = jnp.where(lvl == level, _dot_nt(qh[sl], kh[sl]), diag[r])
        else:
            for p in range(rows // (2 * m)):
                lo, mid = 2 * m * p, 2 * m * p + m
                for r in range(mid // LANES, (mid + m) // LANES):
                    part = _dot(_dot_nt(qh[tile_rows[r]], kh[lo:mid]), v[lo:mid])
                    o_tiles[r] = part if o_tiles[r] is None else o_tiles[r] + part
    for r, sl in enumerate(tile_rows):
        part = _dot(diag[r], v[sl])
        o_tiles[r] = part if o_tiles[r] is None else o_tiles[r] + part
    b, after = sums[-1]

    st = st_ref[...]
    o = (jnp.concatenate(o_tiles, axis=0) + jnp.sum(q * k, axis=-1, keepdims=True) * v
         + _dot_nt(q * jnp.exp2(b), st))
    st_ref[...] = st * jnp.exp2(b[rows - 1:rows, :]) + _dot_tn(v, k * jnp.exp2(after))

    ms = jnp.mean(o * o, axis=-1, keepdims=True)
    y = o * lax.rsqrt(ms + EPS) * ng_ref[...]
    out_ref[...] = (y * (gp * _sigmoid(gp))).astype(out_ref.dtype)


def _hgrn2(u, lb_tab, norm_g, *, rows, cast=None):
    t = u.shape[0]
    qb, fb = OFF_HG_Q // HG_DK, OFF_HG_F // HG_DK
    ib, gb = OFF_HG_I // HG_DV, OFF_HG_G // HG_DV
    return _mixer_call(
        functools.partial(_hgrn2_kernel, rows=rows),
        grid=(HG_HEADS, t // rows),
        in_specs=[pl.BlockSpec((rows, HG_DK), lambda h, c: (c, qb + h)),
                  pl.BlockSpec((rows, HG_DK), lambda h, c: (c, fb + h)),
                  pl.BlockSpec((rows, HG_DV), lambda h, c: (c, ib + h)),
                  pl.BlockSpec((rows, HG_DV), lambda h, c: (c, gb + h)),
                  pl.BlockSpec((3, HG_DK), lambda h, c: (0, h)),
                  pl.BlockSpec((1, HG_DV), lambda h, c: (0, h)),
                  pl.BlockSpec((LANES, LANES), lambda h, c: (0, 0))],
        out_spec=pl.BlockSpec((rows, HG_DV), lambda h, c: (c, h)),
        out_shape=jax.ShapeDtypeStruct((t, HG_V_W), MXU_DTYPE),
        scratch_shapes=[pltpu.VMEM((HG_DV, HG_DK), F32)],
        name="hgrn2",
        operands=(u, u, u, u, lb_tab, norm_g.reshape(1, HG_V_W), _pair_level_table(LANES)),
        cast=cast)


def _rope_slab(u, cc, sa, sb, half):
    return u * cc + pltpu.roll(u, LANES - half, 1) * sa + pltpu.roll(u, half, 1) * sb


def _rope_coeffs(ang, half):
    lane = lax.broadcasted_iota(jnp.int32, ang.shape, 1)
    cos, sin = jnp.cos(ang), jnp.sin(ang)
    cc = jnp.where(lane < 2 * half, cos, 1.0)
    sa = jnp.where(lane < half, -sin, 0.0)
    sb = jnp.where((lane >= half) & (lane < 2 * half), sin, 0.0)
    return cc, sa, sb


def _dsa_prep_kernel(cq_ref, k_ref, v_ref, idx_ref, pos_ref, g_ref, w_ref, fr_ref,
                     q_out, qi_out, k_out, v_out, ki_out, wi_out):
    pos = pos_ref[...].astype(F32)
    cq = cq_ref[...]
    ms = jnp.mean(cq * cq, axis=-1, keepdims=True)
    hq = cq * lax.rsqrt(ms + EPS) * g_ref[...]
    q_all = _dot(hq, w_ref[...])

    half_a = DSA_DH // ROPE_FRACTION // 2
    half_i = IDX_DH // ROPE_FRACTION // 2
    ca = _rope_coeffs(pos * fr_ref[0:1, :], half_a)
    ci = _rope_coeffs(pos * fr_ref[1:2, :], half_i)

    kk = k_ref[...]
    for h in range(DSA_HEADS):
        sl = slice(h * DSA_DH, (h + 1) * DSA_DH)
        q_out[:, sl] = (_rope_slab(q_all[:, sl], *ca, half_a) * Q_SCALE).astype(q_out.dtype)
        k_out[:, sl] = _rope_slab(kk[:, sl], *ca, half_a).astype(k_out.dtype)
    for h in range(IDX_HEADS):
        src = slice(DSA_W + h * IDX_SLOT, DSA_W + (h + 1) * IDX_SLOT)
        qi_out[h] = _rope_slab(q_all[:, src], *ci, half_i).astype(qi_out.dtype)
    v_out[...] = v_ref[...].astype(v_out.dtype)

    idx = idx_ref[...]
    lane = lax.broadcasted_iota(jnp.int32, idx.shape, 1)
    ki = _rope_slab(jnp.where(lane < IDX_DH, idx, 0.0), *ci, half_i)
    ki_out[...] = ki.astype(ki_out.dtype)
    wi = pltpu.roll(idx, LANES - IDX_DH, 1) * (IDX_HEADS ** -0.5 * IDX_DH ** -0.5)
    wi_out[...] = jnp.where(lane < IDX_HEADS, wi, 0.0)


def _dsa_prep(u, pos, q_norm_g, w_uq_pad, freqs, *, rows):
    t = u.shape[0]
    nq = DSA_W + IDX_HEADS * IDX_SLOT
    outs = (jax.ShapeDtypeStruct((t, DSA_W), MXU_DTYPE),
            jax.ShapeDtypeStruct((IDX_HEADS, t, IDX_SLOT), MXU_DTYPE),
            jax.ShapeDtypeStruct((t, DSA_W), MXU_DTYPE),
            jax.ShapeDtypeStruct((t, DSA_W), MXU_DTYPE),
            jax.ShapeDtypeStruct((t, IDX_SLOT), MXU_DTYPE),
            jax.ShapeDtypeStruct((t, LANES), F32))
    row_spec = lambda w, blk: pl.BlockSpec((rows, w), lambda i: (i, blk))
    return pl.pallas_call(
        _dsa_prep_kernel,
        grid=(t // rows,),
        in_specs=[row_spec(DSA_Q_RANK, OFF_DSA_CQ // DSA_Q_RANK),
                  row_spec(DSA_W, OFF_DSA_K // DSA_W),
                  row_spec(DSA_W, OFF_DSA_V // DSA_W),
                  row_spec(LANES, OFF_IDX // LANES),
                  pl.BlockSpec((rows, 1), lambda i: (i, 0)),
                  pl.BlockSpec((1, DSA_Q_RANK), lambda i: (0, 0)),
                  pl.BlockSpec((DSA_Q_RANK, nq), lambda i: (0, 0)),
                  pl.BlockSpec((2, LANES), lambda i: (0, 0))],
        out_specs=(row_spec(DSA_W, 0),
                   pl.BlockSpec((IDX_HEADS, rows, IDX_SLOT), lambda i: (0, i, 0)),
                   row_spec(DSA_W, 0), row_spec(DSA_W, 0), row_spec(IDX_SLOT, 0), row_spec(LANES, 0)),
        out_shape=outs,
        compiler_params=_params("parallel"),
        name="dsa_prep",
    )(u, u, u, u, pos, q_norm_g.reshape(1, DSA_Q_RANK), w_uq_pad, freqs)


IDX_TQ = 256
IDX_TK = 512
IDX_SLAB = 128


def _indexer_kernel(qi_ref, wi_ref, ki_ref, incl_ref, bias_ref, key_ref, *, topk):
    tq, tk = IDX_TQ, IDX_TK
    qb = pl.program_id(0)
    nkb = ((qb + 1) * tq + tk - 1) // tk
    q_all = qi_ref[...].reshape(IDX_HEADS * tq, IDX_SLOT)
    w = wi_ref[...]
    w_col = jnp.concatenate([w[:, h:h + 1] for h in range(IDX_HEADS)], axis=0)
    row_chunk = (qb * tq + lax.broadcasted_iota(jnp.int32, (tq, 1), 0)) // CHUNK
    col_in_blk = lax.broadcasted_iota(jnp.int32, (1, tk), 1)

    def score_body(kb, carry):
        kt = ki_ref[pl.ds(pl.multiple_of(kb * tk, tk), tk), :]
        weighted = jnp.maximum(_dot_nt(q_all, kt), 0.0) * w_col
        parts = [weighted[h * tq:(h + 1) * tq] for h in range(IDX_HEADS)]
        while len(parts) > 1:
            parts = [a + b for a, b in zip(parts[0::2], parts[1::2])]
        col_chunk = (kb * tk + col_in_blk) // CHUNK
        score = jnp.where(col_chunk <= row_chunk, parts[0], -jnp.inf)
        bits = pltpu.bitcast(score, jnp.int32)
        bits = jnp.where(bits == INT_MIN, 0, bits)
        key_ref[kb] = jnp.where(bits < 0, bits ^ 0x7FFFFFFF, bits)
        return carry

    lax.fori_loop(0, nkb, score_body, 0)

    ones_mat = jnp.ones((LANES, LANES), MXU_DTYPE)

    def count_ge(cand):
        counts = []
        for r0 in range(0, tq, IDX_SLAB):
            cand_r = cand[r0:r0 + IDX_SLAB]

            def body(kb, cnt, r0=r0, cand_r=cand_r):
                for j in range(tk // LANES):
                    cnt += jnp.where(key_ref[kb, r0:r0 + IDX_SLAB, j * LANES:(j + 1) * LANES] >= cand_r, 1, 0)
                return cnt
            counts.append(lax.fori_loop(0, nkb, body, jnp.zeros((IDX_SLAB, LANES), jnp.int32)))
        cnt = jnp.concatenate(counts, axis=0)
        return jnp.dot(cnt.astype(F32).astype(MXU_DTYPE), ones_mat, preferred_element_type=F32)

    zero = jnp.zeros((tq, LANES), jnp.int32)
    cnt0 = count_ge(zero)
    thr_rep = jnp.where(cnt0 >= topk, zero, INT_MIN)
    at_thr = jnp.where(cnt0 >= topk, cnt0, (nkb * tk).astype(F32))

    def bit_body(i, carry):
        thr_rep, at_thr = carry
        cand = thr_rep + jnp.left_shift(jnp.int32(1), 30 - i)
        cnt = count_ge(cand)
        ok = cnt >= topk
        return jnp.where(ok, cand, thr_rep), jnp.where(ok, cnt, at_thr)

    thr_rep, at_thr = lax.fori_loop(0, 31, bit_body, (thr_rep, at_thr))
    thr = thr_rep[:, 0:1]
    exact = jnp.all(at_thr == topk)

    bias_ref[...] = jnp.full(bias_ref.shape, NEG_BIG, bias_ref.dtype)

    @pl.when(exact)
    def _():
        def emit_body(kb, carry):
            key = key_ref[kb]
            take = (key >= thr) & (key > KEY_NEG_INF)
            bias_ref[:, pl.ds(pl.multiple_of(kb * tk, tk), tk)] = jnp.where(take, 0.0, NEG_BIG).astype(bias_ref.dtype)
            return carry

        lax.fori_loop(0, nkb, emit_body, 0)

    @pl.when(jnp.logical_not(exact))
    def _():
        need = topk - count_ge(thr_rep + 1)[:, 0:1]

        def emit_body(kb, seen):
            key = key_ref[kb]
            eq = key == thr
            eq_f = jnp.where(eq, 1.0, 0.0)
            rank = seen + jnp.dot(eq_f.astype(MXU_DTYPE), incl_ref[...], preferred_element_type=F32)
            take = ((key > thr) | (eq & (rank <= need))) & (key > KEY_NEG_INF)
            bias_ref[:, pl.ds(pl.multiple_of(kb * tk, tk), tk)] = jnp.where(take, 0.0, NEG_BIG).astype(bias_ref.dtype)
            return seen + jnp.sum(eq_f, axis=1, keepdims=True)

        lax.fori_loop(0, nkb, emit_body, jnp.zeros((tq, 1), F32))


def _indexer(qi, wi, ki, *, topk):
    t = ki.shape[0]
    incl = jnp.asarray(np.triu(np.ones((IDX_TK, IDX_TK), np.float32)), MXU_DTYPE)
    return pl.pallas_call(
        functools.partial(_indexer_kernel, topk=topk),
        grid=(t // IDX_TQ,),
        in_specs=[pl.BlockSpec((IDX_HEADS, IDX_TQ, IDX_SLOT), lambda i: (0, i, 0)),
                  pl.BlockSpec((IDX_TQ, LANES), lambda i: (i, 0)),
                  pl.BlockSpec((t, IDX_SLOT), lambda i: (0, 0)),
                  pl.BlockSpec((IDX_TK, IDX_TK), lambda i: (0, 0))],
        out_specs=pl.BlockSpec((IDX_TQ, t), lambda i: (i, 0)),
        out_shape=jax.ShapeDtypeStruct((t, t), BF16),
        scratch_shapes=[pltpu.VMEM((t // IDX_TK, IDX_TQ, IDX_TK), jnp.int32)],
        compiler_params=_params("parallel"),
        name="dsa_indexer",
    )(qi, wi, ki, incl)


def _attn_kernel(q_ref, k_ref, v_ref, bias_ref, o_ref, m_ref, l_ref, acc_ref, *, tq, tk):
    qb, kb = pl.program_id(0), pl.program_id(1)
    last = ((qb + 1) * tq - 1) // tk

    @pl.when(kb == 0)
    def _():
        m_ref[...] = jnp.full(m_ref.shape, NEG_BIG, F32)
        l_ref[...] = jnp.zeros_like(l_ref)
        acc_ref[...] = jnp.zeros_like(acc_ref)

    @pl.when(kb <= last)
    def _():
        bias = bias_ref[...].astype(F32)
        ones = jnp.ones((tk, LANES), v_ref.dtype)
        heads = [slice(h * DSA_DH, (h + 1) * DSA_DH) for h in range(DSA_HEADS)]
        qk = _dot_nt(q_ref[:, heads[0]], k_ref[:, heads[0]])
        for h, sl in enumerate(heads):
            s = qk + bias
            if h + 1 < DSA_HEADS:
                qk = _dot_nt(q_ref[:, heads[h + 1]], k_ref[:, heads[h + 1]])
            m_old = m_ref[h]
            m_new = jnp.maximum(m_old, jnp.max(s, axis=-1, keepdims=True))
            alpha = jnp.exp2(m_old - m_new)
            p = jnp.exp2(s - jnp.concatenate([m_new] * (tk // LANES), axis=1))
            pv = _dot(p, jnp.concatenate([v_ref[:, sl], ones], axis=1))
            l_ref[h] = alpha * l_ref[h] + pv[:, DSA_DH:]
            acc_ref[:, sl] = alpha * acc_ref[:, sl] + pv[:, :DSA_DH]
            m_ref[h] = m_new

    @pl.when(kb == last)
    def _():
        for h in range(DSA_HEADS):
            sl = slice(h * DSA_DH, (h + 1) * DSA_DH)
            o_ref[:, sl] = (acc_ref[:, sl] / l_ref[h]).astype(o_ref.dtype)


def _attention(q, k, v, bias, *, tq, tk, cast=None):
    t = q.shape[0]
    last = lambda i: ((i + 1) * tq - 1) // tk
    return _mixer_call(
        functools.partial(_attn_kernel, tq=tq, tk=tk),
        grid=(t // tq, t // tk),
        in_specs=[pl.BlockSpec((tq, DSA_W), lambda i, j: (i, 0)),
                  pl.BlockSpec((tk, DSA_W), lambda i, j: (jnp.minimum(j, last(i)), 0)),
                  pl.BlockSpec((tk, DSA_W), lambda i, j: (jnp.minimum(j, last(i)), 0)),
                  pl.BlockSpec((tq, tk), lambda i, j: (i, jnp.minimum(j, last(i))))],
        out_spec=pl.BlockSpec((tq, DSA_W), lambda i, j: (i, 0)),
        out_shape=jax.ShapeDtypeStruct((t, DSA_W), MXU_DTYPE),
        scratch_shapes=[pltpu.VMEM((DSA_HEADS, tq, LANES), F32),
                        pltpu.VMEM((DSA_HEADS, tq, LANES), F32),
                        pltpu.VMEM((tq, DSA_W), F32)],
        name="dsa_attention",
        operands=(q, k, v, bias),
        cast=cast)


def _pad_heads(w, heads, width, padded):
    lead = w.shape[:-1]
    w = w.reshape(lead + (heads, width))
    w = jnp.pad(w, [(0, 0)] * len(lead) + [(0, 0), (0, padded - width)])
    return w.reshape(lead + (heads * padded,))


def _pack_plan():
    src, start = {}, 0
    names = ("ml_q", "ml_k", "ml_v", "ml_i", "ml_f", "ml_o", "dsa_cq", "dsa_k", "dsa_v", "idx_k", "idx_w",
             "hg_q", "hg_f", "hg_i", "hg_g")
    for name, width in zip(names, IN_SPLITS):
        src[name] = start
        start += width
    copies = []
    for h in range(ML_HEADS):
        for dst0, name in ((OFF_ML_Q, "ml_q"), (OFF_ML_K, "ml_k")):
            copies.append((dst0 + h * ML_DQK_PAD, src[name] + h * ML_DQK, ML_DQK))
    copies += [(OFF_DSA_K, src["dsa_k"], DSA_W), (OFF_ML_V, src["ml_v"], ML_V_W), (OFF_ML_O, src["ml_o"], ML_V_W),
               (OFF_HG_Q, src["hg_q"], HG_K_W), (OFF_HG_F, src["hg_f"], HG_K_W),
               (OFF_HG_I, src["hg_i"], HG_V_W), (OFF_HG_G, src["hg_g"], HG_V_W),
               (OFF_DSA_V, src["dsa_v"], DSA_W),
               (OFF_GATES, src["ml_i"], 2 * ML_HEADS),
               (OFF_DSA_CQ, src["dsa_cq"], DSA_Q_RANK),
               (OFF_IDX, src["idx_k"], IDX_DH + IDX_HEADS)]
    return copies


def _pack_kernel(wt_ref, o_ref):
    kt = o_ref.shape[0]
    for dst, src, width in _pack_plan():
        span = -(-width // LANES) * LANES
        seg = wt_ref[src:src + span, :].T
        if span != width:
            lane = lax.broadcasted_iota(jnp.int32, seg.shape, 1)
            seg = jnp.where(lane < width, seg, 0.0)
        o_ref[:, dst:dst + span] = seg.astype(o_ref.dtype)
    tail = OFF_IDX + LANES
    o_ref[:, tail:] = jnp.zeros((kt, D_IN_PAD - tail), o_ref.dtype)


def _pack_w_in(w_in, *, kt):
    depth, d, n = w_in.shape
    wt = jnp.swapaxes(w_in, 1, 2)
    return pl.pallas_call(
        _pack_kernel,
        grid=(depth, d // kt),
        in_specs=[pl.BlockSpec((None, n, kt), lambda l, i: (l, 0, i))],
        out_specs=pl.BlockSpec((None, kt, D_IN_PAD), lambda l, i: (l, i, 0)),
        out_shape=jax.ShapeDtypeStruct((depth, d, D_IN_PAD), MXU_DTYPE),
        compiler_params=_params("parallel", "parallel"),
        name="pack_w_in",
    )(wt)


def _rope_freqs():
    def lanes(d):
        rot = d // ROPE_FRACTION
        half = rot // 2
        inv = jnp.power(ROPE_THETA, -jnp.arange(half, dtype=F32) * (2.0 / rot))
        return jnp.concatenate([inv, inv, jnp.zeros((LANES - rot,), F32)])
    return jnp.stack([lanes(DSA_DH), lanes(IDX_DH)])


def kernel(x, positions, ln_mix_g, w_in, ml_conv_w, ml_gate_b, ml_norm_g, dsa_q_norm_g, dsa_w_uq,
           hg_lb_logits, hg_norm_g, w_out, ln_mlp_g, w_up, w_down, ln_final_g):
    bsz, t, d = x.shape
    assert bsz == 1 and t % 512 == 0 and d == D_MODEL
    depth = w_in.shape[0]
    topk = min(TOPK_MAX, t // 4)
    xs = x.reshape(t, d)
    pos = positions.reshape(t, 1)

    w_in_p = _pack_w_in(w_in, kt=256)
    conv_q = _pad_heads(ml_conv_w[..., :ML_QK_W], ML_HEADS, ML_DQK, ML_DQK_PAD)
    conv_k = _pad_heads(ml_conv_w[..., ML_QK_W:], ML_HEADS, ML_DQK, ML_DQK_PAD)
    conv_p = jnp.concatenate([conv_q.reshape(depth, ML_CONV, ML_HEADS, ML_DQK_PAD),
                              conv_k.reshape(depth, ML_CONV, ML_HEADS, ML_DQK_PAD)], axis=-1)
    conv_p = jnp.transpose(conv_p, (0, 2, 1, 3))
    gate_b = jnp.pad(ml_gate_b.reshape(depth, 1, 2 * ML_HEADS), ((0, 0), (0, 0), (0, LANES - 2 * ML_HEADS)))
    w_uq_p = jnp.concatenate([dsa_w_uq[..., :DSA_W],
                              _pad_heads(dsa_w_uq[..., DSA_W:], IDX_HEADS, IDX_DH, IDX_SLOT)],
                             axis=-1).astype(MXU_DTYPE)
    freqs = _rope_freqs()
    lb_cum = jnp.cumsum(jax.nn.softmax(hg_lb_logits.astype(F32), axis=0), axis=0)
    lb = lb_cum - lb_cum[:1]
    lb_tab = jnp.stack([jnp.log(lb), jnp.log1p(-lb), 1.0 - lb], axis=1)

    for layer in range(depth):
        u = _norm_matmul(xs, ln_mix_g[layer], w_in_p, layer, tm=512, tn=1024, act=False, out_dtype=F32)
        y_a, w_out_b = _mlstm(u, conv_p[layer], gate_b[layer], ml_norm_g[layer], rows=256, cast=(w_out, layer))
        q_r, qi_r, k_r, v_b, ki_r, wi = _dsa_prep(u, pos, dsa_q_norm_g[layer], w_uq_p[layer], freqs, rows=256)
        bias = _indexer(qi_r, wi, ki_r, topk=topk)
        y_b, w_down_b = _attention(q_r, k_r, v_b, bias, tq=512, tk=1024, cast=(w_down, layer))
        y_c, w_up_b = _hgrn2(u, lb_tab[layer], hg_norm_g[layer], rows=256, cast=(w_up, layer))
        xs = _mix_out(y_a, y_b, y_c, w_out_b[None], 0, xs, tm=512, tn=1024)
        a = _norm_matmul(xs, ln_mlp_g[layer], w_up_b[None], 0, tm=512, tn=1024, act=True, out_dtype=MXU_DTYPE)
        xs = _matmul_res(a, w_down_b[None], 0, xs, tm=1024, tn=1024, tk=2048)
    return _final_norm(xs, ln_final_g, tm=256).reshape(bsz, t, d)
```

```python
import functools

import jax
import jax.numpy as jnp
import numpy as np
from jax import lax
from jax.experimental import pallas as pl
from jax.experimental.pallas import tpu as pltpu

F32 = jnp.float32
BF16 = jnp.bfloat16
MXU_DTYPE = jnp.bfloat16

D_MODEL = 4096
DEPTH = 4
CHUNK = 64
EPS = 1e-6
ROPE_THETA = 500000.0
ROPE_FRACTION = 4
ML_HEADS = 4
ML_DV = 384
ML_DQK = ML_DV // 2
ML_DQK_PAD = 256
ML_CONV = 4
ML_GATE_CAP = 15.0
DSA_HEADS = 8
DSA_DH = 128
DSA_Q_RANK = 384
IDX_HEADS = 8
IDX_DH = 64
IDX_SLOT = 128
TOPK_MAX = 256
HG_HEADS = 12
HG_DK = 128
HG_DV = 128
D_FF = 4 * D_MODEL

ML_QK_W = ML_HEADS * ML_DQK
ML_V_W = ML_HEADS * ML_DV
DSA_W = DSA_HEADS * DSA_DH
IDX_W = IDX_HEADS * IDX_DH
HG_K_W = HG_HEADS * HG_DK
HG_V_W = HG_HEADS * HG_DV
D_MIX = ML_V_W + DSA_W + HG_V_W
IN_SPLITS = (ML_QK_W, ML_QK_W, ML_V_W, ML_HEADS, ML_HEADS, ML_V_W,
             DSA_Q_RANK, DSA_W, DSA_W, IDX_DH, IDX_HEADS,
             HG_K_W, HG_K_W, HG_V_W, HG_V_W)

LANES = 128
VMEM_LIMIT = 56 * 1024 * 1024

OFF_ML_Q = 0
OFF_ML_K = 1024
OFF_DSA_K = 2048
OFF_ML_V = 3072
OFF_ML_O = 4608
OFF_HG_Q = 6144
OFF_HG_F = 7680
OFF_HG_I = 9216
OFF_HG_G = 10752
OFF_DSA_V = 12288
OFF_GATES = 13312
OFF_DSA_CQ = 13440
OFF_IDX = 13824
D_IN_PAD = 14336

LOG2E = 1.4426950408889634
Q_SCALE = DSA_DH ** -0.5 * LOG2E

INT_MIN = -2 ** 31
KEY_NEG_INF = -2139095041
NEG_BIG = -1e30


def _mxu(a):
    return a.astype(MXU_DTYPE)


def _dot(a, b):
    return jnp.dot(_mxu(a), _mxu(b), preferred_element_type=F32)


def _dot_nt(a, b):
    return lax.dot_general(_mxu(a), _mxu(b), (((1,), (1,)), ((), ())), preferred_element_type=F32)


def _dot_tn(a, b):
    return lax.dot_general(_mxu(a), _mxu(b), (((0,), (0,)), ((), ())), preferred_element_type=F32)


def _dot_f32(a, b):
    return jnp.dot(a, b, precision=lax.Precision.HIGHEST, preferred_element_type=F32)


def _sigmoid(x):
    return 1.0 / (1.0 + jnp.exp(-x))


def _log_sigmoid(x):
    return jnp.minimum(x, 0.0) - jnp.log1p(jnp.exp(-jnp.abs(x)))


def _params(*sem):
    return pltpu.CompilerParams(dimension_semantics=sem, vmem_limit_bytes=VMEM_LIMIT)


BF16_SUBLANES = 16


def _mixer_call(kernel_fn, *, grid, in_specs, out_spec, out_shape, scratch_shapes, name, operands, cast=None):
    params = _params("parallel", "arbitrary")
    if cast is None:
        return pl.pallas_call(kernel_fn, grid=grid, in_specs=in_specs, out_specs=out_spec, out_shape=out_shape,
                              scratch_shapes=scratch_shapes, compiler_params=params, name=name)(*operands)
    w, layer, tick, ticks, every = cast
    params = _params("arbitrary", "arbitrary")
    rows, cols = w.shape[1:]
    blocks = ticks // every
    while rows % blocks or (rows // blocks) % BF16_SUBLANES:
        blocks -= 1
    block = lambda a, b: jnp.minimum(tick(a, b) // every, blocks - 1)
    n_in = len(in_specs)

    def body(*refs):
        w_ref, w_out = refs[n_in], refs[n_in + 2]
        now = tick(pl.program_id(0), pl.program_id(1))

        @pl.when((now % every == 0) & (now // every < blocks))
        def _():
            w_out[...] = w_ref[...].astype(w_out.dtype)

        kernel_fn(*refs[:n_in], refs[n_in + 1], *refs[n_in + 3:])

    return pl.pallas_call(
        body, grid=grid,
        in_specs=in_specs + [pl.BlockSpec((None, rows // blocks, cols), lambda a, b: (layer, block(a, b), 0))],
        out_specs=(out_spec, pl.BlockSpec((rows // blocks, cols), lambda a, b: (block(a, b), 0))),
        out_shape=(out_shape, jax.ShapeDtypeStruct((rows, cols), MXU_DTYPE)),
        scratch_shapes=scratch_shapes, compiler_params=params, name=name)(*operands, w)


def _norm_matmul_kernel(x_ref, g_ref, w_ref, o_ref, h_ref, *, act):
    @pl.when(pl.program_id(1) == 0)
    def _():
        x = x_ref[...]
        ms = jnp.mean(x * x, axis=-1, keepdims=True)
        h_ref[...] = (x * lax.rsqrt(ms + EPS) * g_ref[...]).astype(h_ref.dtype)

    y = jnp.dot(h_ref[...], w_ref[...], preferred_element_type=F32)
    if act:
        y = jnp.square(jnp.maximum(y, 0.0))
    o_ref[...] = y.astype(o_ref.dtype)


def _norm_matmul(x, g, w, layer, *, tm, tn, act, out_dtype):
    m, k = x.shape
    n = w.shape[2]
    return pl.pallas_call(
        functools.partial(_norm_matmul_kernel, act=act),
        grid=(m // tm, n // tn),
        in_specs=[pl.BlockSpec((tm, k), lambda i, j: (i, 0)),
                  pl.BlockSpec((1, k), lambda i, j: (0, 0)),
                  pl.BlockSpec((None, k, tn), lambda i, j: (layer, 0, j))],
        out_specs=pl.BlockSpec((tm, tn), lambda i, j: (i, j)),
        out_shape=jax.ShapeDtypeStruct((m, n), out_dtype),
        scratch_shapes=[pltpu.VMEM((tm, k), MXU_DTYPE)],
        compiler_params=_params("parallel", "arbitrary"),
        name="norm_matmul",
    )(x, g.reshape(1, k), w)


def _matmul_res_kernel(a_ref, w_ref, r_ref, o_ref):
    part = jnp.dot(a_ref[...], w_ref[...], preferred_element_type=F32)

    @pl.when(pl.program_id(2) == 0)
    def _():
        o_ref[...] = r_ref[...] + part

    @pl.when(pl.program_id(2) != 0)
    def _():
        o_ref[...] += part


def _matmul_res(a, w, layer, res, *, tm, tn, tk):
    m, k = a.shape
    n = w.shape[2]
    return pl.pallas_call(
        _matmul_res_kernel,
        grid=(m // tm, n // tn, k // tk),
        in_specs=[pl.BlockSpec((tm, tk), lambda i, j, kk: (i, kk)),
                  pl.BlockSpec((None, tk, tn), lambda i, j, kk: (layer, kk, j)),
                  pl.BlockSpec((tm, tn), lambda i, j, kk: (i, j))],
        out_specs=pl.BlockSpec((tm, tn), lambda i, j, kk: (i, j)),
        out_shape=jax.ShapeDtypeStruct((m, n), F32),
        compiler_params=_params("parallel", "parallel", "arbitrary"),
        name="matmul_res",
    )(a, w, res)


def _mix_out_kernel(ya_ref, yb_ref, yc_ref, w_ref, r_ref, o_ref):
    acc = r_ref[...]
    acc += jnp.dot(ya_ref[...], w_ref[0:ML_V_W, :], preferred_element_type=F32)
    acc += jnp.dot(yb_ref[...], w_ref[ML_V_W:ML_V_W + DSA_W, :], preferred_element_type=F32)
    acc += jnp.dot(yc_ref[...], w_ref[ML_V_W + DSA_W:D_MIX, :], preferred_element_type=F32)
    o_ref[...] = acc


def _mix_out(ya, yb, yc, w, layer, res, *, tm, tn):
    m = ya.shape[0]
    n = w.shape[2]
    return pl.pallas_call(
        _mix_out_kernel,
        grid=(m // tm, n // tn),
        in_specs=[pl.BlockSpec((tm, ML_V_W), lambda i, j: (i, 0)),
                  pl.BlockSpec((tm, DSA_W), lambda i, j: (i, 0)),
                  pl.BlockSpec((tm, HG_V_W), lambda i, j: (i, 0)),
                  pl.BlockSpec((None, D_MIX, tn), lambda i, j: (layer, 0, j)),
                  pl.BlockSpec((tm, tn), lambda i, j: (i, j))],
        out_specs=pl.BlockSpec((tm, tn), lambda i, j: (i, j)),
        out_shape=jax.ShapeDtypeStruct((m, n), F32),
        compiler_params=_params("parallel", "parallel"),
        name="mix_out",
    )(ya, yb, yc, w, res)


def _final_norm_kernel(x_ref, g_ref, o_ref):
    x = x_ref[...]
    ms = jnp.mean(x * x, axis=-1, keepdims=True)
    o_ref[...] = x * lax.rsqrt(ms + EPS) * g_ref[...]


def _final_norm(x, g, *, tm):
    m, k = x.shape
    return pl.pallas_call(
        _final_norm_kernel,
        grid=(m // tm,),
        in_specs=[pl.BlockSpec((tm, k), lambda i: (i, 0)),
                  pl.BlockSpec((1, k), lambda i: (0, 0))],
        out_specs=pl.BlockSpec((tm, k), lambda i: (i, 0)),
        out_shape=jax.ShapeDtypeStruct((m, k), F32),
        compiler_params=_params("parallel"),
        name="final_norm",
    )(x, g.reshape(1, k))


def _mlstm_kernel(q_ref, k_ref, v_ref, o_ref, gt_ref, cw_ref, gb_ref, ng_ref, out_ref,
                  xbuf, c_ref, *, rows):
    head = pl.program_id(0)

    @pl.when(pl.program_id(1) == 0)
    def _():
        xbuf[0:8, :] = jnp.zeros((8, 2 * ML_DQK_PAD), F32)
        c_ref[...] = jnp.zeros_like(c_ref)

    xbuf[8:8 + rows, 0:ML_DQK_PAD] = q_ref[...]
    xbuf[8:8 + rows, ML_DQK_PAD:] = k_ref[...]
    cw = cw_ref[...]
    acc = xbuf[8:8 + rows, :] * cw[ML_CONV - 1:ML_CONV, :]
    for j in range(1, ML_CONV):
        acc += xbuf[8 - j:8 - j + rows, :] * cw[ML_CONV - 1 - j:ML_CONV - j, :]
    xbuf[0:8, :] = xbuf[rows:rows + 8, :]
    qk = acc * _sigmoid(acc)
    q = qk[:, 0:ML_DQK_PAD]
    k = qk[:, ML_DQK_PAD:] * (ML_DQK ** -0.5)

    capped = ML_GATE_CAP * jnp.tanh((gt_ref[...] + gb_ref[...]) * (1.0 / ML_GATE_CAP))
    lsig = _log_sigmoid(capped)
    r_i = lax.broadcasted_iota(jnp.int32, (rows, rows), 0)
    c_i = lax.broadcasted_iota(jnp.int32, (rows, rows), 1)
    causal = c_i <= r_i
    b_all = _dot_f32(jnp.where(causal, 1.0, 0.0), lsig)
    lane = lax.broadcasted_iota(jnp.int32, (rows, LANES), 1)
    ig_col = jnp.sum(jnp.where(lane == head, capped, 0.0), axis=1, keepdims=True)
    b_col = jnp.sum(jnp.where(lane == head + ML_HEADS, b_all, 0.0), axis=1, keepdims=True)
    sub = lax.broadcasted_iota(jnp.int32, (LANES, rows), 0)
    ig_row = jnp.sum(jnp.where(sub == head, capped.T, 0.0), axis=0, keepdims=True)
    b_row = jnp.sum(jnp.where(sub == head + ML_HEADS, b_all.T, 0.0), axis=0, keepdims=True)

    dmat = jnp.exp(jnp.where(causal, b_col - b_row + ig_row, -jnp.inf))
    s = _dot_nt(q, k) * dmat
    one_col = jnp.where(lane == 0, 1.0, 0.0)
    v_ext = jnp.concatenate([v_ref[...], one_col], axis=1)
    c_old = c_ref[...]
    num_ext = _dot(s, v_ext) + jnp.exp(b_col) * _dot(q, c_old)
    num = num_ext[:, 0:ML_DV]
    den = num_ext[:, ML_DV:ML_DV + 1]
    hh = num / jnp.maximum(jnp.abs(den), 1.0)

    b_last = b_col[rows - 1:rows, :]
    w_s = jnp.exp(b_last - b_col + ig_col)
    c_ref[...] = jnp.exp(b_last) * c_old + _dot_tn(k, w_s * v_ext)

    ms = jnp.mean(hh * hh, axis=-1, keepdims=True)
    y = hh * lax.rsqrt(ms + EPS) * ng_ref[...]
    out_ref[...] = (_sigmoid(o_ref[...]) * y).astype(out_ref.dtype)


def _mlstm(u, conv_w, gate_b, norm_g, *, rows, cast=None):
    t = u.shape[0]
    qb, kb = OFF_ML_Q // ML_DQK_PAD, OFF_ML_K // ML_DQK_PAD
    vb, ob = OFF_ML_V // ML_DV, OFF_ML_O // ML_DV
    gb = OFF_GATES // LANES
    return _mixer_call(
        functools.partial(_mlstm_kernel, rows=rows),
        grid=(ML_HEADS, t // rows),
        in_specs=[pl.BlockSpec((rows, ML_DQK_PAD), lambda h, c: (c, qb + h)),
                  pl.BlockSpec((rows, ML_DQK_PAD), lambda h, c: (c, kb + h)),
                  pl.BlockSpec((rows, ML_DV), lambda h, c: (c, vb + h)),
                  pl.BlockSpec((rows, ML_DV), lambda h, c: (c, ob + h)),
                  pl.BlockSpec((rows, LANES), lambda h, c: (c, gb)),
                  pl.BlockSpec((None, ML_CONV, 2 * ML_DQK_PAD), lambda h, c: (h, 0, 0)),
                  pl.BlockSpec((1, LANES), lambda h, c: (0, 0)),
                  pl.BlockSpec((1, ML_DV), lambda h, c: (0, h))],
        out_spec=pl.BlockSpec((rows, ML_DV), lambda h, c: (c, h)),
        out_shape=jax.ShapeDtypeStruct((t, ML_V_W), MXU_DTYPE),
        scratch_shapes=[pltpu.VMEM((rows + 8, 2 * ML_DQK_PAD), F32),
                        pltpu.VMEM((ML_DQK_PAD, ML_DV + LANES), F32)],
        name="mlstm",
        operands=(u, u, u, u, u, conv_w, gate_b, norm_g.reshape(1, ML_V_W)),
        cast=cast and (*cast, lambda h, c: h * (t // rows) + c, ML_HEADS * (t // rows), 1))


SUBLANES = 8
HG_CAST_EVERY = 6


def _pair_level_table(rows):
    t = np.arange(rows)[:, None]
    s = np.arange(rows)[None, :]
    x = np.bitwise_xor(t, s)
    lvl = np.floor(np.log2(np.maximum(x, 1))).astype(np.int32)
    return jnp.asarray(np.where(t > s, lvl, -1).astype(np.int32))


def _block_sums(log_f, rows):
    sub = lax.broadcasted_iota(jnp.int32, log_f.shape, 0)
    groups = rows // SUBLANES

    def row_of_group(x, j):
        x3 = x.reshape(groups, SUBLANES, x.shape[-1])
        return jnp.broadcast_to(x3[:, j:j + 1, :], x3.shape).reshape(x.shape)

    c = log_f
    e = jnp.zeros_like(log_f)
    out = [(c, e)]
    odd = (sub & 1) == 1
    c, e = (c + jnp.where(odd, pltpu.roll(c, 1, 0), 0.0),
            e + jnp.where(odd, 0.0, pltpu.roll(c, rows - 1, 0)))
    out.append((c, e))
    r8 = sub & 7
    c, e = (c + jnp.where((r8 == 2) | (r8 == 3), row_of_group(c, 1),
                          jnp.where((r8 == 6) | (r8 == 7), row_of_group(c, 5), 0.0)),
            e + jnp.where((r8 == 0) | (r8 == 1), row_of_group(c, 3),
                          jnp.where((r8 == 4) | (r8 == 5), row_of_group(c, 7), 0.0)))
    out.append((c, e))
    c, e = (c + jnp.where(r8 >= 4, row_of_group(c, 3), 0.0),
            e + jnp.where(r8 < 4, row_of_group(c, 7), 0.0))
    out.append((c, e))
    m = SUBLANES
    while m < rows:
        cs, es = [], []
        for p in range(rows // (2 * m)):
            lo, mid, hi = 2 * m * p, 2 * m * p + m, 2 * m * (p + 1)
            cs += [c[lo:mid], c[mid:hi] + c[mid - 1:mid]]
            es += [e[lo:mid] + c[hi - 1:hi], e[mid:hi]]
        c, e = jnp.concatenate(cs, axis=0), jnp.concatenate(es, axis=0)
        out.append((c, e))
        m *= 2
    return out


def _hgrn2_kernel(q_ref, f_ref, i_ref, g_ref, lb_ref, ng_ref, lvl_ref, out_ref, st_ref, *, rows):
    @pl.when(pl.program_id(1) == 0)
    def _():
        st_ref[...] = jnp.zeros_like(st_ref)

    qp = q_ref[...]
    q = qp * _sigmoid(qp)
    fp = f_ref[...]
    v = i_ref[...]
    gp = g_ref[...]
    log_lb = lb_ref[0:1, :]
    x2 = lb_ref[1:2, :] + _log_sigmoid(fp)
    log_f = jnp.maximum(log_lb, x2) + jnp.log1p(jnp.exp(-jnp.abs(log_lb - x2)))
    k = lb_ref[2:3, :] * _sigmoid(-fp)

    sums = _block_sums(log_f * LOG2E, rows)
    lvl = lvl_ref[...]
    tiles = rows // LANES
    tile_rows = [slice(r * LANES, (r + 1) * LANES) for r in range(tiles)]
    diag = [jnp.zeros((LANES, LANES), F32) for _ in range(tiles)]
    o_tiles = [None] * tiles
    for level, (c_m, e_m) in enumerate(sums[:-1]):
        m = 1 << level
        qh = q * jnp.exp2(c_m)
        kh = k * jnp.exp2(e_m)
        if m < LANES:
            for r, sl in enumerate(tile_rows):
                diag[r] = jnp.where(lvl == level, _dot_nt(qh[sl], kh[sl]), diag[r])
        else:
            for p in range(rows // (2 * m)):
                lo, mid = 2 * m * p, 2 * m * p + m
                for r in range(mid // LANES, (mid + m) // LANES):
                    part = _dot(_dot_nt(qh[tile_rows[r]], kh[lo:mid]), v[lo:mid])
                    o_tiles[r] = part if o_tiles[r] is None else o_tiles[r] + part
    for r, sl in enumerate(tile_rows):
        part = _dot(diag[r], v[sl])
        o_tiles[r] = part if o_tiles[r] is None else o_tiles[r] + part
    b, after = sums[-1]

    st = st_ref[...]
    o = (jnp.concatenate(o_tiles, axis=0) + jnp.sum(q * k, axis=-1, keepdims=True) * v
         + _dot_nt(q * jnp.exp2(b), st))
    st_ref[...] = st * jnp.exp2(b[rows - 1:rows, :]) + _dot_tn(v, k * jnp.exp2(after))

    ms = jnp.mean(o * o, axis=-1, keepdims=True)
    y = o * lax.rsqrt(ms + EPS) * ng_ref[...]
    out_ref[...] = (y * (gp * _sigmoid(gp))).astype(out_ref.dtype)


def _hgrn2(u, lb_tab, norm_g, *, rows, cast=None):
    t = u.shape[0]
    qb, fb = OFF_HG_Q // HG_DK, OFF_HG_F // HG_DK
    ib, gb = OFF_HG_I // HG_DV, OFF_HG_G // HG_DV
    return _mixer_call(
        functools.partial(_hgrn2_kernel, rows=rows),
        grid=(HG_HEADS, t // rows),
        in_specs=[pl.BlockSpec((rows, HG_DK), lambda h, c: (c, qb + h)),
                  pl.BlockSpec((rows, HG_DK), lambda h, c: (c, fb + h)),
                  pl.BlockSpec((rows, HG_DV), lambda h, c: (c, ib + h)),
                  pl.BlockSpec((rows, HG_DV), lambda h, c: (c, gb + h)),
                  pl.BlockSpec((3, HG_DK), lambda h, c: (0, h)),
                  pl.BlockSpec((1, HG_DV), lambda h, c: (0, h)),
                  pl.BlockSpec((LANES, LANES), lambda h, c: (0, 0))],
        out_spec=pl.BlockSpec((rows, HG_DV), lambda h, c: (c, h)),
        out_shape=jax.ShapeDtypeStruct((t, HG_V_W), MXU_DTYPE),
        scratch_shapes=[pltpu.VMEM((HG_DV, HG_DK), F32)],
        name="hgrn2",
        operands=(u, u, u, u, lb_tab, norm_g.reshape(1, HG_V_W), _pair_level_table(LANES)),
        cast=cast and (*cast, lambda h, c: h * (t // rows) + c, HG_HEADS * (t // rows), HG_CAST_EVERY))


def _rope_slab(u, cc, sa, sb, half):
    return u * cc + pltpu.roll(u, LANES - half, 1) * sa + pltpu.roll(u, half, 1) * sb


def _rope_coeffs(ang, half):
    lane = lax.broadcasted_iota(jnp.int32, ang.shape, 1)
    cos, sin = jnp.cos(ang), jnp.sin(ang)
    cc = jnp.where(lane < 2 * half, cos, 1.0)
    sa = jnp.where(lane < half, -sin, 0.0)
    sb = jnp.where((lane >= half) & (lane < 2 * half), sin, 0.0)
    return cc, sa, sb


def _dsa_prep_kernel(cq_ref, k_ref, v_ref, idx_ref, pos_ref, g_ref, w_ref, fr_ref,
                     q_out, qi_out, k_out, v_out, ki_out, wi_out):
    pos = pos_ref[...].astype(F32)
    cq = cq_ref[...]
    ms = jnp.mean(cq * cq, axis=-1, keepdims=True)
    hq = cq * lax.rsqrt(ms + EPS) * g_ref[...]
    q_all = _dot(hq, w_ref[...])

    half_a = DSA_DH // ROPE_FRACTION // 2
    half_i = IDX_DH // ROPE_FRACTION // 2
    ca = _rope_coeffs(pos * fr_ref[0:1, :], half_a)
    ci = _rope_coeffs(pos * fr_ref[1:2, :], half_i)

    kk = k_ref[...]
    for h in range(DSA_HEADS):
        sl = slice(h * DSA_DH, (h + 1) * DSA_DH)
        q_out[:, sl] = (_rope_slab(q_all[:, sl], *ca, half_a) * Q_SCALE).astype(q_out.dtype)
        k_out[:, sl] = _rope_slab(kk[:, sl], *ca, half_a).astype(k_out.dtype)
    for h in range(IDX_HEADS):
        src = slice(DSA_W + h * IDX_SLOT, DSA_W + (h + 1) * IDX_SLOT)
        qi_out[h] = _rope_slab(q_all[:, src], *ci, half_i).astype(qi_out.dtype)
    v_out[...] = v_ref[...].astype(v_out.dtype)

    idx = idx_ref[...]
    lane = lax.broadcasted_iota(jnp.int32, idx.shape, 1)
    ki = _rope_slab(jnp.where(lane < IDX_DH, idx, 0.0), *ci, half_i)
    ki_out[...] = ki.astype(ki_out.dtype)
    wi = pltpu.roll(idx, LANES - IDX_DH, 1) * (IDX_HEADS ** -0.5 * IDX_DH ** -0.5)
    wi_out[...] = jnp.where(lane < IDX_HEADS, wi, 0.0)


def _dsa_prep(u, pos, q_norm_g, w_uq_pad, freqs, *, rows):
    t = u.shape[0]
    nq = DSA_W + IDX_HEADS * IDX_SLOT
    outs = (jax.ShapeDtypeStruct((t, DSA_W), MXU_DTYPE),
            jax.ShapeDtypeStruct((IDX_HEADS, t, IDX_SLOT), MXU_DTYPE),
            jax.ShapeDtypeStruct((t, DSA_W), MXU_DTYPE),
            jax.ShapeDtypeStruct((t, DSA_W), MXU_DTYPE),
            jax.ShapeDtypeStruct((t, IDX_SLOT), MXU_DTYPE),
            jax.ShapeDtypeStruct((t, LANES), F32))
    row_spec = lambda w, blk: pl.BlockSpec((rows, w), lambda i: (i, blk))
    return pl.pallas_call(
        _dsa_prep_kernel,
        grid=(t // rows,),
        in_specs=[row_spec(DSA_Q_RANK, OFF_DSA_CQ // DSA_Q_RANK),
                  row_spec(DSA_W, OFF_DSA_K // DSA_W),
                  row_spec(DSA_W, OFF_DSA_V // DSA_W),
                  row_spec(LANES, OFF_IDX // LANES),
                  pl.BlockSpec((rows, 1), lambda i: (i, 0)),
                  pl.BlockSpec((1, DSA_Q_RANK), lambda i: (0, 0)),
                  pl.BlockSpec((DSA_Q_RANK, nq), lambda i: (0, 0)),
                  pl.BlockSpec((2, LANES), lambda i: (0, 0))],
        out_specs=(row_spec(DSA_W, 0),
                   pl.BlockSpec((IDX_HEADS, rows, IDX_SLOT), lambda i: (0, i, 0)),
                   row_spec(DSA_W, 0), row_spec(DSA_W, 0), row_spec(IDX_SLOT, 0), row_spec(LANES, 0)),
        out_shape=outs,
        compiler_params=_params("parallel"),
        name="dsa_prep",
    )(u, u, u, u, pos, q_norm_g.reshape(1, DSA_Q_RANK), w_uq_pad, freqs)


IDX_TQ = 256
IDX_TK = 512
IDX_SLAB = 128


def _indexer_kernel(qi_ref, wi_ref, ki_ref, incl_ref, bias_ref, key_ref, *, topk):
    tq, tk = IDX_TQ, IDX_TK
    qb = pl.program_id(0)
    nkb = ((qb + 1) * tq + tk - 1) // tk
    q_all = qi_ref[...].reshape(IDX_HEADS * tq, IDX_SLOT)
    w = wi_ref[...]
    w_col = jnp.concatenate([w[:, h:h + 1] for h in range(IDX_HEADS)], axis=0)
    row_chunk = (qb * tq + lax.broadcasted_iota(jnp.int32, (tq, 1), 0)) // CHUNK
    col_in_blk = lax.broadcasted_iota(jnp.int32, (1, tk), 1)

    def score_body(kb, carry):
        kt = ki_ref[pl.ds(pl.multiple_of(kb * tk, tk), tk), :]
        weighted = jnp.maximum(_dot_nt(q_all, kt), 0.0) * w_col
        parts = [weighted[h * tq:(h + 1) * tq] for h in range(IDX_HEADS)]
        while len(parts) > 1:
            parts = [a + b for a, b in zip(parts[0::2], parts[1::2])]
        col_chunk = (kb * tk + col_in_blk) // CHUNK
        score = jnp.where(col_chunk <= row_chunk, parts[0], -jnp.inf)
        bits = pltpu.bitcast(score, jnp.int32)
        bits = jnp.where(bits == INT_MIN, 0, bits)
        key_ref[kb] = jnp.where(bits < 0, bits ^ 0x7FFFFFFF, bits)
        return carry

    lax.fori_loop(0, nkb, score_body, 0)

    ones_mat = jnp.ones((LANES, LANES), MXU_DTYPE)

    def count_ge(cand):
        counts = []
        for r0 in range(0, tq, IDX_SLAB):
            cand_r = cand[r0:r0 + IDX_SLAB]

            def body(kb, cnt, r0=r0, cand_r=cand_r):
                for j in range(tk // LANES):
                    cnt += jnp.where(key_ref[kb, r0:r0 + IDX_SLAB, j * LANES:(j + 1) * LANES] >= cand_r, 1, 0)
                return cnt
            counts.append(lax.fori_loop(0, nkb, body, jnp.zeros((IDX_SLAB, LANES), jnp.int32)))
        cnt = jnp.concatenate(counts, axis=0)
        return jnp.dot(cnt.astype(F32).astype(MXU_DTYPE), ones_mat, preferred_element_type=F32)

    zero = jnp.zeros((tq, LANES), jnp.int32)
    cnt0 = count_ge(zero)
    thr_rep = jnp.where(cnt0 >= topk, zero, INT_MIN)
    at_thr = jnp.where(cnt0 >= topk, cnt0, (nkb * tk).astype(F32))

    def bit_body(i, carry):
        thr_rep, at_thr = carry
        cand = thr_rep + jnp.left_shift(jnp.int32(1), 30 - i)
        cnt = count_ge(cand)
        ok = cnt >= topk
        return jnp.where(ok, cand, thr_rep), jnp.where(ok, cnt, at_thr)

    thr_rep, at_thr = lax.fori_loop(0, 31, bit_body, (thr_rep, at_thr))
    thr = thr_rep[:, 0:1]
    exact = jnp.all(at_thr == topk)

    bias_ref[...] = jnp.full(bias_ref.shape, NEG_BIG, bias_ref.dtype)

    @pl.when(exact)
    def _():
        def emit_body(kb, carry):
            key = key_ref[kb]
            take = (key >= thr) & (key > KEY_NEG_INF)
            bias_ref[:, pl.ds(pl.multiple_of(kb * tk, tk), tk)] = jnp.where(take, 0.0, NEG_BIG).astype(bias_ref.dtype)
            return carry

        lax.fori_loop(0, nkb, emit_body, 0)

    @pl.when(jnp.logical_not(exact))
    def _():
        need = topk - count_ge(thr_rep + 1)[:, 0:1]

        def emit_body(kb, seen):
            key = key_ref[kb]
            eq = key == thr
            eq_f = jnp.where(eq, 1.0, 0.0)
            rank = seen + jnp.dot(eq_f.astype(MXU_DTYPE), incl_ref[...], preferred_element_type=F32)
            take = ((key > thr) | (eq & (rank <= need))) & (key > KEY_NEG_INF)
            bias_ref[:, pl.ds(pl.multiple_of(kb * tk, tk), tk)] = jnp.where(take, 0.0, NEG_BIG).astype(bias_ref.dtype)
            return seen + jnp.sum(eq_f, axis=1, keepdims=True)

        lax.fori_loop(0, nkb, emit_body, jnp.zeros((tq, 1), F32))


def _indexer(qi, wi, ki, *, topk):
    t = ki.shape[0]
    incl = jnp.asarray(np.triu(np.ones((IDX_TK, IDX_TK), np.float32)), MXU_DTYPE)
    return pl.pallas_call(
        functools.partial(_indexer_kernel, topk=topk),
        grid=(t // IDX_TQ,),
        in_specs=[pl.BlockSpec((IDX_HEADS, IDX_TQ, IDX_SLOT), lambda i: (0, i, 0)),
                  pl.BlockSpec((IDX_TQ, LANES), lambda i: (i, 0)),
                  pl.BlockSpec((t, IDX_SLOT), lambda i: (0, 0)),
                  pl.BlockSpec((IDX_TK, IDX_TK), lambda i: (0, 0))],
        out_specs=pl.BlockSpec((IDX_TQ, t), lambda i: (i, 0)),
        out_shape=jax.ShapeDtypeStruct((t, t), BF16),
        scratch_shapes=[pltpu.VMEM((t // IDX_TK, IDX_TQ, IDX_TK), jnp.int32)],
        compiler_params=_params("parallel"),
        name="dsa_indexer",
    )(qi, wi, ki, incl)


def _attn_kernel(q_ref, k_ref, v_ref, bias_ref, o_ref, m_ref, l_ref, acc_ref, *, tq, tk):
    qb, kb = pl.program_id(0), pl.program_id(1)
    last = ((qb + 1) * tq - 1) // tk

    @pl.when(kb == 0)
    def _():
        m_ref[...] = jnp.full(m_ref.shape, NEG_BIG, F32)
        l_ref[...] = jnp.zeros_like(l_ref)
        acc_ref[...] = jnp.zeros_like(acc_ref)

    @pl.when(kb <= last)
    def _():
        bias = bias_ref[...].astype(F32)
        ones = jnp.ones((tk, LANES), v_ref.dtype)
        heads = [slice(h * DSA_DH, (h + 1) * DSA_DH) for h in range(DSA_HEADS)]
        qk = _dot_nt(q_ref[:, heads[0]], k_ref[:, heads[0]])
        for h, sl in enumerate(heads):
            s = qk + bias
            if h + 1 < DSA_HEADS:
                qk = _dot_nt(q_ref[:, heads[h + 1]], k_ref[:, heads[h + 1]])
            m_old = m_ref[h]
            m_new = jnp.maximum(m_old, jnp.max(s, axis=-1, keepdims=True))
            alpha = jnp.exp2(m_old - m_new)
            p = jnp.exp2(s - jnp.concatenate([m_new] * (tk // LANES), axis=1))
            pv = _dot(p, jnp.concatenate([v_ref[:, sl], ones], axis=1))
            l_ref[h] = alpha * l_ref[h] + pv[:, DSA_DH:]
            acc_ref[:, sl] = alpha * acc_ref[:, sl] + pv[:, :DSA_DH]
            m_ref[h] = m_new

    @pl.when(kb == last)
    def _():
        for h in range(DSA_HEADS):
            sl = slice(h * DSA_DH, (h + 1) * DSA_DH)
            o_ref[:, sl] = (acc_ref[:, sl] / l_ref[h]).astype(o_ref.dtype)


def _attention(q, k, v, bias, *, tq, tk, cast=None):
    t = q.shape[0]
    last = lambda i: ((i + 1) * tq - 1) // tk
    ratio = tk // tq
    assert tk == ratio * tq

    def before(i):
        m = i // ratio
        return i + ratio * (m * (m - 1) // 2) + (i - ratio * m) * m

    tick = lambda i, j: before(i) + jnp.minimum(j, last(i))
    return _mixer_call(
        functools.partial(_attn_kernel, tq=tq, tk=tk),
        grid=(t // tq, t // tk),
        in_specs=[pl.BlockSpec((tq, DSA_W), lambda i, j: (i, 0)),
                  pl.BlockSpec((tk, DSA_W), lambda i, j: (jnp.minimum(j, last(i)), 0)),
                  pl.BlockSpec((tk, DSA_W), lambda i, j: (jnp.minimum(j, last(i)), 0)),
                  pl.BlockSpec((tq, tk), lambda i, j: (i, jnp.minimum(j, last(i))))],
        out_spec=pl.BlockSpec((tq, DSA_W), lambda i, j: (i, 0)),
        out_shape=jax.ShapeDtypeStruct((t, DSA_W), MXU_DTYPE),
        scratch_shapes=[pltpu.VMEM((DSA_HEADS, tq, LANES), F32),
                        pltpu.VMEM((DSA_HEADS, tq, LANES), F32),
                        pltpu.VMEM((tq, DSA_W), F32)],
        name="dsa_attention",
        operands=(q, k, v, bias),
        cast=cast and (*cast, tick, before(t // tq), 1))


def _pad_heads(w, heads, width, padded):
    lead = w.shape[:-1]
    w = w.reshape(lead + (heads, width))
    w = jnp.pad(w, [(0, 0)] * len(lead) + [(0, 0), (0, padded - width)])
    return w.reshape(lead + (heads * padded,))


def _pack_plan():
    src, start = {}, 0
    names = ("ml_q", "ml_k", "ml_v", "ml_i", "ml_f", "ml_o", "dsa_cq", "dsa_k", "dsa_v", "idx_k", "idx_w",
             "hg_q", "hg_f", "hg_i", "hg_g")
    for name, width in zip(names, IN_SPLITS):
        src[name] = start
        start += width
    copies = []
    for h in range(ML_HEADS):
        for dst0, name in ((OFF_ML_Q, "ml_q"), (OFF_ML_K, "ml_k")):
            copies.append((dst0 + h * ML_DQK_PAD, src[name] + h * ML_DQK, ML_DQK))
    copies += [(OFF_DSA_K, src["dsa_k"], DSA_W), (OFF_ML_V, src["ml_v"], ML_V_W), (OFF_ML_O, src["ml_o"], ML_V_W),
               (OFF_HG_Q, src["hg_q"], HG_K_W), (OFF_HG_F, src["hg_f"], HG_K_W),
               (OFF_HG_I, src["hg_i"], HG_V_W), (OFF_HG_G, src["hg_g"], HG_V_W),
               (OFF_DSA_V, src["dsa_v"], DSA_W),
               (OFF_GATES, src["ml_i"], 2 * ML_HEADS),
               (OFF_DSA_CQ, src["dsa_cq"], DSA_Q_RANK),
               (OFF_IDX, src["idx_k"], IDX_DH + IDX_HEADS)]
    return copies


def _pack_kernel(wt_ref, o_ref):
    kt = o_ref.shape[0]
    for dst, src, width in _pack_plan():
        span = -(-width // LANES) * LANES
        seg = wt_ref[src:src + span, :].T
        if span != width:
            lane = lax.broadcasted_iota(jnp.int32, seg.shape, 1)
            seg = jnp.where(lane < width, seg, 0.0)
        o_ref[:, dst:dst + span] = seg.astype(o_ref.dtype)
    tail = OFF_IDX + LANES
    o_ref[:, tail:] = jnp.zeros((kt, D_IN_PAD - tail), o_ref.dtype)


def _pack_w_in(w_in, *, kt):
    depth, d, n = w_in.shape
    wt = jnp.swapaxes(w_in, 1, 2)
    return pl.pallas_call(
        _pack_kernel,
        grid=(depth, d // kt),
        in_specs=[pl.BlockSpec((None, n, kt), lambda l, i: (l, 0, i))],
        out_specs=pl.BlockSpec((None, kt, D_IN_PAD), lambda l, i: (l, i, 0)),
        out_shape=jax.ShapeDtypeStruct((depth, d, D_IN_PAD), MXU_DTYPE),
        compiler_params=_params("parallel", "parallel"),
        name="pack_w_in",
    )(wt)


def _rope_freqs():
    def lanes(d):
        rot = d // ROPE_FRACTION
        half = rot // 2
        inv = jnp.power(ROPE_THETA, -jnp.arange(half, dtype=F32) * (2.0 / rot))
        return jnp.concatenate([inv, inv, jnp.zeros((LANES - rot,), F32)])
    return jnp.stack([lanes(DSA_DH), lanes(IDX_DH)])


def kernel(x, positions, ln_mix_g, w_in, ml_conv_w, ml_gate_b, ml_norm_g, dsa_q_norm_g, dsa_w_uq,
           hg_lb_logits, hg_norm_g, w_out, ln_mlp_g, w_up, w_down, ln_final_g):
    bsz, t, d = x.shape
    assert bsz == 1 and t % 512 == 0 and d == D_MODEL
    depth = w_in.shape[0]
    topk = min(TOPK_MAX, t // 4)
    xs = x.reshape(t, d)
    pos = positions.reshape(t, 1)

    w_in_p = _pack_w_in(w_in, kt=256)
    conv_q = _pad_heads(ml_conv_w[..., :ML_QK_W], ML_HEADS, ML_DQK, ML_DQK_PAD)
    conv_k = _pad_heads(ml_conv_w[..., ML_QK_W:], ML_HEADS, ML_DQK, ML_DQK_PAD)
    conv_p = jnp.concatenate([conv_q.reshape(depth, ML_CONV, ML_HEADS, ML_DQK_PAD),
                              conv_k.reshape(depth, ML_CONV, ML_HEADS, ML_DQK_PAD)], axis=-1)
    conv_p = jnp.transpose(conv_p, (0, 2, 1, 3))
    gate_b = jnp.pad(ml_gate_b.reshape(depth, 1, 2 * ML_HEADS), ((0, 0), (0, 0), (0, LANES - 2 * ML_HEADS)))
    w_uq_p = jnp.concatenate([dsa_w_uq[..., :DSA_W],
                              _pad_heads(dsa_w_uq[..., DSA_W:], IDX_HEADS, IDX_DH, IDX_SLOT)],
                             axis=-1).astype(MXU_DTYPE)
    freqs = _rope_freqs()
    lb_cum = jnp.cumsum(jax.nn.softmax(hg_lb_logits.astype(F32), axis=0), axis=0)
    lb = lb_cum - lb_cum[:1]
    lb_tab = jnp.stack([jnp.log(lb), jnp.log1p(-lb), 1.0 - lb], axis=1)

    for layer in range(depth):
        u = _norm_matmul(xs, ln_mix_g[layer], w_in_p, layer, tm=512, tn=1024, act=False, out_dtype=F32)
        y_a, w_out_b = _mlstm(u, conv_p[layer], gate_b[layer], ml_norm_g[layer], rows=256, cast=(w_out, layer))
        q_r, qi_r, k_r, v_b, ki_r, wi = _dsa_prep(u, pos, dsa_q_norm_g[layer], w_uq_p[layer], freqs, rows=256)
        bias = _indexer(qi_r, wi, ki_r, topk=topk)
        y_b, w_down_b = _attention(q_r, k_r, v_b, bias, tq=512, tk=1024, cast=(w_down, layer))
        y_c, w_up_b = _hgrn2(u, lb_tab[layer], hg_norm_g[layer], rows=256, cast=(w_up, layer))
        xs = _mix_out(y_a, y_b, y_c, w_out_b[None], 0, xs, tm=512, tn=1024)
        a = _norm_matmul(xs, ln_mlp_g[layer], w_up_b[None], 0, tm=512, tn=1024, act=True, out_dtype=MXU_DTYPE)
        xs = _matmul_res(a, w_down_b[None], 0, xs, tm=1024, tn=1024, tk=2048)
    return _final_norm(xs, ln_final_g, tm=256).reshape(bsz, t, d)
```

```python
import functools

import jax
import jax.numpy as jnp
import numpy as np
from jax import lax
from jax.experimental import pallas as pl
from jax.experimental.pallas import tpu as pltpu

F32 = jnp.float32
BF16 = jnp.bfloat16
MXU_DTYPE = jnp.bfloat16

D_MODEL = 4096
DEPTH = 4
CHUNK = 64
EPS = 1e-6
ROPE_THETA = 500000.0
ROPE_FRACTION = 4
ML_HEADS = 4
ML_DV = 384
ML_DQK = ML_DV // 2
ML_DQK_PAD = 256
ML_CONV = 4
ML_GATE_CAP = 15.0
DSA_HEADS = 8
DSA_DH = 128
DSA_Q_RANK = 384
IDX_HEADS = 8
IDX_DH = 64
IDX_SLOT = 128
TOPK_MAX = 256
HG_HEADS = 12
HG_DK = 128
HG_DV = 128
D_FF = 4 * D_MODEL

ML_QK_W = ML_HEADS * ML_DQK
ML_V_W = ML_HEADS * ML_DV
DSA_W = DSA_HEADS * DSA_DH
IDX_W = IDX_HEADS * IDX_DH
HG_K_W = HG_HEADS * HG_DK
HG_V_W = HG_HEADS * HG_DV
D_MIX = ML_V_W + DSA_W + HG_V_W
IN_SPLITS = (ML_QK_W, ML_QK_W, ML_V_W, ML_HEADS, ML_HEADS, ML_V_W,
             DSA_Q_RANK, DSA_W, DSA_W, IDX_DH, IDX_HEADS,
             HG_K_W, HG_K_W, HG_V_W, HG_V_W)

LANES = 128
VMEM_LIMIT = 56 * 1024 * 1024

OFF_ML_Q = 0
OFF_ML_K = 1024
OFF_DSA_K = 2048
OFF_ML_V = 3072
OFF_ML_O = 4608
OFF_HG_Q = 6144
OFF_HG_F = 7680
OFF_HG_I = 9216
OFF_HG_G = 10752
OFF_DSA_V = 12288
OFF_GATES = 13312
OFF_DSA_CQ = 13440
OFF_IDX = 13824
D_IN_PAD = 14336

LOG2E = 1.4426950408889634
Q_SCALE = DSA_DH ** -0.5 * LOG2E

INT_MIN = -2 ** 31
KEY_NEG_INF = -2139095041
NEG_BIG = -1e30


def _mxu(a):
    return a.astype(MXU_DTYPE)


def _dot(a, b):
    return jnp.dot(_mxu(a), _mxu(b), preferred_element_type=F32)


def _dot_nt(a, b):
    return lax.dot_general(_mxu(a), _mxu(b), (((1,), (1,)), ((), ())), preferred_element_type=F32)


def _dot_tn(a, b):
    return lax.dot_general(_mxu(a), _mxu(b), (((0,), (0,)), ((), ())), preferred_element_type=F32)


def _dot_f32(a, b):
    return jnp.dot(a, b, precision=lax.Precision.HIGHEST, preferred_element_type=F32)


def _sigmoid(x):
    return 1.0 / (1.0 + jnp.exp(-x))


def _log_sigmoid(x):
    return jnp.minimum(x, 0.0) - jnp.log1p(jnp.exp(-jnp.abs(x)))


def _params(*sem):
    return pltpu.CompilerParams(dimension_semantics=sem, vmem_limit_bytes=VMEM_LIMIT)


BF16_SUBLANES = 16


def _mixer_call(kernel_fn, *, grid, in_specs, out_spec, out_shape, scratch_shapes, name, operands, cast=None):
    params = _params("parallel", "arbitrary")
    if cast is None:
        return pl.pallas_call(kernel_fn, grid=grid, in_specs=in_specs, out_specs=out_spec, out_shape=out_shape,
                              scratch_shapes=scratch_shapes, compiler_params=params, name=name)(*operands)
    w, layer, tick, ticks, every = cast
    params = _params("arbitrary", "arbitrary")
    rows, cols = w.shape[1:]
    blocks = ticks // every
    while rows % blocks or (rows // blocks) % BF16_SUBLANES:
        blocks -= 1
    block = lambda a, b: jnp.minimum(tick(a, b) // every, blocks - 1)
    n_in = len(in_specs)

    def body(*refs):
        w_ref, w_out = refs[n_in], refs[n_in + 2]
        now = tick(pl.program_id(0), pl.program_id(1))

        @pl.when((now % every == 0) & (now // every < blocks))
        def _():
            w_out[...] = w_ref[...].astype(w_out.dtype)

        kernel_fn(*refs[:n_in], refs[n_in + 1], *refs[n_in + 3:])

    return pl.pallas_call(
        body, grid=grid,
        in_specs=in_specs + [pl.BlockSpec((None, rows // blocks, cols), lambda a, b: (layer, block(a, b), 0))],
        out_specs=(out_spec, pl.BlockSpec((rows // blocks, cols), lambda a, b: (block(a, b), 0))),
        out_shape=(out_shape, jax.ShapeDtypeStruct((rows, cols), MXU_DTYPE)),
        scratch_shapes=scratch_shapes, compiler_params=params, name=name)(*operands, w)


def _norm_matmul_kernel(x_ref, g_ref, w_ref, o_ref, h_ref, *, act):
    @pl.when(pl.program_id(1) == 0)
    def _():
        x = x_ref[...]
        ms = jnp.mean(x * x, axis=-1, keepdims=True)
        h_ref[...] = (x * lax.rsqrt(ms + EPS) * g_ref[...]).astype(h_ref.dtype)

    y = jnp.dot(h_ref[...], w_ref[...], preferred_element_type=F32)
    if act:
        y = jnp.square(jnp.maximum(y, 0.0))
    o_ref[...] = y.astype(o_ref.dtype)


def _norm_matmul(x, g, w, layer, *, tm, tn, act, out_dtype):
    m, k = x.shape
    n = w.shape[2]
    return pl.pallas_call(
        functools.partial(_norm_matmul_kernel, act=act),
        grid=(m // tm, n // tn),
        in_specs=[pl.BlockSpec((tm, k), lambda i, j: (i, 0)),
                  pl.BlockSpec((1, k), lambda i, j: (0, 0)),
                  pl.BlockSpec((None, k, tn), lambda i, j: (layer, 0, j))],
        out_specs=pl.BlockSpec((tm, tn), lambda i, j: (i, j)),
        out_shape=jax.ShapeDtypeStruct((m, n), out_dtype),
        scratch_shapes=[pltpu.VMEM((tm, k), MXU_DTYPE)],
        compiler_params=_params("parallel", "arbitrary"),
        name="norm_matmul",
    )(x, g.reshape(1, k), w)


def _matmul_res_kernel(a_ref, w_ref, r_ref, o_ref):
    part = jnp.dot(a_ref[...], w_ref[...], preferred_element_type=F32)

    @pl.when(pl.program_id(2) == 0)
    def _():
        o_ref[...] = r_ref[...] + part

    @pl.when(pl.program_id(2) != 0)
    def _():
        o_ref[...] += part


def _matmul_res(a, w, layer, res, *, tm, tn, tk):
    m, k = a.shape
    n = w.shape[2]
    return pl.pallas_call(
        _matmul_res_kernel,
        grid=(m // tm, n // tn, k // tk),
        in_specs=[pl.BlockSpec((tm, tk), lambda i, j, kk: (i, kk)),
                  pl.BlockSpec((None, tk, tn), lambda i, j, kk: (layer, kk, j)),
                  pl.BlockSpec((tm, tn), lambda i, j, kk: (i, j))],
        out_specs=pl.BlockSpec((tm, tn), lambda i, j, kk: (i, j)),
        out_shape=jax.ShapeDtypeStruct((m, n), F32),
        compiler_params=_params("parallel", "parallel", "arbitrary"),
        name="matmul_res",
    )(a, w, res)


def _mix_out_kernel(ya_ref, yb_ref, yc_ref, w_ref, r_ref, o_ref):
    acc = r_ref[...]
    acc += jnp.dot(ya_ref[...], w_ref[0:ML_V_W, :], preferred_element_type=F32)
    acc += jnp.dot(yb_ref[...], w_ref[ML_V_W:ML_V_W + DSA_W, :], preferred_element_type=F32)
    acc += jnp.dot(yc_ref[...], w_ref[ML_V_W + DSA_W:D_MIX, :], preferred_element_type=F32)
    o_ref[...] = acc


def _mix_out(ya, yb, yc, w, layer, res, *, tm, tn):
    m = ya.shape[0]
    n = w.shape[2]
    return pl.pallas_call(
        _mix_out_kernel,
        grid=(m // tm, n // tn),
        in_specs=[pl.BlockSpec((tm, ML_V_W), lambda i, j: (i, 0)),
                  pl.BlockSpec((tm, DSA_W), lambda i, j: (i, 0)),
                  pl.BlockSpec((tm, HG_V_W), lambda i, j: (i, 0)),
                  pl.BlockSpec((None, D_MIX, tn), lambda i, j: (layer, 0, j)),
                  pl.BlockSpec((tm, tn), lambda i, j: (i, j))],
        out_specs=pl.BlockSpec((tm, tn), lambda i, j: (i, j)),
        out_shape=jax.ShapeDtypeStruct((m, n), F32),
        compiler_params=_params("parallel", "parallel"),
        name="mix_out",
    )(ya, yb, yc, w, res)


def _final_norm_kernel(x_ref, g_ref, o_ref):
    x = x_ref[...]
    ms = jnp.mean(x * x, axis=-1, keepdims=True)
    o_ref[...] = x * lax.rsqrt(ms + EPS) * g_ref[...]


def _final_norm(x, g, *, tm):
    m, k = x.shape
    return pl.pallas_call(
        _final_norm_kernel,
        grid=(m // tm,),
        in_specs=[pl.BlockSpec((tm, k), lambda i: (i, 0)),
                  pl.BlockSpec((1, k), lambda i: (0, 0))],
        out_specs=pl.BlockSpec((tm, k), lambda i: (i, 0)),
        out_shape=jax.ShapeDtypeStruct((m, k), F32),
        compiler_params=_params("parallel"),
        name="final_norm",
    )(x, g.reshape(1, k))


def _mlstm_kernel(q_ref, k_ref, v_ref, o_ref, gt_ref, cw_ref, gb_ref, ng_ref, out_ref,
                  xbuf, c_ref, *, rows):
    head = pl.program_id(0)

    @pl.when(pl.program_id(1) == 0)
    def _():
        xbuf[0:8, :] = jnp.zeros((8, 2 * ML_DQK_PAD), F32)
        c_ref[...] = jnp.zeros_like(c_ref)

    xbuf[8:8 + rows, 0:ML_DQK_PAD] = q_ref[...]
    xbuf[8:8 + rows, ML_DQK_PAD:] = k_ref[...]
    cw = cw_ref[...]
    acc = xbuf[8:8 + rows, :] * cw[ML_CONV - 1:ML_CONV, :]
    for j in range(1, ML_CONV):
        acc += xbuf[8 - j:8 - j + rows, :] * cw[ML_CONV - 1 - j:ML_CONV - j, :]
    xbuf[0:8, :] = xbuf[rows:rows + 8, :]
    qk = acc * _sigmoid(acc)
    q = qk[:, 0:ML_DQK_PAD]
    k = qk[:, ML_DQK_PAD:] * (ML_DQK ** -0.5)

    capped = ML_GATE_CAP * jnp.tanh((gt_ref[...] + gb_ref[...]) * (1.0 / ML_GATE_CAP))
    lsig = _log_sigmoid(capped)
    r_i = lax.broadcasted_iota(jnp.int32, (rows, rows), 0)
    c_i = lax.broadcasted_iota(jnp.int32, (rows, rows), 1)
    causal = c_i <= r_i
    b_all = _dot_f32(jnp.where(causal, 1.0, 0.0), lsig)
    lane = lax.broadcasted_iota(jnp.int32, (rows, LANES), 1)
    ig_col = jnp.sum(jnp.where(lane == head, capped, 0.0), axis=1, keepdims=True)
    b_col = jnp.sum(jnp.where(lane == head + ML_HEADS, b_all, 0.0), axis=1, keepdims=True)
    sub = lax.broadcasted_iota(jnp.int32, (LANES, rows), 0)
    ig_row = jnp.sum(jnp.where(sub == head, capped.T, 0.0), axis=0, keepdims=True)
    b_row = jnp.sum(jnp.where(sub == head + ML_HEADS, b_all.T, 0.0), axis=0, keepdims=True)

    dmat = jnp.exp(jnp.where(causal, b_col - b_row + ig_row, -jnp.inf))
    s = _dot_nt(q, k) * dmat
    one_col = jnp.where(lane == 0, 1.0, 0.0)
    v_ext = jnp.concatenate([v_ref[...], one_col], axis=1)
    c_old = c_ref[...]
    num_ext = _dot(s, v_ext) + jnp.exp(b_col) * _dot(q, c_old)
    num = num_ext[:, 0:ML_DV]
    den = num_ext[:, ML_DV:ML_DV + 1]
    hh = num / jnp.maximum(jnp.abs(den), 1.0)

    b_last = b_col[rows - 1:rows, :]
    w_s = jnp.exp(b_last - b_col + ig_col)
    c_ref[...] = jnp.exp(b_last) * c_old + _dot_tn(k, w_s * v_ext)

    ms = jnp.mean(hh * hh, axis=-1, keepdims=True)
    y = hh * lax.rsqrt(ms + EPS) * ng_ref[...]
    out_ref[...] = (_sigmoid(o_ref[...]) * y).astype(out_ref.dtype)


def _mlstm(u, conv_w, gate_b, norm_g, *, rows, cast=None):
    t = u.shape[0]
    qb, kb = OFF_ML_Q // ML_DQK_PAD, OFF_ML_K // ML_DQK_PAD
    vb, ob = OFF_ML_V // ML_DV, OFF_ML_O // ML_DV
    gb = OFF_GATES // LANES
    return _mixer_call(
        functools.partial(_mlstm_kernel, rows=rows),
        grid=(ML_HEADS, t // rows),
        in_specs=[pl.BlockSpec((rows, ML_DQK_PAD), lambda h, c: (c, qb + h)),
                  pl.BlockSpec((rows, ML_DQK_PAD), lambda h, c: (c, kb + h)),
                  pl.BlockSpec((rows, ML_DV), lambda h, c: (c, vb + h)),
                  pl.BlockSpec((rows, ML_DV), lambda h, c: (c, ob + h)),
                  pl.BlockSpec((rows, LANES), lambda h, c: (c, gb)),
                  pl.BlockSpec((None, ML_CONV, 2 * ML_DQK_PAD), lambda h, c: (h, 0, 0)),
                  pl.BlockSpec((1, LANES), lambda h, c: (0, 0)),
                  pl.BlockSpec((1, ML_DV), lambda h, c: (0, h))],
        out_spec=pl.BlockSpec((rows, ML_DV), lambda h, c: (c, h)),
        out_shape=jax.ShapeDtypeStruct((t, ML_V_W), MXU_DTYPE),
        scratch_shapes=[pltpu.VMEM((rows + 8, 2 * ML_DQK_PAD), F32),
                        pltpu.VMEM((ML_DQK_PAD, ML_DV + LANES), F32)],
        name="mlstm",
        operands=(u, u, u, u, u, conv_w, gate_b, norm_g.reshape(1, ML_V_W)),
        cast=cast and (*cast, lambda h, c: h * (t // rows) + c, ML_HEADS * (t // rows), 1))


SUBLANES = 8
HG_CAST_EVERY = 1


def _pair_level_table(rows):
    t = np.arange(rows)[:, None]
    s = np.arange(rows)[None, :]
    x = np.bitwise_xor(t, s)
    lvl = np.floor(np.log2(np.maximum(x, 1))).astype(np.int32)
    return jnp.asarray(np.where(t > s, lvl, -1).astype(np.int32))


def _block_sums(log_f, rows):
    sub = lax.broadcasted_iota(jnp.int32, log_f.shape, 0)
    groups = rows // SUBLANES

    def row_of_group(x, j):
        x3 = x.reshape(groups, SUBLANES, x.shape[-1])
        return jnp.broadcast_to(x3[:, j:j + 1, :], x3.shape).reshape(x.shape)

    c = log_f
    e = jnp.zeros_like(log_f)
    out = [(c, e)]
    odd = (sub & 1) == 1
    c, e = (c + jnp.where(odd, pltpu.roll(c, 1, 0), 0.0),
            e + jnp.where(odd, 0.0, pltpu.roll(c, rows - 1, 0)))
    out.append((c, e))
    r8 = sub & 7
    c, e = (c + jnp.where((r8 == 2) | (r8 == 3), row_of_group(c, 1),
                          jnp.where((r8 == 6) | (r8 == 7), row_of_group(c, 5), 0.0)),
            e + jnp.where((r8 == 0) | (r8 == 1), row_of_group(c, 3),
                          jnp.where((r8 == 4) | (r8 == 5), row_of_group(c, 7), 0.0)))
    out.append((c, e))
    c, e = (c + jnp.where(r8 >= 4, row_of_group(c, 3), 0.0),
            e + jnp.where(r8 < 4, row_of_group(c, 7), 0.0))
    out.append((c, e))
    m = SUBLANES
    while m < rows:
        cs, es = [], []
        for p in range(rows // (2 * m)):
            lo, mid, hi = 2 * m * p, 2 * m * p + m, 2 * m * (p + 1)
            cs += [c[lo:mid], c[mid:hi] + c[mid - 1:mid]]
            es += [e[lo:mid] + c[hi - 1:hi], e[mid:hi]]
        c, e = jnp.concatenate(cs, axis=0), jnp.concatenate(es, axis=0)
        out.append((c, e))
        m *= 2
    return out


def _hgrn2_kernel(q_ref, f_ref, i_ref, g_ref, lb_ref, ng_ref, lvl_ref, out_ref, st_ref, *, rows):
    @pl.when(pl.program_id(1) == 0)
    def _():
        st_ref[...] = jnp.zeros_like(st_ref)

    qp = q_ref[...]
    q = qp * _sigmoid(qp)
    fp = f_ref[...]
    v = i_ref[...]
    gp = g_ref[...]
    log_lb = lb_ref[0:1, :]
    x2 = lb_ref[1:2, :] + _log_sigmoid(fp)
    log_f = jnp.maximum(log_lb, x2) + jnp.log1p(jnp.exp(-jnp.abs(log_lb - x2)))
    k = lb_ref[2:3, :] * _sigmoid(-fp)

    sums = _block_sums(log_f * LOG2E, rows)
    lvl = lvl_ref[...]
    tiles = rows // LANES
    tile_rows = [slice(r * LANES, (r + 1) * LANES) for r in range(tiles)]
    diag = [jnp.zeros((LANES, LANES), F32) for _ in range(tiles)]
    o_tiles = [None] * tiles
    for level, (c_m, e_m) in enumerate(sums[:-1]):
        m = 1 << level
        qh = q * jnp.exp2(c_m)
        kh = k * jnp.exp2(e_m)
        if m < LANES:
            for r, sl in enumerate(tile_rows):
                diag[r] = jnp.where(lvl == level, _dot_nt(qh[sl], kh[sl]), diag[r])
        else:
            for p in range(rows // (2 * m)):
                lo, mid = 2 * m * p, 2 * m * p + m
                for r in range(mid // LANES, (mid + m) // LANES):
                    part = _dot(_dot_nt(qh[tile_rows[r]], kh[lo:mid]), v[lo:mid])
                    o_tiles[r] = part if o_tiles[r] is None else o_tiles[r] + part
    for r, sl in enumerate(tile_rows):
        part = _dot(diag[r], v[sl])
        o_tiles[r] = part if o_tiles[r] is None else o_tiles[r] + part
    b, after = sums[-1]

    st = st_ref[...]
    o = (jnp.concatenate(o_tiles, axis=0) + jnp.sum(q * k, axis=-1, keepdims=True) * v
         + _dot_nt(q * jnp.exp2(b), st))
    st_ref[...] = st * jnp.exp2(b[rows - 1:rows, :]) + _dot_tn(v, k * jnp.exp2(after))

    ms = jnp.mean(o * o, axis=-1, keepdims=True)
    y = o * lax.rsqrt(ms + EPS) * ng_ref[...]
    out_ref[...] = (y * (gp * _sigmoid(gp))).astype(out_ref.dtype)


def _hgrn2(u, lb_tab, norm_g, *, rows, cast=None):
    t = u.shape[0]
    qb, fb = OFF_HG_Q // HG_DK, OFF_HG_F // HG_DK
    ib, gb = OFF_HG_I // HG_DV, OFF_HG_G // HG_DV
    return _mixer_call(
        functools.partial(_hgrn2_kernel, rows=rows),
        grid=(HG_HEADS, t // rows),
        in_specs=[pl.BlockSpec((rows, HG_DK), lambda h, c: (c, qb + h)),
                  pl.BlockSpec((rows, HG_DK), lambda h, c: (c, fb + h)),
                  pl.BlockSpec((rows, HG_DV), lambda h, c: (c, ib + h)),
                  pl.BlockSpec((rows, HG_DV), lambda h, c: (c, gb + h)),
                  pl.BlockSpec((3, HG_DK), lambda h, c: (0, h)),
                  pl.BlockSpec((1, HG_DV), lambda h, c: (0, h)),
                  pl.BlockSpec((LANES, LANES), lambda h, c: (0, 0))],
        out_spec=pl.BlockSpec((rows, HG_DV), lambda h, c: (c, h)),
        out_shape=jax.ShapeDtypeStruct((t, HG_V_W), MXU_DTYPE),
        scratch_shapes=[pltpu.VMEM((HG_DV, HG_DK), F32)],
        name="hgrn2",
        operands=(u, u, u, u, lb_tab, norm_g.reshape(1, HG_V_W), _pair_level_table(LANES)),
        cast=cast and (*cast, lambda h, c: h * (t // rows) + c, HG_HEADS * (t // rows), HG_CAST_EVERY))


def _rope_slab(u, cc, sa, sb, half):
    return u * cc + pltpu.roll(u, LANES - half, 1) * sa + pltpu.roll(u, half, 1) * sb


def _rope_coeffs(ang, half):
    lane = lax.broadcasted_iota(jnp.int32, ang.shape, 1)
    cos, sin = jnp.cos(ang), jnp.sin(ang)
    cc = jnp.where(lane < 2 * half, cos, 1.0)
    sa = jnp.where(lane < half, -sin, 0.0)
    sb = jnp.where((lane >= half) & (lane < 2 * half), sin, 0.0)
    return cc, sa, sb


def _dsa_prep_kernel(cq_ref, k_ref, v_ref, idx_ref, pos_ref, g_ref, w_ref, fr_ref,
                     q_out, qi_out, k_out, v_out, ki_out, wi_out):
    pos = pos_ref[...].astype(F32)
    cq = cq_ref[...]
    ms = jnp.mean(cq * cq, axis=-1, keepdims=True)
    hq = cq * lax.rsqrt(ms + EPS) * g_ref[...]
    q_all = _dot(hq, w_ref[...])

    half_a = DSA_DH // ROPE_FRACTION // 2
    half_i = IDX_DH // ROPE_FRACTION // 2
    ca = _rope_coeffs(pos * fr_ref[0:1, :], half_a)
    ci = _rope_coeffs(pos * fr_ref[1:2, :], half_i)

    kk = k_ref[...]
    for h in range(DSA_HEADS):
        sl = slice(h * DSA_DH, (h + 1) * DSA_DH)
        q_out[:, sl] = (_rope_slab(q_all[:, sl], *ca, half_a) * Q_SCALE).astype(q_out.dtype)
        k_out[:, sl] = _rope_slab(kk[:, sl], *ca, half_a).astype(k_out.dtype)
    for h in range(IDX_HEADS):
        src = slice(DSA_W + h * IDX_SLOT, DSA_W + (h + 1) * IDX_SLOT)
        qi_out[h] = _rope_slab(q_all[:, src], *ci, half_i).astype(qi_out.dtype)
    v_out[...] = v_ref[...].astype(v_out.dtype)

    idx = idx_ref[...]
    lane = lax.broadcasted_iota(jnp.int32, idx.shape, 1)
    ki = _rope_slab(jnp.where(lane < IDX_DH, idx, 0.0), *ci, half_i)
    ki_out[...] = ki.astype(ki_out.dtype)
    wi = pltpu.roll(idx, LANES - IDX_DH, 1) * (IDX_HEADS ** -0.5 * IDX_DH ** -0.5)
    wi_out[...] = jnp.where(lane < IDX_HEADS, wi, 0.0)


def _dsa_prep(u, pos, q_norm_g, w_uq_pad, freqs, *, rows):
    t = u.shape[0]
    nq = DSA_W + IDX_HEADS * IDX_SLOT
    outs = (jax.ShapeDtypeStruct((t, DSA_W), MXU_DTYPE),
            jax.ShapeDtypeStruct((IDX_HEADS, t, IDX_SLOT), MXU_DTYPE),
            jax.ShapeDtypeStruct((t, DSA_W), MXU_DTYPE),
            jax.ShapeDtypeStruct((t, DSA_W), MXU_DTYPE),
            jax.ShapeDtypeStruct((t, IDX_SLOT), MXU_DTYPE),
            jax.ShapeDtypeStruct((t, LANES), F32))
    row_spec = lambda w, blk: pl.BlockSpec((rows, w), lambda i: (i, blk))
    return pl.pallas_call(
        _dsa_prep_kernel,
        grid=(t // rows,),
        in_specs=[row_spec(DSA_Q_RANK, OFF_DSA_CQ // DSA_Q_RANK),
                  row_spec(DSA_W, OFF_DSA_K // DSA_W),
                  row_spec(DSA_W, OFF_DSA_V // DSA_W),
                  row_spec(LANES, OFF_IDX // LANES),
                  pl.BlockSpec((rows, 1), lambda i: (i, 0)),
                  pl.BlockSpec((1, DSA_Q_RANK), lambda i: (0, 0)),
                  pl.BlockSpec((DSA_Q_RANK, nq), lambda i: (0, 0)),
                  pl.BlockSpec((2, LANES), lambda i: (0, 0))],
        out_specs=(row_spec(DSA_W, 0),
                   pl.BlockSpec((IDX_HEADS, rows, IDX_SLOT), lambda i: (0, i, 0)),
                   row_spec(DSA_W, 0), row_spec(DSA_W, 0), row_spec(IDX_SLOT, 0), row_spec(LANES, 0)),
        out_shape=outs,
        compiler_params=_params("parallel"),
        name="dsa_prep",
    )(u, u, u, u, pos, q_norm_g.reshape(1, DSA_Q_RANK), w_uq_pad, freqs)


IDX_TQ = 512
IDX_TK = 512
IDX_SLAB = 128


def _indexer_kernel(qi_ref, wi_ref, ki_ref, incl_ref, bias_ref, key_ref, *, topk):
    tq, tk = IDX_TQ, IDX_TK
    qb = pl.program_id(0)
    nkb = ((qb + 1) * tq + tk - 1) // tk
    q_all = qi_ref[...].reshape(IDX_HEADS * tq, IDX_SLOT)
    w = wi_ref[...]
    w_col = jnp.concatenate([w[:, h:h + 1] for h in range(IDX_HEADS)], axis=0)
    row_chunk = (qb * tq + lax.broadcasted_iota(jnp.int32, (tq, 1), 0)) // CHUNK
    col_in_blk = lax.broadcasted_iota(jnp.int32, (1, tk), 1)

    def score_body(kb, carry):
        kt = ki_ref[pl.ds(pl.multiple_of(kb * tk, tk), tk), :]
        weighted = jnp.maximum(_dot_nt(q_all, kt), 0.0) * w_col
        parts = [weighted[h * tq:(h + 1) * tq] for h in range(IDX_HEADS)]
        while len(parts) > 1:
            parts = [a + b for a, b in zip(parts[0::2], parts[1::2])]
        col_chunk = (kb * tk + col_in_blk) // CHUNK
        score = jnp.where(col_chunk <= row_chunk, parts[0], -jnp.inf)
        bits = pltpu.bitcast(score, jnp.int32)
        bits = jnp.where(bits == INT_MIN, 0, bits)
        key_ref[kb] = jnp.where(bits < 0, bits ^ 0x7FFFFFFF, bits)
        return carry

    lax.fori_loop(0, nkb, score_body, 0)

    ones_mat = jnp.ones((LANES, LANES), MXU_DTYPE)

    def count_ge(cand):
        counts = []
        for r0 in range(0, tq, IDX_SLAB):
            cand_r = cand[r0:r0 + IDX_SLAB]

            def body(kb, cnt, r0=r0, cand_r=cand_r):
                for j in range(tk // LANES):
                    cnt += jnp.where(key_ref[kb, r0:r0 + IDX_SLAB, j * LANES:(j + 1) * LANES] >= cand_r, 1, 0)
                return cnt
            counts.append(lax.fori_loop(0, nkb, body, jnp.zeros((IDX_SLAB, LANES), jnp.int32)))
        cnt = jnp.concatenate(counts, axis=0)
        return jnp.dot(cnt.astype(F32).astype(MXU_DTYPE), ones_mat, preferred_element_type=F32)

    zero = jnp.zeros((tq, LANES), jnp.int32)
    cnt0 = count_ge(zero)
    thr_rep = jnp.where(cnt0 >= topk, zero, INT_MIN)
    at_thr = jnp.where(cnt0 >= topk, cnt0, (nkb * tk).astype(F32))

    def bit_body(i, carry):
        thr_rep, at_thr = carry
        cand = thr_rep + jnp.left_shift(jnp.int32(1), 30 - i)
        cnt = count_ge(cand)
        ok = cnt >= topk
        return jnp.where(ok, cand, thr_rep), jnp.where(ok, cnt, at_thr)

    thr_rep, at_thr = lax.fori_loop(0, 31, bit_body, (thr_rep, at_thr))
    thr = thr_rep[:, 0:1]
    exact = jnp.all(at_thr == topk)

    bias_ref[...] = jnp.full(bias_ref.shape, NEG_BIG, bias_ref.dtype)

    @pl.when(exact)
    def _():
        def emit_body(kb, carry):
            key = key_ref[kb]
            take = (key >= thr) & (key > KEY_NEG_INF)
            bias_ref[:, pl.ds(pl.multiple_of(kb * tk, tk), tk)] = jnp.where(take, 0.0, NEG_BIG).astype(bias_ref.dtype)
            return carry

        lax.fori_loop(0, nkb, emit_body, 0)

    @pl.when(jnp.logical_not(exact))
    def _():
        need = topk - count_ge(thr_rep + 1)[:, 0:1]

        def emit_body(kb, seen):
            key = key_ref[kb]
            eq = key == thr
            eq_f = jnp.where(eq, 1.0, 0.0)
            rank = seen + jnp.dot(eq_f.astype(MXU_DTYPE), incl_ref[...], preferred_element_type=F32)
            take = ((key > thr) | (eq & (rank <= need))) & (key > KEY_NEG_INF)
            bias_ref[:, pl.ds(pl.multiple_of(kb * tk, tk), tk)] = jnp.where(take, 0.0, NEG_BIG).astype(bias_ref.dtype)
            return seen + jnp.sum(eq_f, axis=1, keepdims=True)

        lax.fori_loop(0, nkb, emit_body, jnp.zeros((tq, 1), F32))


def _indexer(qi, wi, ki, *, topk):
    t = ki.shape[0]
    incl = jnp.asarray(np.triu(np.ones((IDX_TK, IDX_TK), np.float32)), MXU_DTYPE)
    return pl.pallas_call(
        functools.partial(_indexer_kernel, topk=topk),
        grid=(t // IDX_TQ,),
        in_specs=[pl.BlockSpec((IDX_HEADS, IDX_TQ, IDX_SLOT), lambda i: (0, i, 0)),
                  pl.BlockSpec((IDX_TQ, LANES), lambda i: (i, 0)),
                  pl.BlockSpec((t, IDX_SLOT), lambda i: (0, 0)),
                  pl.BlockSpec((IDX_TK, IDX_TK), lambda i: (0, 0))],
        out_specs=pl.BlockSpec((IDX_TQ, t), lambda i: (i, 0)),
        out_shape=jax.ShapeDtypeStruct((t, t), BF16),
        scratch_shapes=[pltpu.VMEM((t // IDX_TK, IDX_TQ, IDX_TK), jnp.int32)],
        compiler_params=_params("parallel"),
        name="dsa_indexer",
    )(qi, wi, ki, incl)


def _attn_kernel(q_ref, k_ref, v_ref, bias_ref, o_ref, m_ref, l_ref, acc_ref, *, tq, tk):
    qb, kb = pl.program_id(0), pl.program_id(1)
    last = ((qb + 1) * tq - 1) // tk

    @pl.when(kb == 0)
    def _():
        m_ref[...] = jnp.full(m_ref.shape, NEG_BIG, F32)
        l_ref[...] = jnp.zeros_like(l_ref)
        acc_ref[...] = jnp.zeros_like(acc_ref)

    @pl.when(kb <= last)
    def _():
        bias = bias_ref[...].astype(F32)
        ones = jnp.ones((tk, LANES), v_ref.dtype)
        heads = [slice(h * DSA_DH, (h + 1) * DSA_DH) for h in range(DSA_HEADS)]
        qk = _dot_nt(q_ref[:, heads[0]], k_ref[:, heads[0]])
        for h, sl in enumerate(heads):
            s = qk + bias
            if h + 1 < DSA_HEADS:
                qk = _dot_nt(q_ref[:, heads[h + 1]], k_ref[:, heads[h + 1]])
            m_old = m_ref[h]
            m_new = jnp.maximum(m_old, jnp.max(s, axis=-1, keepdims=True))
            alpha = jnp.exp2(m_old - m_new)
            p = jnp.exp2(s - jnp.concatenate([m_new] * (tk // LANES), axis=1))
            pv = _dot(p, jnp.concatenate([v_ref[:, sl], ones], axis=1))
            l_ref[h] = alpha * l_ref[h] + pv[:, DSA_DH:]
            acc_ref[:, sl] = alpha * acc_ref[:, sl] + pv[:, :DSA_DH]
            m_ref[h] = m_new

    @pl.when(kb == last)
    def _():
        for h in range(DSA_HEADS):
            sl = slice(h * DSA_DH, (h + 1) * DSA_DH)
            o_ref[:, sl] = (acc_ref[:, sl] / l_ref[h]).astype(o_ref.dtype)


def _attention(q, k, v, bias, *, tq, tk, cast=None):
    t = q.shape[0]
    last = lambda i: ((i + 1) * tq - 1) // tk
    ratio = tk // tq
    assert tk == ratio * tq

    def before(i):
        m = i // ratio
        return i + ratio * (m * (m - 1) // 2) + (i - ratio * m) * m

    tick = lambda i, j: before(i) + jnp.minimum(j, last(i))
    return _mixer_call(
        functools.partial(_attn_kernel, tq=tq, tk=tk),
        grid=(t // tq, t // tk),
        in_specs=[pl.BlockSpec((tq, DSA_W), lambda i, j: (i, 0)),
                  pl.BlockSpec((tk, DSA_W), lambda i, j: (jnp.minimum(j, last(i)), 0)),
                  pl.BlockSpec((tk, DSA_W), lambda i, j: (jnp.minimum(j, last(i)), 0)),
                  pl.BlockSpec((tq, tk), lambda i, j: (i, jnp.minimum(j, last(i))))],
        out_spec=pl.BlockSpec((tq, DSA_W), lambda i, j: (i, 0)),
        out_shape=jax.ShapeDtypeStruct((t, DSA_W), MXU_DTYPE),
        scratch_shapes=[pltpu.VMEM((DSA_HEADS, tq, LANES), F32),
                        pltpu.VMEM((DSA_HEADS, tq, LANES), F32),
                        pltpu.VMEM((tq, DSA_W), F32)],
        name="dsa_attention",
        operands=(q, k, v, bias),
        cast=cast and (*cast, tick, before(t // tq), 1))


def _pad_heads(w, heads, width, padded):
    lead = w.shape[:-1]
    w = w.reshape(lead + (heads, width))
    w = jnp.pad(w, [(0, 0)] * len(lead) + [(0, 0), (0, padded - width)])
    return w.reshape(lead + (heads * padded,))


def _pack_plan():
    src, start = {}, 0
    names = ("ml_q", "ml_k", "ml_v", "ml_i", "ml_f", "ml_o", "dsa_cq", "dsa_k", "dsa_v", "idx_k", "idx_w",
             "hg_q", "hg_f", "hg_i", "hg_g")
    for name, width in zip(names, IN_SPLITS):
        src[name] = start
        start += width
    copies = []
    for h in range(ML_HEADS):
        for dst0, name in ((OFF_ML_Q, "ml_q"), (OFF_ML_K, "ml_k")):
            copies.append((dst0 + h * ML_DQK_PAD, src[name] + h * ML_DQK, ML_DQK))
    copies += [(OFF_DSA_K, src["dsa_k"], DSA_W), (OFF_ML_V, src["ml_v"], ML_V_W), (OFF_ML_O, src["ml_o"], ML_V_W),
               (OFF_HG_Q, src["hg_q"], HG_K_W), (OFF_HG_F, src["hg_f"], HG_K_W),
               (OFF_HG_I, src["hg_i"], HG_V_W), (OFF_HG_G, src["hg_g"], HG_V_W),
               (OFF_DSA_V, src["dsa_v"], DSA_W),
               (OFF_GATES, src["ml_i"], 2 * ML_HEADS),
               (OFF_DSA_CQ, src["dsa_cq"], DSA_Q_RANK),
               (OFF_IDX, src["idx_k"], IDX_DH + IDX_HEADS)]
    return copies


def _pack_kernel(wt_ref, o_ref):
    kt = o_ref.shape[0]
    for dst, src, width in _pack_plan():
        span = -(-width // LANES) * LANES
        seg = wt_ref[src:src + span, :].T
        if span != width:
            lane = lax.broadcasted_iota(jnp.int32, seg.shape, 1)
            seg = jnp.where(lane < width, seg, 0.0)
        o_ref[:, dst:dst + span] = seg.astype(o_ref.dtype)
    tail = OFF_IDX + LANES
    o_ref[:, tail:] = jnp.zeros((kt, D_IN_PAD - tail), o_ref.dtype)


def _pack_w_in(w_in, *, kt):
    depth, d, n = w_in.shape
    wt = jnp.swapaxes(w_in, 1, 2)
    return pl.pallas_call(
        _pack_kernel,
        grid=(depth, d // kt),
        in_specs=[pl.BlockSpec((None, n, kt), lambda l, i: (l, 0, i))],
        out_specs=pl.BlockSpec((None, kt, D_IN_PAD), lambda l, i: (l, i, 0)),
        out_shape=jax.ShapeDtypeStruct((depth, d, D_IN_PAD), MXU_DTYPE),
        compiler_params=_params("parallel", "parallel"),
        name="pack_w_in",
    )(wt)


def _rope_freqs():
    def lanes(d):
        rot = d // ROPE_FRACTION
        half = rot // 2
        inv = jnp.power(ROPE_THETA, -jnp.arange(half, dtype=F32) * (2.0 / rot))
        return jnp.concatenate([inv, inv, jnp.zeros((LANES - rot,), F32)])
    return jnp.stack([lanes(DSA_DH), lanes(IDX_DH)])


def kernel(x, positions, ln_mix_g, w_in, ml_conv_w, ml_gate_b, ml_norm_g, dsa_q_norm_g, dsa_w_uq,
           hg_lb_logits, hg_norm_g, w_out, ln_mlp_g, w_up, w_down, ln_final_g):
    bsz, t, d = x.shape
    assert bsz == 1 and t % 512 == 0 and d == D_MODEL
    depth = w_in.shape[0]
    topk = min(TOPK_MAX, t // 4)
    xs = x.reshape(t, d)
    pos = positions.reshape(t, 1)

    w_in_p = _pack_w_in(w_in, kt=256)
    conv_q = _pad_heads(ml_conv_w[..., :ML_QK_W], ML_HEADS, ML_DQK, ML_DQK_PAD)
    conv_k = _pad_heads(ml_conv_w[..., ML_QK_W:], ML_HEADS, ML_DQK, ML_DQK_PAD)
    conv_p = jnp.concatenate([conv_q.reshape(depth, ML_CONV, ML_HEADS, ML_DQK_PAD),
                              conv_k.reshape(depth, ML_CONV, ML_HEADS, ML_DQK_PAD)], axis=-1)
    conv_p = jnp.transpose(conv_p, (0, 2, 1, 3))
    gate_b = jnp.pad(ml_gate_b.reshape(depth, 1, 2 * ML_HEADS), ((0, 0), (0, 0), (0, LANES - 2 * ML_HEADS)))
    w_uq_p = jnp.concatenate([dsa_w_uq[..., :DSA_W],
                              _pad_heads(dsa_w_uq[..., DSA_W:], IDX_HEADS, IDX_DH, IDX_SLOT)],
                             axis=-1).astype(MXU_DTYPE)
    freqs = _rope_freqs()
    lb_cum = jnp.cumsum(jax.nn.softmax(hg_lb_logits.astype(F32), axis=0), axis=0)
    lb = lb_cum - lb_cum[:1]
    lb_tab = jnp.stack([jnp.log(lb), jnp.log1p(-lb), 1.0 - lb], axis=1)

    for layer in range(depth):
        u = _norm_matmul(xs, ln_mix_g[layer], w_in_p, layer, tm=512, tn=1024, act=False, out_dtype=F32)
        y_a, w_out_b = _mlstm(u, conv_p[layer], gate_b[layer], ml_norm_g[layer], rows=256, cast=(w_out, layer))
        q_r, qi_r, k_r, v_b, ki_r, wi = _dsa_prep(u, pos, dsa_q_norm_g[layer], w_uq_p[layer], freqs, rows=256)
        bias = _indexer(qi_r, wi, ki_r, topk=topk)
        y_b, w_down_b = _attention(q_r, k_r, v_b, bias, tq=512, tk=1024, cast=(w_down, layer))
        y_c, w_up_b = _hgrn2(u, lb_tab[layer], hg_norm_g[layer], rows=256, cast=(w_up, layer))
        xs = _mix_out(y_a, y_b, y_c, w_out_b[None], 0, xs, tm=512, tn=1024)
        a = _norm_matmul(xs, ln_mlp_g[layer], w_up_b[None], 0, tm=512, tn=1024, act=True, out_dtype=MXU_DTYPE)
        xs = _matmul_res(a, w_down_b[None], 0, xs, tm=1024, tn=1024, tk=2048)
    return _final_norm(xs, ln_final_g, tm=256).reshape(bsz, t, d)
```

```python
import functools

import jax
import jax.numpy as jnp
import numpy as np
from jax import lax
from jax.experimental import pallas as pl
from jax.experimental.pallas import tpu as pltpu

F32 = jnp.float32
BF16 = jnp.bfloat16
MXU_DTYPE = jnp.bfloat16

D_MODEL = 4096
DEPTH = 4
CHUNK = 64
EPS = 1e-6
ROPE_THETA = 500000.0
ROPE_FRACTION = 4
ML_HEADS = 4
ML_DV = 384
ML_DQK = ML_DV // 2
ML_DQK_PAD = 256
ML_CONV = 4
ML_GATE_CAP = 15.0
DSA_HEADS = 8
DSA_DH = 128
DSA_Q_RANK = 384
IDX_HEADS = 8
IDX_DH = 64
IDX_SLOT = 128
TOPK_MAX = 256
HG_HEADS = 12
HG_DK = 128
HG_DV = 128
D_FF = 4 * D_MODEL

ML_QK_W = ML_HEADS * ML_DQK
ML_V_W = ML_HEADS * ML_DV
DSA_W = DSA_HEADS * DSA_DH
IDX_W = IDX_HEADS * IDX_DH
HG_K_W = HG_HEADS * HG_DK
HG_V_W = HG_HEADS * HG_DV
D_MIX = ML_V_W + DSA_W + HG_V_W
IN_SPLITS = (ML_QK_W, ML_QK_W, ML_V_W, ML_HEADS, ML_HEADS, ML_V_W,
             DSA_Q_RANK, DSA_W, DSA_W, IDX_DH, IDX_HEADS,
             HG_K_W, HG_K_W, HG_V_W, HG_V_W)

LANES = 128
VMEM_LIMIT = 56 * 1024 * 1024

OFF_ML_Q = 0
OFF_ML_K = 768
OFF_ML_V = 1536
OFF_ML_O = 3072
OFF_HG_Q = 4608
OFF_HG_F = 6144
OFF_HG_I = 7680
OFF_HG_G = 9216
OFF_DSA_CQ = 10752
OFF_SMALL = 11136
OFF_DSA_K = 11264
OFF_DSA_V = 12288
D_IN_PAD = 13312
SMALL_GATES = 0
SMALL_IDX_K = 2 * ML_HEADS
SMALL_IDX_W = SMALL_IDX_K + IDX_DH

LOG2E = 1.4426950408889634
Q_SCALE = DSA_DH ** -0.5 * LOG2E

INT_MIN = -2 ** 31
KEY_NEG_INF = -2139095041
NEG_BIG = -1e30


def _mxu(a):
    return a.astype(MXU_DTYPE)


def _dot(a, b):
    return jnp.dot(_mxu(a), _mxu(b), preferred_element_type=F32)


def _dot_nt(a, b):
    return lax.dot_general(_mxu(a), _mxu(b), (((1,), (1,)), ((), ())), preferred_element_type=F32)


def _dot_tn(a, b):
    return lax.dot_general(_mxu(a), _mxu(b), (((0,), (0,)), ((), ())), preferred_element_type=F32)


def _dot_f32(a, b):
    return jnp.dot(a, b, precision=lax.Precision.HIGHEST, preferred_element_type=F32)


def _sigmoid(x):
    return 1.0 / (1.0 + jnp.exp(-x))


def _log_sigmoid(x):
    return jnp.minimum(x, 0.0) - jnp.log1p(jnp.exp(-jnp.abs(x)))


def _params(*sem):
    return pltpu.CompilerParams(dimension_semantics=sem, vmem_limit_bytes=VMEM_LIMIT)


BF16_SUBLANES = 16


def _mixer_call(kernel_fn, *, grid, in_specs, out_spec, out_shape, scratch_shapes, name, operands, cast=None):
    params = _params("parallel", "arbitrary")
    if cast is None:
        return pl.pallas_call(kernel_fn, grid=grid, in_specs=in_specs, out_specs=out_spec, out_shape=out_shape,
                              scratch_shapes=scratch_shapes, compiler_params=params, name=name)(*operands)
    w, layer, tick, ticks, every = cast
    params = _params("arbitrary", "arbitrary")
    rows, cols = w.shape[1:]
    blocks = ticks // every
    while rows % blocks or (rows // blocks) % BF16_SUBLANES:
        blocks -= 1
    block = lambda a, b: jnp.minimum(tick(a, b) // every, blocks - 1)
    n_in = len(in_specs)

    def body(*refs):
        w_ref, w_out = refs[n_in], refs[n_in + 2]
        now = tick(pl.program_id(0), pl.program_id(1))

        @pl.when((now % every == 0) & (now // every < blocks))
        def _():
            w_out[...] = w_ref[...].astype(w_out.dtype)

        kernel_fn(*refs[:n_in], refs[n_in + 1], *refs[n_in + 3:])

    return pl.pallas_call(
        body, grid=grid,
        in_specs=in_specs + [pl.BlockSpec((None, rows // blocks, cols), lambda a, b: (layer, block(a, b), 0))],
        out_specs=(out_spec, pl.BlockSpec((rows // blocks, cols), lambda a, b: (block(a, b), 0))),
        out_shape=(out_shape, jax.ShapeDtypeStruct((rows, cols), MXU_DTYPE)),
        scratch_shapes=scratch_shapes, compiler_params=params, name=name)(*operands, w)


def _norm_matmul_kernel(x_ref, g_ref, w_ref, o_ref, h_ref, *, act):
    @pl.when(pl.program_id(1) == 0)
    def _():
        x = x_ref[...]
        ms = jnp.mean(x * x, axis=-1, keepdims=True)
        h_ref[...] = (x * lax.rsqrt(ms + EPS) * g_ref[...]).astype(h_ref.dtype)

    y = jnp.dot(h_ref[...], w_ref[...], preferred_element_type=F32)
    if act:
        y = jnp.square(jnp.maximum(y, 0.0))
    o_ref[...] = y.astype(o_ref.dtype)


def _norm_matmul(x, g, w, layer, *, tm, tn, act, out_dtype):
    m, k = x.shape
    n = w.shape[2]
    return pl.pallas_call(
        functools.partial(_norm_matmul_kernel, act=act),
        grid=(m // tm, n // tn),
        in_specs=[pl.BlockSpec((tm, k), lambda i, j: (i, 0)),
                  pl.BlockSpec((1, k), lambda i, j: (0, 0)),
                  pl.BlockSpec((None, k, tn), lambda i, j: (layer, 0, j))],
        out_specs=pl.BlockSpec((tm, tn), lambda i, j: (i, j)),
        out_shape=jax.ShapeDtypeStruct((m, n), out_dtype),
        scratch_shapes=[pltpu.VMEM((tm, k), MXU_DTYPE)],
        compiler_params=_params("parallel", "arbitrary"),
        name="norm_matmul",
    )(x, g.reshape(1, k), w)


def _matmul_res_kernel(a_ref, w_ref, r_ref, o_ref):
    part = jnp.dot(a_ref[...], w_ref[...], preferred_element_type=F32)

    @pl.when(pl.program_id(2) == 0)
    def _():
        o_ref[...] = r_ref[...] + part

    @pl.when(pl.program_id(2) != 0)
    def _():
        o_ref[...] += part


def _matmul_res(a, w, layer, res, *, tm, tn, tk):
    m, k = a.shape
    n = w.shape[2]
    return pl.pallas_call(
        _matmul_res_kernel,
        grid=(m // tm, n // tn, k // tk),
        in_specs=[pl.BlockSpec((tm, tk), lambda i, j, kk: (i, kk)),
                  pl.BlockSpec((None, tk, tn), lambda i, j, kk: (layer, kk, j)),
                  pl.BlockSpec((tm, tn), lambda i, j, kk: (i, j))],
        out_specs=pl.BlockSpec((tm, tn), lambda i, j, kk: (i, j)),
        out_shape=jax.ShapeDtypeStruct((m, n), F32),
        compiler_params=_params("parallel", "parallel", "arbitrary"),
        name="matmul_res",
    )(a, w, res)


def _mix_out_kernel(ya_ref, yb_ref, yc_ref, w_ref, r_ref, o_ref):
    acc = r_ref[...]
    acc += jnp.dot(ya_ref[...], w_ref[0:ML_V_W, :], preferred_element_type=F32)
    acc += jnp.dot(yb_ref[...], w_ref[ML_V_W:ML_V_W + DSA_W, :], preferred_element_type=F32)
    acc += jnp.dot(yc_ref[...], w_ref[ML_V_W + DSA_W:D_MIX, :], preferred_element_type=F32)
    o_ref[...] = acc


def _mix_out(ya, yb, yc, w, layer, res, *, tm, tn):
    m = ya.shape[0]
    n = w.shape[2]
    return pl.pallas_call(
        _mix_out_kernel,
        grid=(m // tm, n // tn),
        in_specs=[pl.BlockSpec((tm, ML_V_W), lambda i, j: (i, 0)),
                  pl.BlockSpec((tm, DSA_W), lambda i, j: (i, 0)),
                  pl.BlockSpec((tm, HG_V_W), lambda i, j: (i, 0)),
                  pl.BlockSpec((None, D_MIX, tn), lambda i, j: (layer, 0, j)),
                  pl.BlockSpec((tm, tn), lambda i, j: (i, j))],
        out_specs=pl.BlockSpec((tm, tn), lambda i, j: (i, j)),
        out_shape=jax.ShapeDtypeStruct((m, n), F32),
        compiler_params=_params("parallel", "parallel"),
        name="mix_out",
    )(ya, yb, yc, w, res)


def _final_norm_kernel(x_ref, g_ref, o_ref):
    x = x_ref[...]
    ms = jnp.mean(x * x, axis=-1, keepdims=True)
    o_ref[...] = x * lax.rsqrt(ms + EPS) * g_ref[...]


def _final_norm(x, g, *, tm):
    m, k = x.shape
    return pl.pallas_call(
        _final_norm_kernel,
        grid=(m // tm,),
        in_specs=[pl.BlockSpec((tm, k), lambda i: (i, 0)),
                  pl.BlockSpec((1, k), lambda i: (0, 0))],
        out_specs=pl.BlockSpec((tm, k), lambda i: (i, 0)),
        out_shape=jax.ShapeDtypeStruct((m, k), F32),
        compiler_params=_params("parallel"),
        name="final_norm",
    )(x, g.reshape(1, k))


def _mlstm_kernel(q_ref, k_ref, v_ref, o_ref, sm_ref, cw_ref, gb_ref, ng_ref, out_ref,
                  xbuf, c_ref, *, rows):
    @pl.when(pl.program_id(1) == 0)
    def _():
        xbuf[0:8, :] = jnp.zeros((8, 2 * ML_QK_W), F32)
        c_ref[...] = jnp.zeros_like(c_ref)

    xbuf[8:8 + rows, 0:ML_QK_W] = q_ref[...]
    xbuf[8:8 + rows, ML_QK_W:] = k_ref[...]
    cw = cw_ref[...]
    acc = xbuf[8:8 + rows, :] * cw[ML_CONV - 1:ML_CONV, :]
    for j in range(1, ML_CONV):
        acc += xbuf[8 - j:8 - j + rows, :] * cw[ML_CONV - 1 - j:ML_CONV - j, :]
    xbuf[0:8, :] = xbuf[rows:rows + 8, :]
    qk = acc * _sigmoid(acc)
    qk = jnp.concatenate([qk, jnp.zeros((rows, LANES), F32)], axis=1)

    capped = ML_GATE_CAP * jnp.tanh((sm_ref[...] + gb_ref[...]) * (1.0 / ML_GATE_CAP))
    lsig = _log_sigmoid(capped)
    r_i = lax.broadcasted_iota(jnp.int32, (rows, rows), 0)
    c_i = lax.broadcasted_iota(jnp.int32, (rows, rows), 1)
    causal = c_i <= r_i
    b_all = _dot_f32(jnp.where(causal, 1.0, 0.0), lsig)
    capped_t, b_all_t = capped.T, b_all.T
    one_col = jnp.where(lax.broadcasted_iota(jnp.int32, (rows, LANES), 1) == 0, 1.0, 0.0)
    real = lax.broadcasted_iota(jnp.int32, (rows, ML_DQK_PAD), 1) < ML_DQK

    for h in range(ML_HEADS):
        q = jnp.where(real, qk[:, h * ML_DQK:h * ML_DQK + ML_DQK_PAD], 0.0)
        k = jnp.where(real, qk[:, ML_QK_W + h * ML_DQK:ML_QK_W + h * ML_DQK + ML_DQK_PAD], 0.0) * (ML_DQK ** -0.5)
        ig_col, b_col = capped[:, h:h + 1], b_all[:, ML_HEADS + h:ML_HEADS + h + 1]
        ig_row, b_row = capped_t[h:h + 1, :], b_all_t[ML_HEADS + h:ML_HEADS + h + 1, :]
        dv = slice(h * ML_DV, (h + 1) * ML_DV)

        dmat = jnp.exp(jnp.where(causal, b_col - b_row + ig_row, -jnp.inf))
        s = _dot_nt(q, k) * dmat
        v_ext = jnp.concatenate([v_ref[:, dv], one_col], axis=1)
        c_old = c_ref[h]
        num_ext = _dot(s, v_ext) + jnp.exp(b_col) * _dot(q, c_old)
        num = num_ext[:, 0:ML_DV]
        den = num_ext[:, ML_DV:ML_DV + 1]
        hh = num / jnp.maximum(jnp.abs(den), 1.0)

        b_last = b_col[rows - 1:rows, :]
        w_s = jnp.exp(b_last - b_col + ig_col)
        c_ref[h] = jnp.exp(b_last) * c_old + _dot_tn(k, w_s * v_ext)

        ms = jnp.mean(hh * hh, axis=-1, keepdims=True)
        y = hh * lax.rsqrt(ms + EPS) * ng_ref[:, dv]
        out_ref[:, dv] = (_sigmoid(o_ref[:, dv]) * y).astype(out_ref.dtype)


def _mlstm(u, conv_w, gate_b, norm_g, *, rows, cast=None):
    t = u.shape[0]
    return _mixer_call(
        functools.partial(_mlstm_kernel, rows=rows),
        grid=(1, t // rows),
        in_specs=[pl.BlockSpec((rows, ML_QK_W), lambda _, c: (c, OFF_ML_Q // ML_QK_W)),
                  pl.BlockSpec((rows, ML_QK_W), lambda _, c: (c, OFF_ML_K // ML_QK_W)),
                  pl.BlockSpec((rows, ML_V_W), lambda _, c: (c, OFF_ML_V // ML_V_W)),
                  pl.BlockSpec((rows, ML_V_W), lambda _, c: (c, OFF_ML_O // ML_V_W)),
                  pl.BlockSpec((rows, LANES), lambda _, c: (c, OFF_SMALL // LANES)),
                  pl.BlockSpec((ML_CONV, 2 * ML_QK_W), lambda _, c: (0, 0)),
                  pl.BlockSpec((1, LANES), lambda _, c: (0, 0)),
                  pl.BlockSpec((1, ML_V_W), lambda _, c: (0, 0))],
        out_spec=pl.BlockSpec((rows, ML_V_W), lambda _, c: (c, 0)),
        out_shape=jax.ShapeDtypeStruct((t, ML_V_W), MXU_DTYPE),
        scratch_shapes=[pltpu.VMEM((rows + 8, 2 * ML_QK_W), F32),
                        pltpu.VMEM((ML_HEADS, ML_DQK_PAD, ML_DV + LANES), F32)],
        name="mlstm",
        operands=(u, u, u, u, u, conv_w, gate_b, norm_g.reshape(1, ML_V_W)),
        cast=cast and (*cast, lambda _, c: c, t // rows, 1))


SUBLANES = 8
HG_CAST_EVERY = 1


def _pair_level_table(rows):
    t = np.arange(rows)[:, None]
    s = np.arange(rows)[None, :]
    x = np.bitwise_xor(t, s)
    lvl = np.floor(np.log2(np.maximum(x, 1))).astype(np.int32)
    return jnp.asarray(np.where(t > s, lvl, -1).astype(np.int32))


def _block_sums(log_f, rows):
    sub = lax.broadcasted_iota(jnp.int32, log_f.shape, 0)
    groups = rows // SUBLANES

    def row_of_group(x, j):
        x3 = x.reshape(groups, SUBLANES, x.shape[-1])
        return jnp.broadcast_to(x3[:, j:j + 1, :], x3.shape).reshape(x.shape)

    c = log_f
    e = jnp.zeros_like(log_f)
    out = [(c, e)]
    odd = (sub & 1) == 1
    c, e = (c + jnp.where(odd, pltpu.roll(c, 1, 0), 0.0),
            e + jnp.where(odd, 0.0, pltpu.roll(c, rows - 1, 0)))
    out.append((c, e))
    r8 = sub & 7
    c, e = (c + jnp.where((r8 == 2) | (r8 == 3), row_of_group(c, 1),
                          jnp.where((r8 == 6) | (r8 == 7), row_of_group(c, 5), 0.0)),
            e + jnp.where((r8 == 0) | (r8 == 1), row_of_group(c, 3),
                          jnp.where((r8 == 4) | (r8 == 5), row_of_group(c, 7), 0.0)))
    out.append((c, e))
    c, e = (c + jnp.where(r8 >= 4, row_of_group(c, 3), 0.0),
            e + jnp.where(r8 < 4, row_of_group(c, 7), 0.0))
    out.append((c, e))
    m = SUBLANES
    while m < rows:
        cs, es = [], []
        for p in range(rows // (2 * m)):
            lo, mid, hi = 2 * m * p, 2 * m * p + m, 2 * m * (p + 1)
            cs += [c[lo:mid], c[mid:hi] + c[mid - 1:mid]]
            es += [e[lo:mid] + c[hi - 1:hi], e[mid:hi]]
        c, e = jnp.concatenate(cs, axis=0), jnp.concatenate(es, axis=0)
        out.append((c, e))
        m *= 2
    return out


def _hgrn2_kernel(q_ref, f_ref, i_ref, g_ref, lb_ref, ng_ref, lvl_ref, out_ref, st_ref, *, rows):
    @pl.when(pl.program_id(1) == 0)
    def _():
        st_ref[...] = jnp.zeros_like(st_ref)

    qp = q_ref[...]
    q = qp * _sigmoid(qp)
    fp = f_ref[...]
    v = i_ref[...]
    gp = g_ref[...]
    log_lb = lb_ref[0:1, :]
    x2 = lb_ref[1:2, :] + _log_sigmoid(fp)
    log_f = jnp.maximum(log_lb, x2) + jnp.log1p(jnp.exp(-jnp.abs(log_lb - x2)))
    k = lb_ref[2:3, :] * _sigmoid(-fp)

    sums = _block_sums(log_f * LOG2E, rows)
    lvl = lvl_ref[...]
    tiles = rows // LANES
    tile_rows = [slice(r * LANES, (r + 1) * LANES) for r in range(tiles)]
    diag = [jnp.zeros((LANES, LANES), F32) for _ in range(tiles)]
    o_tiles = [None] * tiles
    for level, (c_m, e_m) in enumerate(sums[:-1]):
        m = 1 << level
        qh = q * jnp.exp2(c_m)
        kh = k * jnp.exp2(e_m)
        if m < LANES:
            for r, sl in enumerate(tile_rows):
                diag[r] = jnp.where(lvl == level, _dot_nt(qh[sl], kh[sl]), diag[r])
        else:
            for p in range(rows // (2 * m)):
                lo, mid = 2 * m * p, 2 * m * p + m
                for r in range(mid // LANES, (mid + m) // LANES):
                    part = _dot(_dot_nt(qh[tile_rows[r]], kh[lo:mid]), v[lo:mid])
                    o_tiles[r] = part if o_tiles[r] is None else o_tiles[r] + part
    for r, sl in enumerate(tile_rows):
        part = _dot(diag[r], v[sl])
        o_tiles[r] = part if o_tiles[r] is None else o_tiles[r] + part
    b, after = sums[-1]

    st = st_ref[...]
    o = (jnp.concatenate(o_tiles, axis=0) + jnp.sum(q * k, axis=-1, keepdims=True) * v
         + _dot_nt(q * jnp.exp2(b), st))
    st_ref[...] = st * jnp.exp2(b[rows - 1:rows, :]) + _dot_tn(v, k * jnp.exp2(after))

    ms = jnp.mean(o * o, axis=-1, keepdims=True)
    y = o * lax.rsqrt(ms + EPS) * ng_ref[...]
    out_ref[...] = (y * (gp * _sigmoid(gp))).astype(out_ref.dtype)


def _hgrn2(u, lb_tab, norm_g, *, rows, cast=None):
    t = u.shape[0]
    qb, fb = OFF_HG_Q // HG_DK, OFF_HG_F // HG_DK
    ib, gb = OFF_HG_I // HG_DV, OFF_HG_G // HG_DV
    return _mixer_call(
        functools.partial(_hgrn2_kernel, rows=rows),
        grid=(HG_HEADS, t // rows),
        in_specs=[pl.BlockSpec((rows, HG_DK), lambda h, c: (c, qb + h)),
                  pl.BlockSpec((rows, HG_DK), lambda h, c: (c, fb + h)),
                  pl.BlockSpec((rows, HG_DV), lambda h, c: (c, ib + h)),
                  pl.BlockSpec((rows, HG_DV), lambda h, c: (c, gb + h)),
                  pl.BlockSpec((3, HG_DK), lambda h, c: (0, h)),
                  pl.BlockSpec((1, HG_DV), lambda h, c: (0, h)),
                  pl.BlockSpec((LANES, LANES), lambda h, c: (0, 0))],
        out_spec=pl.BlockSpec((rows, HG_DV), lambda h, c: (c, h)),
        out_shape=jax.ShapeDtypeStruct((t, HG_V_W), MXU_DTYPE),
        scratch_shapes=[pltpu.VMEM((HG_DV, HG_DK), F32)],
        name="hgrn2",
        operands=(u, u, u, u, lb_tab, norm_g.reshape(1, HG_V_W), _pair_level_table(LANES)),
        cast=cast and (*cast, lambda h, c: h * (t // rows) + c, HG_HEADS * (t // rows), HG_CAST_EVERY))


def _rope_slab(u, cc, sa, sb, half):
    return u * cc + pltpu.roll(u, LANES - half, 1) * sa + pltpu.roll(u, half, 1) * sb


def _rope_coeffs(ang, half):
    lane = lax.broadcasted_iota(jnp.int32, ang.shape, 1)
    cos, sin = jnp.cos(ang), jnp.sin(ang)
    cc = jnp.where(lane < 2 * half, cos, 1.0)
    sa = jnp.where(lane < half, -sin, 0.0)
    sb = jnp.where((lane >= half) & (lane < 2 * half), sin, 0.0)
    return cc, sa, sb


def _dsa_prep_kernel(cq_ref, k_ref, v_ref, idx_ref, pos_ref, g_ref, w_ref, fr_ref,
                     q_out, qi_out, k_out, v_out, ki_out, wi_out):
    pos = pos_ref[...].astype(F32)
    cq = cq_ref[...]
    ms = jnp.mean(cq * cq, axis=-1, keepdims=True)
    hq = cq * lax.rsqrt(ms + EPS) * g_ref[...]
    q_all = _dot(hq, w_ref[...])

    half_a = DSA_DH // ROPE_FRACTION // 2
    half_i = IDX_DH // ROPE_FRACTION // 2
    ca = _rope_coeffs(pos * fr_ref[0:1, :], half_a)
    ci = _rope_coeffs(pos * fr_ref[1:2, :], half_i)

    kk = k_ref[...]
    for h in range(DSA_HEADS):
        sl = slice(h * DSA_DH, (h + 1) * DSA_DH)
        q_out[:, sl] = (_rope_slab(q_all[:, sl], *ca, half_a) * Q_SCALE).astype(q_out.dtype)
        k_out[:, sl] = _rope_slab(kk[:, sl], *ca, half_a).astype(k_out.dtype)
    for h in range(IDX_HEADS):
        src = slice(DSA_W + h * IDX_SLOT, DSA_W + (h + 1) * IDX_SLOT)
        qi_out[h] = _rope_slab(q_all[:, src], *ci, half_i).astype(qi_out.dtype)
    v_out[...] = v_ref[...].astype(v_out.dtype)

    small = idx_ref[...]
    lane = lax.broadcasted_iota(jnp.int32, small.shape, 1)
    ki = jnp.where(lane < IDX_DH, pltpu.roll(small, LANES - SMALL_IDX_K, 1), 0.0)
    ki_out[...] = _rope_slab(ki, *ci, half_i).astype(ki_out.dtype)
    wi = pltpu.roll(small, LANES - SMALL_IDX_W, 1) * (IDX_HEADS ** -0.5 * IDX_DH ** -0.5)
    wi_out[...] = jnp.where(lane < IDX_HEADS, wi, 0.0)


def _dsa_prep(u, pos, q_norm_g, w_uq_pad, freqs, *, rows):
    t = u.shape[0]
    nq = DSA_W + IDX_HEADS * IDX_SLOT
    outs = (jax.ShapeDtypeStruct((t, DSA_W), MXU_DTYPE),
            jax.ShapeDtypeStruct((IDX_HEADS, t, IDX_SLOT), MXU_DTYPE),
            jax.ShapeDtypeStruct((t, DSA_W), MXU_DTYPE),
            jax.ShapeDtypeStruct((t, DSA_W), MXU_DTYPE),
            jax.ShapeDtypeStruct((t, IDX_SLOT), MXU_DTYPE),
            jax.ShapeDtypeStruct((t, LANES), F32))
    row_spec = lambda w, blk: pl.BlockSpec((rows, w), lambda i: (i, blk))
    return pl.pallas_call(
        _dsa_prep_kernel,
        grid=(t // rows,),
        in_specs=[row_spec(DSA_Q_RANK, OFF_DSA_CQ // DSA_Q_RANK),
                  row_spec(DSA_W, OFF_DSA_K // DSA_W),
                  row_spec(DSA_W, OFF_DSA_V // DSA_W),
                  row_spec(LANES, OFF_SMALL // LANES),
                  pl.BlockSpec((rows, 1), lambda i: (i, 0)),
                  pl.BlockSpec((1, DSA_Q_RANK), lambda i: (0, 0)),
                  pl.BlockSpec((DSA_Q_RANK, nq), lambda i: (0, 0)),
                  pl.BlockSpec((2, LANES), lambda i: (0, 0))],
        out_specs=(row_spec(DSA_W, 0),
                   pl.BlockSpec((IDX_HEADS, rows, IDX_SLOT), lambda i: (0, i, 0)),
                   row_spec(DSA_W, 0), row_spec(DSA_W, 0), row_spec(IDX_SLOT, 0), row_spec(LANES, 0)),
        out_shape=outs,
        compiler_params=_params("parallel"),
        name="dsa_prep",
    )(u, u, u, u, pos, q_norm_g.reshape(1, DSA_Q_RANK), w_uq_pad, freqs)


IDX_TQ = 512
IDX_TK = 512
IDX_SLAB = 128


def _indexer_kernel(qi_ref, wi_ref, ki_ref, incl_ref, bias_ref, key_ref, *, topk):
    tq, tk = IDX_TQ, IDX_TK
    qb = pl.program_id(0)
    nkb = ((qb + 1) * tq + tk - 1) // tk
    q_all = qi_ref[...].reshape(IDX_HEADS * tq, IDX_SLOT)
    w = wi_ref[...]
    w_col = jnp.concatenate([w[:, h:h + 1] for h in range(IDX_HEADS)], axis=0)
    row_chunk = (qb * tq + lax.broadcasted_iota(jnp.int32, (tq, 1), 0)) // CHUNK
    col_in_blk = lax.broadcasted_iota(jnp.int32, (1, tk), 1)

    def score_body(kb, carry):
        kt = ki_ref[pl.ds(pl.multiple_of(kb * tk, tk), tk), :]
        weighted = jnp.maximum(_dot_nt(q_all, kt), 0.0) * w_col
        parts = [weighted[h * tq:(h + 1) * tq] for h in range(IDX_HEADS)]
        while len(parts) > 1:
            parts = [a + b for a, b in zip(parts[0::2], parts[1::2])]
        col_chunk = (kb * tk + col_in_blk) // CHUNK
        score = jnp.where(col_chunk <= row_chunk, parts[0], -jnp.inf)
        bits = pltpu.bitcast(score, jnp.int32)
        bits = jnp.where(bits == INT_MIN, 0, bits)
        key_ref[kb] = jnp.where(bits < 0, bits ^ 0x7FFFFFFF, bits)
        return carry

    lax.fori_loop(0, nkb, score_body, 0)

    ones_mat = jnp.ones((LANES, LANES), MXU_DTYPE)

    def count_ge(cand):
        counts = []
        for r0 in range(0, tq, IDX_SLAB):
            cand_r = cand[r0:r0 + IDX_SLAB]

            def body(kb, cnt, r0=r0, cand_r=cand_r):
                for j in range(tk // LANES):
                    cnt += jnp.where(key_ref[kb, r0:r0 + IDX_SLAB, j * LANES:(j + 1) * LANES] >= cand_r, 1, 0)
                return cnt
            counts.append(lax.fori_loop(0, nkb, body, jnp.zeros((IDX_SLAB, LANES), jnp.int32)))
        cnt = jnp.concatenate(counts, axis=0)
        return jnp.dot(cnt.astype(F32).astype(MXU_DTYPE), ones_mat, preferred_element_type=F32)

    zero = jnp.zeros((tq, LANES), jnp.int32)
    cnt0 = count_ge(zero)
    thr_rep = jnp.where(cnt0 >= topk, zero, INT_MIN)
    at_thr = jnp.where(cnt0 >= topk, cnt0, (nkb * tk).astype(F32))

    def bit_body(i, carry):
        thr_rep, at_thr = carry
        cand = thr_rep + jnp.left_shift(jnp.int32(1), 30 - i)
        cnt = count_ge(cand)
        ok = cnt >= topk
        return jnp.where(ok, cand, thr_rep), jnp.where(ok, cnt, at_thr)

    thr_rep, at_thr = lax.fori_loop(0, 31, bit_body, (thr_rep, at_thr))
    thr = thr_rep[:, 0:1]
    exact = jnp.all(at_thr == topk)

    bias_ref[...] = jnp.full(bias_ref.shape, NEG_BIG, bias_ref.dtype)

    @pl.when(exact)
    def _():
        def emit_body(kb, carry):
            key = key_ref[kb]
            take = (key >= thr) & (key > KEY_NEG_INF)
            bias_ref[:, pl.ds(pl.multiple_of(kb * tk, tk), tk)] = jnp.where(take, 0.0, NEG_BIG).astype(bias_ref.dtype)
            return carry

        lax.fori_loop(0, nkb, emit_body, 0)

    @pl.when(jnp.logical_not(exact))
    def _():
        need = topk - count_ge(thr_rep + 1)[:, 0:1]

        def emit_body(kb, seen):
            key = key_ref[kb]
            eq = key == thr
            eq_f = jnp.where(eq, 1.0, 0.0)
            rank = seen + jnp.dot(eq_f.astype(MXU_DTYPE), incl_ref[...], preferred_element_type=F32)
            take = ((key > thr) | (eq & (rank <= need))) & (key > KEY_NEG_INF)
            bias_ref[:, pl.ds(pl.multiple_of(kb * tk, tk), tk)] = jnp.where(take, 0.0, NEG_BIG).astype(bias_ref.dtype)
            return seen + jnp.sum(eq_f, axis=1, keepdims=True)

        lax.fori_loop(0, nkb, emit_body, jnp.zeros((tq, 1), F32))


def _indexer(qi, wi, ki, *, topk):
    t = ki.shape[0]
    incl = jnp.asarray(np.triu(np.ones((IDX_TK, IDX_TK), np.float32)), MXU_DTYPE)
    return pl.pallas_call(
        functools.partial(_indexer_kernel, topk=topk),
        grid=(t // IDX_TQ,),
        in_specs=[pl.BlockSpec((IDX_HEADS, IDX_TQ, IDX_SLOT), lambda i: (0, i, 0)),
                  pl.BlockSpec((IDX_TQ, LANES), lambda i: (i, 0)),
                  pl.BlockSpec((t, IDX_SLOT), lambda i: (0, 0)),
                  pl.BlockSpec((IDX_TK, IDX_TK), lambda i: (0, 0))],
        out_specs=pl.BlockSpec((IDX_TQ, t), lambda i: (i, 0)),
        out_shape=jax.ShapeDtypeStruct((t, t), BF16),
        scratch_shapes=[pltpu.VMEM((t // IDX_TK, IDX_TQ, IDX_TK), jnp.int32)],
        compiler_params=_params("parallel"),
        name="dsa_indexer",
    )(qi, wi, ki, incl)


def _attn_kernel(q_ref, k_ref, v_ref, bias_ref, o_ref, m_ref, l_ref, acc_ref, *, tq, tk):
    qb, kb = pl.program_id(0), pl.program_id(1)
    last = ((qb + 1) * tq - 1) // tk

    @pl.when(kb == 0)
    def _():
        m_ref[...] = jnp.full(m_ref.shape, NEG_BIG, F32)
        l_ref[...] = jnp.zeros_like(l_ref)
        acc_ref[...] = jnp.zeros_like(acc_ref)

    @pl.when(kb <= last)
    def _():
        bias = bias_ref[...].astype(F32)
        ones = jnp.ones((tk, LANES), v_ref.dtype)
        heads = [slice(h * DSA_DH, (h + 1) * DSA_DH) for h in range(DSA_HEADS)]
        qk = _dot_nt(q_ref[:, heads[0]], k_ref[:, heads[0]])
        for h, sl in enumerate(heads):
            s = qk + bias
            if h + 1 < DSA_HEADS:
                qk = _dot_nt(q_ref[:, heads[h + 1]], k_ref[:, heads[h + 1]])
            m_old = m_ref[h]
            m_new = jnp.maximum(m_old, jnp.max(s, axis=-1, keepdims=True))
            alpha = jnp.exp2(m_old - m_new)
            p = jnp.exp2(s - jnp.concatenate([m_new] * (tk // LANES), axis=1))
            pv = _dot(p, jnp.concatenate([v_ref[:, sl], ones], axis=1))
            l_ref[h] = alpha * l_ref[h] + pv[:, DSA_DH:]
            acc_ref[:, sl] = alpha * acc_ref[:, sl] + pv[:, :DSA_DH]
            m_ref[h] = m_new

    @pl.when(kb == last)
    def _():
        for h in range(DSA_HEADS):
            sl = slice(h * DSA_DH, (h + 1) * DSA_DH)
            o_ref[:, sl] = (acc_ref[:, sl] / l_ref[h]).astype(o_ref.dtype)


def _attention(q, k, v, bias, *, tq, tk, cast=None):
    t = q.shape[0]
    last = lambda i: ((i + 1) * tq - 1) // tk
    ratio = tk // tq
    assert tk == ratio * tq

    def before(i):
        m = i // ratio
        return i + ratio * (m * (m - 1) // 2) + (i - ratio * m) * m

    tick = lambda i, j: before(i) + jnp.minimum(j, last(i))
    return _mixer_call(
        functools.partial(_attn_kernel, tq=tq, tk=tk),
        grid=(t // tq, t // tk),
        in_specs=[pl.BlockSpec((tq, DSA_W), lambda i, j: (i, 0)),
                  pl.BlockSpec((tk, DSA_W), lambda i, j: (jnp.minimum(j, last(i)), 0)),
                  pl.BlockSpec((tk, DSA_W), lambda i, j: (jnp.minimum(j, last(i)), 0)),
                  pl.BlockSpec((tq, tk), lambda i, j: (i, jnp.minimum(j, last(i))))],
        out_spec=pl.BlockSpec((tq, DSA_W), lambda i, j: (i, 0)),
        out_shape=jax.ShapeDtypeStruct((t, DSA_W), MXU_DTYPE),
        scratch_shapes=[pltpu.VMEM((DSA_HEADS, tq, LANES), F32),
                        pltpu.VMEM((DSA_HEADS, tq, LANES), F32),
                        pltpu.VMEM((tq, DSA_W), F32)],
        name="dsa_attention",
        operands=(q, k, v, bias),
        cast=cast and (*cast, tick, before(t // tq), 1))


def _pad_heads(w, heads, width, padded):
    lead = w.shape[:-1]
    w = w.reshape(lead + (heads, width))
    w = jnp.pad(w, [(0, 0)] * len(lead) + [(0, 0), (0, padded - width)])
    return w.reshape(lead + (heads * padded,))


def _pack_plan():
    src, start = {}, 0
    names = ("ml_q", "ml_k", "ml_v", "ml_i", "ml_f", "ml_o", "dsa_cq", "dsa_k", "dsa_v", "idx_k", "idx_w",
             "hg_q", "hg_f", "hg_i", "hg_g")
    for name, width in zip(names, IN_SPLITS):
        src[name] = start
        start += width
    copies = [(OFF_ML_Q, src["ml_q"], ML_QK_W), (OFF_ML_K, src["ml_k"], ML_QK_W),
              (OFF_ML_V, src["ml_v"], ML_V_W), (OFF_ML_O, src["ml_o"], ML_V_W),
              (OFF_HG_Q, src["hg_q"], HG_K_W), (OFF_HG_F, src["hg_f"], HG_K_W),
              (OFF_HG_I, src["hg_i"], HG_V_W), (OFF_HG_G, src["hg_g"], HG_V_W),
              (OFF_DSA_CQ, src["dsa_cq"], DSA_Q_RANK), (OFF_DSA_K, src["dsa_k"], DSA_W), (OFF_DSA_V, src["dsa_v"], DSA_W)]
    small = [(src["ml_i"], 2 * ML_HEADS), (src["idx_k"], IDX_DH + IDX_HEADS)]
    return copies, small


def _pack_kernel(wt_ref, o_ref):
    kt = o_ref.shape[0]
    copies, small = _pack_plan()
    for dst, src, width in copies:
        o_ref[:, dst:dst + width] = wt_ref[src:src + width, :].T.astype(o_ref.dtype)
    rows = [wt_ref[src:src + width, :] for src, width in small]
    used = sum(width for _, width in small)
    rows.append(jnp.zeros((LANES - used, kt), wt_ref.dtype))
    o_ref[:, OFF_SMALL:OFF_SMALL + LANES] = jnp.concatenate(rows, axis=0).T.astype(o_ref.dtype)


def _pack_w_in(w_in, *, kt):
    depth, d, n = w_in.shape
    wt = jnp.swapaxes(w_in, 1, 2)
    return pl.pallas_call(
        _pack_kernel,
        grid=(depth, d // kt),
        in_specs=[pl.BlockSpec((None, n, kt), lambda l, i: (l, 0, i))],
        out_specs=pl.BlockSpec((None, kt, D_IN_PAD), lambda l, i: (l, i, 0)),
        out_shape=jax.ShapeDtypeStruct((depth, d, D_IN_PAD), MXU_DTYPE),
        compiler_params=_params("parallel", "parallel"),
        name="pack_w_in",
    )(wt)


def _rope_freqs():
    def lanes(d):
        rot = d // ROPE_FRACTION
        half = rot // 2
        inv = jnp.power(ROPE_THETA, -jnp.arange(half, dtype=F32) * (2.0 / rot))
        return jnp.concatenate([inv, inv, jnp.zeros((LANES - rot,), F32)])
    return jnp.stack([lanes(DSA_DH), lanes(IDX_DH)])


def kernel(x, positions, ln_mix_g, w_in, ml_conv_w, ml_gate_b, ml_norm_g, dsa_q_norm_g, dsa_w_uq,
           hg_lb_logits, hg_norm_g, w_out, ln_mlp_g, w_up, w_down, ln_final_g):
    bsz, t, d = x.shape
    assert bsz == 1 and t % 512 == 0 and d == D_MODEL
    depth = w_in.shape[0]
    topk = min(TOPK_MAX, t // 4)
    xs = x.reshape(t, d)
    pos = positions.reshape(t, 1)

    w_in_p = _pack_w_in(w_in, kt=256)
    gate_b = jnp.pad(ml_gate_b.reshape(depth, 1, 2 * ML_HEADS), ((0, 0), (0, 0), (0, LANES - 2 * ML_HEADS)))
    w_uq_p = jnp.concatenate([dsa_w_uq[..., :DSA_W],
                              _pad_heads(dsa_w_uq[..., DSA_W:], IDX_HEADS, IDX_DH, IDX_SLOT)],
                             axis=-1).astype(MXU_DTYPE)
    freqs = _rope_freqs()
    lb_cum = jnp.cumsum(jax.nn.softmax(hg_lb_logits.astype(F32), axis=0), axis=0)
    lb = lb_cum - lb_cum[:1]
    lb_tab = jnp.stack([jnp.log(lb), jnp.log1p(-lb), 1.0 - lb], axis=1)

    for layer in range(depth):
        u = _norm_matmul(xs, ln_mix_g[layer], w_in_p, layer, tm=512, tn=1024, act=False, out_dtype=F32)
        y_a, w_out_b = _mlstm(u, ml_conv_w[layer], gate_b[layer], ml_norm_g[layer], rows=256, cast=(w_out, layer))
        q_r, qi_r, k_r, v_b, ki_r, wi = _dsa_prep(u, pos, dsa_q_norm_g[layer], w_uq_p[layer], freqs, rows=256)
        bias = _indexer(qi_r, wi, ki_r, topk=topk)
        y_b, w_down_b = _attention(q_r, k_r, v_b, bias, tq=512, tk=1024, cast=(w_down, layer))
        y_c, w_up_b = _hgrn2(u, lb_tab[layer], hg_norm_g[layer], rows=256, cast=(w_up, layer))
        xs = _mix_out(y_a, y_b, y_c, w_out_b[None], 0, xs, tm=512, tn=1024)
        a = _norm_matmul(xs, ln_mlp_g[layer], w_up_b[None], 0, tm=512, tn=1024, act=True, out_dtype=MXU_DTYPE)
        xs = _matmul_res(a, w_down_b[None], 0, xs, tm=1024, tn=1024, tk=2048)
    return _final_norm(xs, ln_final_g, tm=256).reshape(bsz, t, d)
```

```python
import functools

import jax
import jax.numpy as jnp
import numpy as np
from jax import lax
from jax.experimental import pallas as pl
from jax.experimental.pallas import tpu as pltpu

F32 = jnp.float32
BF16 = jnp.bfloat16
MXU_DTYPE = jnp.bfloat16

D_MODEL = 4096
DEPTH = 4
CHUNK = 64
EPS = 1e-6
ROPE_THETA = 500000.0
ROPE_FRACTION = 4
ML_HEADS = 4
ML_DV = 384
ML_DQK = ML_DV // 2
ML_DQK_PAD = 256
ML_CONV = 4
ML_GATE_CAP = 15.0
DSA_HEADS = 8
DSA_DH = 128
DSA_Q_RANK = 384
IDX_HEADS = 8
IDX_DH = 64
IDX_SLOT = 128
TOPK_MAX = 256
HG_HEADS = 12
HG_DK = 128
HG_DV = 128
D_FF = 4 * D_MODEL

ML_QK_W = ML_HEADS * ML_DQK
ML_V_W = ML_HEADS * ML_DV
DSA_W = DSA_HEADS * DSA_DH
IDX_W = IDX_HEADS * IDX_DH
HG_K_W = HG_HEADS * HG_DK
HG_V_W = HG_HEADS * HG_DV
D_MIX = ML_V_W + DSA_W + HG_V_W
IN_SPLITS = (ML_QK_W, ML_QK_W, ML_V_W, ML_HEADS, ML_HEADS, ML_V_W,
             DSA_Q_RANK, DSA_W, DSA_W, IDX_DH, IDX_HEADS,
             HG_K_W, HG_K_W, HG_V_W, HG_V_W)

LANES = 128
VMEM_LIMIT = 56 * 1024 * 1024

OFF_ML_Q = 0
OFF_ML_K = 768
OFF_ML_V = 1536
OFF_ML_O = 3072
OFF_HG_Q = 4608
OFF_HG_F = 6144
OFF_HG_I = 7680
OFF_HG_G = 9216
OFF_DSA_CQ = 10752
OFF_SMALL = 11136
OFF_DSA_K = 11264
OFF_DSA_V = 12288
D_IN_PAD = 13312
SMALL_GATES = 0
SMALL_IDX_K = 2 * ML_HEADS
SMALL_IDX_W = SMALL_IDX_K + IDX_DH

LOG2E = 1.4426950408889634
Q_SCALE = DSA_DH ** -0.5 * LOG2E

INT_MIN = -2 ** 31
KEY_NEG_INF = -2139095041
NEG_BIG = -1e30


def _mxu(a):
    return a.astype(MXU_DTYPE)


def _dot(a, b):
    return jnp.dot(_mxu(a), _mxu(b), preferred_element_type=F32)


def _dot_nt(a, b):
    return lax.dot_general(_mxu(a), _mxu(b), (((1,), (1,)), ((), ())), preferred_element_type=F32)


def _dot_tn(a, b):
    return lax.dot_general(_mxu(a), _mxu(b), (((0,), (0,)), ((), ())), preferred_element_type=F32)


def _dot_f32(a, b):
    return jnp.dot(a, b, precision=lax.Precision.HIGHEST, preferred_element_type=F32)


def _sigmoid(x):
    return 1.0 / (1.0 + jnp.exp(-x))


def _log_sigmoid(x):
    return jnp.minimum(x, 0.0) - jnp.log1p(jnp.exp(-jnp.abs(x)))


def _params(*sem):
    return pltpu.CompilerParams(dimension_semantics=sem, vmem_limit_bytes=VMEM_LIMIT)


BF16_SUBLANES = 16


def _mixer_call(kernel_fn, *, grid, in_specs, out_spec, out_shape, scratch_shapes, name, operands, cast=None):
    params = _params("parallel", "arbitrary")
    if cast is None:
        return pl.pallas_call(kernel_fn, grid=grid, in_specs=in_specs, out_specs=out_spec, out_shape=out_shape,
                              scratch_shapes=scratch_shapes, compiler_params=params, name=name)(*operands)
    w, layer, tick, ticks, every = cast
    params = _params("arbitrary", "arbitrary")
    rows, cols = w.shape[1:]
    blocks = ticks // every
    while rows % blocks or (rows // blocks) % BF16_SUBLANES:
        blocks -= 1
    block = lambda a, b: jnp.minimum(tick(a, b) // every, blocks - 1)
    n_in = len(in_specs)

    def body(*refs):
        w_ref, w_out = refs[n_in], refs[n_in + 2]
        now = tick(pl.program_id(0), pl.program_id(1))

        @pl.when((now % every == 0) & (now // every < blocks))
        def _():
            w_out[...] = w_ref[...].astype(w_out.dtype)

        kernel_fn(*refs[:n_in], refs[n_in + 1], *refs[n_in + 3:])

    return pl.pallas_call(
        body, grid=grid,
        in_specs=in_specs + [pl.BlockSpec((None, rows // blocks, cols), lambda a, b: (layer, block(a, b), 0))],
        out_specs=(out_spec, pl.BlockSpec((rows // blocks, cols), lambda a, b: (block(a, b), 0))),
        out_shape=(out_shape, jax.ShapeDtypeStruct((rows, cols), MXU_DTYPE)),
        scratch_shapes=scratch_shapes, compiler_params=params, name=name)(*operands, w)


def _norm_matmul_kernel(x_ref, g_ref, w_ref, o_ref, h_ref, *, act):
    @pl.when(pl.program_id(1) == 0)
    def _():
        x = x_ref[...]
        ms = jnp.mean(x * x, axis=-1, keepdims=True)
        h_ref[...] = (x * lax.rsqrt(ms + EPS) * g_ref[...]).astype(h_ref.dtype)

    y = jnp.dot(h_ref[...], w_ref[...], preferred_element_type=F32)
    if act:
        y = jnp.square(jnp.maximum(y, 0.0))
    o_ref[...] = y.astype(o_ref.dtype)


def _norm_matmul(x, g, w, layer, *, tm, tn, act, out_dtype):
    m, k = x.shape
    n = w.shape[2]
    return pl.pallas_call(
        functools.partial(_norm_matmul_kernel, act=act),
        grid=(m // tm, n // tn),
        in_specs=[pl.BlockSpec((tm, k), lambda i, j: (i, 0)),
                  pl.BlockSpec((1, k), lambda i, j: (0, 0)),
                  pl.BlockSpec((None, k, tn), lambda i, j: (layer, 0, j))],
        out_specs=pl.BlockSpec((tm, tn), lambda i, j: (i, j)),
        out_shape=jax.ShapeDtypeStruct((m, n), out_dtype),
        scratch_shapes=[pltpu.VMEM((tm, k), MXU_DTYPE)],
        compiler_params=_params("parallel", "arbitrary"),
        name="norm_matmul",
    )(x, g.reshape(1, k), w)


def _matmul_res_kernel(a_ref, w_ref, r_ref, o_ref):
    part = jnp.dot(a_ref[...], w_ref[...], preferred_element_type=F32)

    @pl.when(pl.program_id(2) == 0)
    def _():
        o_ref[...] = r_ref[...] + part

    @pl.when(pl.program_id(2) != 0)
    def _():
        o_ref[...] += part


def _matmul_res(a, w, layer, res, *, tm, tn, tk):
    m, k = a.shape
    n = w.shape[2]
    return pl.pallas_call(
        _matmul_res_kernel,
        grid=(m // tm, n // tn, k // tk),
        in_specs=[pl.BlockSpec((tm, tk), lambda i, j, kk: (i, kk)),
                  pl.BlockSpec((None, tk, tn), lambda i, j, kk: (layer, kk, j)),
                  pl.BlockSpec((tm, tn), lambda i, j, kk: (i, j))],
        out_specs=pl.BlockSpec((tm, tn), lambda i, j, kk: (i, j)),
        out_shape=jax.ShapeDtypeStruct((m, n), F32),
        compiler_params=_params("parallel", "parallel", "arbitrary"),
        name="matmul_res",
    )(a, w, res)


def _mix_out_kernel(ya_ref, yb_ref, yc_ref, w_ref, r_ref, o_ref):
    acc = r_ref[...]
    acc += jnp.dot(ya_ref[...], w_ref[0:ML_V_W, :], preferred_element_type=F32)
    acc += jnp.dot(yb_ref[...], w_ref[ML_V_W:ML_V_W + DSA_W, :], preferred_element_type=F32)
    acc += jnp.dot(yc_ref[...], w_ref[ML_V_W + DSA_W:D_MIX, :], preferred_element_type=F32)
    o_ref[...] = acc


def _mix_out(ya, yb, yc, w, layer, res, *, tm, tn):
    m = ya.shape[0]
    n = w.shape[2]
    return pl.pallas_call(
        _mix_out_kernel,
        grid=(m // tm, n // tn),
        in_specs=[pl.BlockSpec((tm, ML_V_W), lambda i, j: (i, 0)),
                  pl.BlockSpec((tm, DSA_W), lambda i, j: (i, 0)),
                  pl.BlockSpec((tm, HG_V_W), lambda i, j: (i, 0)),
                  pl.BlockSpec((None, D_MIX, tn), lambda i, j: (layer, 0, j)),
                  pl.BlockSpec((tm, tn), lambda i, j: (i, j))],
        out_specs=pl.BlockSpec((tm, tn), lambda i, j: (i, j)),
        out_shape=jax.ShapeDtypeStruct((m, n), F32),
        compiler_params=_params("parallel", "parallel"),
        name="mix_out",
    )(ya, yb, yc, w, res)


def _final_norm_kernel(x_ref, g_ref, o_ref):
    x = x_ref[...]
    ms = jnp.mean(x * x, axis=-1, keepdims=True)
    o_ref[...] = x * lax.rsqrt(ms + EPS) * g_ref[...]


def _final_norm(x, g, *, tm):
    m, k = x.shape
    return pl.pallas_call(
        _final_norm_kernel,
        grid=(m // tm,),
        in_specs=[pl.BlockSpec((tm, k), lambda i: (i, 0)),
                  pl.BlockSpec((1, k), lambda i: (0, 0))],
        out_specs=pl.BlockSpec((tm, k), lambda i: (i, 0)),
        out_shape=jax.ShapeDtypeStruct((m, k), F32),
        compiler_params=_params("parallel"),
        name="final_norm",
    )(x, g.reshape(1, k))


def _mlstm_kernel(q_ref, k_ref, v_ref, o_ref, sm_ref, cw_ref, gb_ref, ng_ref, out_ref,
                  xbuf, c_ref, *, rows):
    @pl.when(pl.program_id(1) == 0)
    def _():
        xbuf[0:8, :] = jnp.zeros((8, 2 * ML_QK_W), F32)
        c_ref[...] = jnp.zeros_like(c_ref)

    xbuf[8:8 + rows, 0:ML_QK_W] = q_ref[...]
    xbuf[8:8 + rows, ML_QK_W:] = k_ref[...]
    cw = cw_ref[...]
    acc = xbuf[8:8 + rows, :] * cw[ML_CONV - 1:ML_CONV, :]
    for j in range(1, ML_CONV):
        acc += xbuf[8 - j:8 - j + rows, :] * cw[ML_CONV - 1 - j:ML_CONV - j, :]
    xbuf[0:8, :] = xbuf[rows:rows + 8, :]
    qk = acc * _sigmoid(acc)
    qk = jnp.concatenate([qk, jnp.zeros((rows, LANES), F32)], axis=1)

    capped = ML_GATE_CAP * jnp.tanh((sm_ref[...] + gb_ref[...]) * (1.0 / ML_GATE_CAP))
    lsig = _log_sigmoid(capped)
    r_i = lax.broadcasted_iota(jnp.int32, (rows, rows), 0)
    c_i = lax.broadcasted_iota(jnp.int32, (rows, rows), 1)
    causal = c_i <= r_i
    b_all = _dot_f32(jnp.where(causal, 1.0, 0.0), lsig)
    capped_t, b_all_t = capped.T, b_all.T
    one_col = jnp.where(lax.broadcasted_iota(jnp.int32, (rows, LANES), 1) == 0, 1.0, 0.0)
    real = lax.broadcasted_iota(jnp.int32, (rows, ML_DQK_PAD), 1) < ML_DQK

    for h in range(ML_HEADS):
        q = jnp.where(real, qk[:, h * ML_DQK:h * ML_DQK + ML_DQK_PAD], 0.0)
        k = jnp.where(real, qk[:, ML_QK_W + h * ML_DQK:ML_QK_W + h * ML_DQK + ML_DQK_PAD], 0.0) * (ML_DQK ** -0.5)
        ig_col, b_col = capped[:, h:h + 1], b_all[:, ML_HEADS + h:ML_HEADS + h + 1]
        ig_row, b_row = capped_t[h:h + 1, :], b_all_t[ML_HEADS + h:ML_HEADS + h + 1, :]
        dv = slice(h * ML_DV, (h + 1) * ML_DV)

        dmat = jnp.exp(jnp.where(causal, b_col - b_row + ig_row, -jnp.inf))
        s = _dot_nt(q, k) * dmat
        v_ext = jnp.concatenate([v_ref[:, dv], one_col], axis=1)
        c_old = c_ref[h]
        num_ext = _dot(s, v_ext) + jnp.exp(b_col) * _dot(q, c_old)
        num = num_ext[:, 0:ML_DV]
        den = num_ext[:, ML_DV:ML_DV + 1]
        hh = num / jnp.maximum(jnp.abs(den), 1.0)

        b_last = b_col[rows - 1:rows, :]
        w_s = jnp.exp(b_last - b_col + ig_col)
        c_ref[h] = jnp.exp(b_last) * c_old + _dot_tn(k, w_s * v_ext)

        ms = jnp.mean(hh * hh, axis=-1, keepdims=True)
        y = hh * lax.rsqrt(ms + EPS) * ng_ref[:, dv]
        out_ref[:, dv] = (_sigmoid(o_ref[:, dv]) * y).astype(out_ref.dtype)


def _mlstm(u, conv_w, gate_b, norm_g, *, rows, cast=None):
    t = u.shape[0]
    return _mixer_call(
        functools.partial(_mlstm_kernel, rows=rows),
        grid=(1, t // rows),
        in_specs=[pl.BlockSpec((rows, ML_QK_W), lambda _, c: (c, OFF_ML_Q // ML_QK_W)),
                  pl.BlockSpec((rows, ML_QK_W), lambda _, c: (c, OFF_ML_K // ML_QK_W)),
                  pl.BlockSpec((rows, ML_V_W), lambda _, c: (c, OFF_ML_V // ML_V_W)),
                  pl.BlockSpec((rows, ML_V_W), lambda _, c: (c, OFF_ML_O // ML_V_W)),
                  pl.BlockSpec((rows, LANES), lambda _, c: (c, OFF_SMALL // LANES)),
                  pl.BlockSpec((ML_CONV, 2 * ML_QK_W), lambda _, c: (0, 0)),
                  pl.BlockSpec((1, LANES), lambda _, c: (0, 0)),
                  pl.BlockSpec((1, ML_V_W), lambda _, c: (0, 0))],
        out_spec=pl.BlockSpec((rows, ML_V_W), lambda _, c: (c, 0)),
        out_shape=jax.ShapeDtypeStruct((t, ML_V_W), MXU_DTYPE),
        scratch_shapes=[pltpu.VMEM((rows + 8, 2 * ML_QK_W), F32),
                        pltpu.VMEM((ML_HEADS, ML_DQK_PAD, ML_DV + LANES), F32)],
        name="mlstm",
        operands=(u, u, u, u, u, conv_w, gate_b, norm_g.reshape(1, ML_V_W)),
        cast=cast and (*cast, lambda _, c: c, t // rows, 1))


SUBLANES = 8
HG_CAST_EVERY = 1
HG_GROUP = 6


def _pair_level_table(rows):
    t = np.arange(rows)[:, None]
    s = np.arange(rows)[None, :]
    x = np.bitwise_xor(t, s)
    lvl = np.floor(np.log2(np.maximum(x, 1))).astype(np.int32)
    return jnp.asarray(np.where(t > s, lvl, -1).astype(np.int32))


def _block_sums(log_f, rows):
    sub = lax.broadcasted_iota(jnp.int32, log_f.shape, 0)
    groups = rows // SUBLANES

    def row_of_group(x, j):
        x3 = x.reshape(groups, SUBLANES, x.shape[-1])
        return jnp.broadcast_to(x3[:, j:j + 1, :], x3.shape).reshape(x.shape)

    c = log_f
    e = jnp.zeros_like(log_f)
    out = [(c, e)]
    odd = (sub & 1) == 1
    c, e = (c + jnp.where(odd, pltpu.roll(c, 1, 0), 0.0),
            e + jnp.where(odd, 0.0, pltpu.roll(c, rows - 1, 0)))
    out.append((c, e))
    r8 = sub & 7
    c, e = (c + jnp.where((r8 == 2) | (r8 == 3), row_of_group(c, 1),
                          jnp.where((r8 == 6) | (r8 == 7), row_of_group(c, 5), 0.0)),
            e + jnp.where((r8 == 0) | (r8 == 1), row_of_group(c, 3),
                          jnp.where((r8 == 4) | (r8 == 5), row_of_group(c, 7), 0.0)))
    out.append((c, e))
    c, e = (c + jnp.where(r8 >= 4, row_of_group(c, 3), 0.0),
            e + jnp.where(r8 < 4, row_of_group(c, 7), 0.0))
    out.append((c, e))
    m = SUBLANES
    while m < rows:
        cs, es = [], []
        for p in range(rows // (2 * m)):
            lo, mid, hi = 2 * m * p, 2 * m * p + m, 2 * m * (p + 1)
            cs += [c[lo:mid], c[mid:hi] + c[mid - 1:mid]]
            es += [e[lo:mid] + c[hi - 1:hi], e[mid:hi]]
        c, e = jnp.concatenate(cs, axis=0), jnp.concatenate(es, axis=0)
        out.append((c, e))
        m *= 2
    return out


def _hgrn2_kernel(q_ref, f_ref, i_ref, g_ref, lb_ref, ng_ref, lvl_ref, out_ref, st_ref, *, rows):
    @pl.when(pl.program_id(1) == 0)
    def _():
        st_ref[...] = jnp.zeros_like(st_ref)

    for h in range(HG_GROUP):
        dk = slice(h * HG_DK, (h + 1) * HG_DK)
        out, st_ref[h] = _hgrn2_head(q_ref[:, dk], f_ref[:, dk], i_ref[:, dk], g_ref[:, dk], lb_ref[:, dk],
                                     ng_ref[:, dk], lvl_ref[...], st_ref[h], rows)
        out_ref[:, dk] = out.astype(out_ref.dtype)


def _hgrn2_head(qp, fp, v, gp, lb, ng, lvl, st, rows):
    q = qp * _sigmoid(qp)
    log_lb = lb[0:1, :]
    x2 = lb[1:2, :] + _log_sigmoid(fp)
    log_f = jnp.maximum(log_lb, x2) + jnp.log1p(jnp.exp(-jnp.abs(log_lb - x2)))
    k = lb[2:3, :] * _sigmoid(-fp)

    sums = _block_sums(log_f * LOG2E, rows)
    tiles = rows // LANES
    tile_rows = [slice(r * LANES, (r + 1) * LANES) for r in range(tiles)]
    diag = [jnp.zeros((LANES, LANES), F32) for _ in range(tiles)]
    o_tiles = [None] * tiles
    for level, (c_m, e_m) in enumerate(sums[:-1]):
        m = 1 << level
        qh = q * jnp.exp2(c_m)
        kh = k * jnp.exp2(e_m)
        if m < LANES:
            for r, sl in enumerate(tile_rows):
                diag[r] = jnp.where(lvl == level, _dot_nt(qh[sl], kh[sl]), diag[r])
        else:
            for p in range(rows // (2 * m)):
                lo, mid = 2 * m * p, 2 * m * p + m
                for r in range(mid // LANES, (mid + m) // LANES):
                    part = _dot(_dot_nt(qh[tile_rows[r]], kh[lo:mid]), v[lo:mid])
                    o_tiles[r] = part if o_tiles[r] is None else o_tiles[r] + part
    for r, sl in enumerate(tile_rows):
        part = _dot(diag[r], v[sl])
        o_tiles[r] = part if o_tiles[r] is None else o_tiles[r] + part
    b, after = sums[-1]

    o = (jnp.concatenate(o_tiles, axis=0) + jnp.sum(q * k, axis=-1, keepdims=True) * v
         + _dot_nt(q * jnp.exp2(b), st))
    st_new = st * jnp.exp2(b[rows - 1:rows, :]) + _dot_tn(v, k * jnp.exp2(after))

    ms = jnp.mean(o * o, axis=-1, keepdims=True)
    y = o * lax.rsqrt(ms + EPS) * ng
    return y * (gp * _sigmoid(gp)), st_new


def _hgrn2(u, lb_tab, norm_g, *, rows, cast=None):
    t = u.shape[0]
    width = HG_GROUP * HG_DK
    groups = HG_HEADS // HG_GROUP
    qb, fb, ib, gb = OFF_HG_Q // width, OFF_HG_F // width, OFF_HG_I // width, OFF_HG_G // width
    return _mixer_call(
        functools.partial(_hgrn2_kernel, rows=rows),
        grid=(groups, t // rows),
        in_specs=[pl.BlockSpec((rows, width), lambda h, c: (c, qb + h)),
                  pl.BlockSpec((rows, width), lambda h, c: (c, fb + h)),
                  pl.BlockSpec((rows, width), lambda h, c: (c, ib + h)),
                  pl.BlockSpec((rows, width), lambda h, c: (c, gb + h)),
                  pl.BlockSpec((3, width), lambda h, c: (0, h)),
                  pl.BlockSpec((1, width), lambda h, c: (0, h)),
                  pl.BlockSpec((LANES, LANES), lambda h, c: (0, 0))],
        out_spec=pl.BlockSpec((rows, width), lambda h, c: (c, h)),
        out_shape=jax.ShapeDtypeStruct((t, HG_V_W), MXU_DTYPE),
        scratch_shapes=[pltpu.VMEM((HG_GROUP, HG_DV, HG_DK), F32)],
        name="hgrn2",
        operands=(u, u, u, u, lb_tab, norm_g.reshape(1, HG_V_W), _pair_level_table(LANES)),
        cast=cast and (*cast, lambda h, c: h * (t // rows) + c, groups * (t // rows), HG_CAST_EVERY))


def _rope_slab(u, cc, sa, sb, half):
    return u * cc + pltpu.roll(u, LANES - half, 1) * sa + pltpu.roll(u, half, 1) * sb


def _rope_coeffs(ang, half):
    lane = lax.broadcasted_iota(jnp.int32, ang.shape, 1)
    cos, sin = jnp.cos(ang), jnp.sin(ang)
    cc = jnp.where(lane < 2 * half, cos, 1.0)
    sa = jnp.where(lane < half, -sin, 0.0)
    sb = jnp.where((lane >= half) & (lane < 2 * half), sin, 0.0)
    return cc, sa, sb


def _dsa_prep_kernel(cq_ref, k_ref, v_ref, idx_ref, pos_ref, g_ref, w_ref, fr_ref,
                     q_out, qi_out, k_out, v_out, ki_out, wi_out):
    pos = pos_ref[...].astype(F32)
    cq = cq_ref[...]
    ms = jnp.mean(cq * cq, axis=-1, keepdims=True)
    hq = cq * lax.rsqrt(ms + EPS) * g_ref[...]
    q_all = _dot(hq, w_ref[...])

    half_a = DSA_DH // ROPE_FRACTION // 2
    half_i = IDX_DH // ROPE_FRACTION // 2
    ca = _rope_coeffs(pos * fr_ref[0:1, :], half_a)
    ci = _rope_coeffs(pos * fr_ref[1:2, :], half_i)

    kk = k_ref[...]
    for h in range(DSA_HEADS):
        sl = slice(h * DSA_DH, (h + 1) * DSA_DH)
        q_out[:, sl] = (_rope_slab(q_all[:, sl], *ca, half_a) * Q_SCALE).astype(q_out.dtype)
        k_out[:, sl] = _rope_slab(kk[:, sl], *ca, half_a).astype(k_out.dtype)
    for h in range(IDX_HEADS):
        src = slice(DSA_W + h * IDX_SLOT, DSA_W + (h + 1) * IDX_SLOT)
        qi_out[h] = _rope_slab(q_all[:, src], *ci, half_i).astype(qi_out.dtype)
    v_out[...] = v_ref[...].astype(v_out.dtype)

    small = idx_ref[...]
    lane = lax.broadcasted_iota(jnp.int32, small.shape, 1)
    ki = jnp.where(lane < IDX_DH, pltpu.roll(small, LANES - SMALL_IDX_K, 1), 0.0)
    ki_out[...] = _rope_slab(ki, *ci, half_i).astype(ki_out.dtype)
    wi = pltpu.roll(small, LANES - SMALL_IDX_W, 1) * (IDX_HEADS ** -0.5 * IDX_DH ** -0.5)
    wi_out[...] = jnp.where(lane < IDX_HEADS, wi, 0.0)


def _dsa_prep(u, pos, q_norm_g, w_uq_pad, freqs, *, rows):
    t = u.shape[0]
    nq = DSA_W + IDX_HEADS * IDX_SLOT
    outs = (jax.ShapeDtypeStruct((t, DSA_W), MXU_DTYPE),
            jax.ShapeDtypeStruct((IDX_HEADS, t, IDX_SLOT), MXU_DTYPE),
            jax.ShapeDtypeStruct((t, DSA_W), MXU_DTYPE),
            jax.ShapeDtypeStruct((t, DSA_W), MXU_DTYPE),
            jax.ShapeDtypeStruct((t, IDX_SLOT), MXU_DTYPE),
            jax.ShapeDtypeStruct((t, LANES), F32))
    row_spec = lambda w, blk: pl.BlockSpec((rows, w), lambda i: (i, blk))
    return pl.pallas_call(
        _dsa_prep_kernel,
        grid=(t // rows,),
        in_specs=[row_spec(DSA_Q_RANK, OFF_DSA_CQ // DSA_Q_RANK),
                  row_spec(DSA_W, OFF_DSA_K // DSA_W),
                  row_spec(DSA_W, OFF_DSA_V // DSA_W),
                  row_spec(LANES, OFF_SMALL // LANES),
                  pl.BlockSpec((rows, 1), lambda i: (i, 0)),
                  pl.BlockSpec((1, DSA_Q_RANK), lambda i: (0, 0)),
                  pl.BlockSpec((DSA_Q_RANK, nq), lambda i: (0, 0)),
                  pl.BlockSpec((2, LANES), lambda i: (0, 0))],
        out_specs=(row_spec(DSA_W, 0),
                   pl.BlockSpec((IDX_HEADS, rows, IDX_SLOT), lambda i: (0, i, 0)),
                   row_spec(DSA_W, 0), row_spec(DSA_W, 0), row_spec(IDX_SLOT, 0), row_spec(LANES, 0)),
        out_shape=outs,
        compiler_params=_params("parallel"),
        name="dsa_prep",
    )(u, u, u, u, pos, q_norm_g.reshape(1, DSA_Q_RANK), w_uq_pad, freqs)


IDX_TQ = 512
IDX_TK = 512
IDX_SLAB = 128


def _indexer_kernel(qi_ref, wi_ref, ki_ref, incl_ref, bias_ref, key_ref, *, topk):
    tq, tk = IDX_TQ, IDX_TK
    qb = pl.program_id(0)
    nkb = ((qb + 1) * tq + tk - 1) // tk
    q_all = qi_ref[...].reshape(IDX_HEADS * tq, IDX_SLOT)
    w = wi_ref[...]
    w_col = jnp.concatenate([w[:, h:h + 1] for h in range(IDX_HEADS)], axis=0)
    row_chunk = (qb * tq + lax.broadcasted_iota(jnp.int32, (tq, 1), 0)) // CHUNK
    col_in_blk = lax.broadcasted_iota(jnp.int32, (1, tk), 1)

    def score_body(kb, carry):
        kt = ki_ref[pl.ds(pl.multiple_of(kb * tk, tk), tk), :]
        weighted = jnp.maximum(_dot_nt(q_all, kt), 0.0) * w_col
        parts = [weighted[h * tq:(h + 1) * tq] for h in range(IDX_HEADS)]
        while len(parts) > 1:
            parts = [a + b for a, b in zip(parts[0::2], parts[1::2])]
        col_chunk = (kb * tk + col_in_blk) // CHUNK
        score = jnp.where(col_chunk <= row_chunk, parts[0], -jnp.inf)
        bits = pltpu.bitcast(score, jnp.int32)
        bits = jnp.where(bits == INT_MIN, 0, bits)
        key_ref[kb] = jnp.where(bits < 0, bits ^ 0x7FFFFFFF, bits)
        return carry

    lax.fori_loop(0, nkb, score_body, 0)

    ones_mat = jnp.ones((LANES, LANES), MXU_DTYPE)

    def count_ge(cand):
        counts = []
        for r0 in range(0, tq, IDX_SLAB):
            cand_r = cand[r0:r0 + IDX_SLAB]

            def body(kb, cnt, r0=r0, cand_r=cand_r):
                for j in range(tk // LANES):
                    cnt += jnp.where(key_ref[kb, r0:r0 + IDX_SLAB, j * LANES:(j + 1) * LANES] >= cand_r, 1, 0)
                return cnt
            counts.append(lax.fori_loop(0, nkb, body, jnp.zeros((IDX_SLAB, LANES), jnp.int32)))
        cnt = jnp.concatenate(counts, axis=0)
        return jnp.dot(cnt.astype(F32).astype(MXU_DTYPE), ones_mat, preferred_element_type=F32)

    zero = jnp.zeros((tq, LANES), jnp.int32)
    cnt0 = count_ge(zero)
    thr_rep = jnp.where(cnt0 >= topk, zero, INT_MIN)
    at_thr = jnp.where(cnt0 >= topk, cnt0, (nkb * tk).astype(F32))

    def bit_body(i, carry):
        thr_rep, at_thr = carry
        cand = thr_rep + jnp.left_shift(jnp.int32(1), 30 - i)
        cnt = count_ge(cand)
        ok = cnt >= topk
        return jnp.where(ok, cand, thr_rep), jnp.where(ok, cnt, at_thr)

    thr_rep, at_thr = lax.fori_loop(0, 31, bit_body, (thr_rep, at_thr))
    thr = thr_rep[:, 0:1]
    exact = jnp.all(at_thr == topk)

    bias_ref[...] = jnp.full(bias_ref.shape, NEG_BIG, bias_ref.dtype)

    @pl.when(exact)
    def _():
        def emit_body(kb, carry):
            key = key_ref[kb]
            take = (key >= thr) & (key > KEY_NEG_INF)
            bias_ref[:, pl.ds(pl.multiple_of(kb * tk, tk), tk)] = jnp.where(take, 0.0, NEG_BIG).astype(bias_ref.dtype)
            return carry

        lax.fori_loop(0, nkb, emit_body, 0)

    @pl.when(jnp.logical_not(exact))
    def _():
        need = topk - count_ge(thr_rep + 1)[:, 0:1]

        def emit_body(kb, seen):
            key = key_ref[kb]
            eq = key == thr
            eq_f = jnp.where(eq, 1.0, 0.0)
            rank = seen + jnp.dot(eq_f.astype(MXU_DTYPE), incl_ref[...], preferred_element_type=F32)
            take = ((key > thr) | (eq & (rank <= need))) & (key > KEY_NEG_INF)
            bias_ref[:, pl.ds(pl.multiple_of(kb * tk, tk), tk)] = jnp.where(take, 0.0, NEG_BIG).astype(bias_ref.dtype)
            return seen + jnp.sum(eq_f, axis=1, keepdims=True)

        lax.fori_loop(0, nkb, emit_body, jnp.zeros((tq, 1), F32))


def _indexer(qi, wi, ki, *, topk):
    t = ki.shape[0]
    incl = jnp.asarray(np.triu(np.ones((IDX_TK, IDX_TK), np.float32)), MXU_DTYPE)
    return pl.pallas_call(
        functools.partial(_indexer_kernel, topk=topk),
        grid=(t // IDX_TQ,),
        in_specs=[pl.BlockSpec((IDX_HEADS, IDX_TQ, IDX_SLOT), lambda i: (0, i, 0)),
                  pl.BlockSpec((IDX_TQ, LANES), lambda i: (i, 0)),
                  pl.BlockSpec((t, IDX_SLOT), lambda i: (0, 0)),
                  pl.BlockSpec((IDX_TK, IDX_TK), lambda i: (0, 0))],
        out_specs=pl.BlockSpec((IDX_TQ, t), lambda i: (i, 0)),
        out_shape=jax.ShapeDtypeStruct((t, t), BF16),
        scratch_shapes=[pltpu.VMEM((t // IDX_TK, IDX_TQ, IDX_TK), jnp.int32)],
        compiler_params=_params("parallel"),
        name="dsa_indexer",
    )(qi, wi, ki, incl)


def _attn_kernel(q_ref, k_ref, v_ref, bias_ref, o_ref, m_ref, l_ref, acc_ref, *, tq, tk):
    qb, kb = pl.program_id(0), pl.program_id(1)
    last = ((qb + 1) * tq - 1) // tk

    @pl.when(kb == 0)
    def _():
        m_ref[...] = jnp.full(m_ref.shape, NEG_BIG, F32)
        l_ref[...] = jnp.zeros_like(l_ref)
        acc_ref[...] = jnp.zeros_like(acc_ref)

    @pl.when(kb <= last)
    def _():
        bias = bias_ref[...].astype(F32)
        ones = jnp.ones((tk, LANES), v_ref.dtype)
        heads = [slice(h * DSA_DH, (h + 1) * DSA_DH) for h in range(DSA_HEADS)]
        qk = _dot_nt(q_ref[:, heads[0]], k_ref[:, heads[0]])
        for h, sl in enumerate(heads):
            s = qk + bias
            if h + 1 < DSA_HEADS:
                qk = _dot_nt(q_ref[:, heads[h + 1]], k_ref[:, heads[h + 1]])
            m_old = m_ref[h]
            m_new = jnp.maximum(m_old, jnp.max(s, axis=-1, keepdims=True))
            alpha = jnp.exp2(m_old - m_new)
            p = jnp.exp2(s - jnp.concatenate([m_new] * (tk // LANES), axis=1))
            pv = _dot(p, jnp.concatenate([v_ref[:, sl], ones], axis=1))
            l_ref[h] = alpha * l_ref[h] + pv[:, DSA_DH:]
            acc_ref[:, sl] = alpha * acc_ref[:, sl] + pv[:, :DSA_DH]
            m_ref[h] = m_new

    @pl.when(kb == last)
    def _():
        for h in range(DSA_HEADS):
            sl = slice(h * DSA_DH, (h + 1) * DSA_DH)
            o_ref[:, sl] = (acc_ref[:, sl] / l_ref[h]).astype(o_ref.dtype)


def _attention(q, k, v, bias, *, tq, tk, cast=None):
    t = q.shape[0]
    last = lambda i: ((i + 1) * tq - 1) // tk
    ratio = tk // tq
    assert tk == ratio * tq

    def before(i):
        m = i // ratio
        return i + ratio * (m * (m - 1) // 2) + (i - ratio * m) * m

    tick = lambda i, j: before(i) + jnp.minimum(j, last(i))
    return _mixer_call(
        functools.partial(_attn_kernel, tq=tq, tk=tk),
        grid=(t // tq, t // tk),
        in_specs=[pl.BlockSpec((tq, DSA_W), lambda i, j: (i, 0)),
                  pl.BlockSpec((tk, DSA_W), lambda i, j: (jnp.minimum(j, last(i)), 0)),
                  pl.BlockSpec((tk, DSA_W), lambda i, j: (jnp.minimum(j, last(i)), 0)),
                  pl.BlockSpec((tq, tk), lambda i, j: (i, jnp.minimum(j, last(i))))],
        out_spec=pl.BlockSpec((tq, DSA_W), lambda i, j: (i, 0)),
        out_shape=jax.ShapeDtypeStruct((t, DSA_W), MXU_DTYPE),
        scratch_shapes=[pltpu.VMEM((DSA_HEADS, tq, LANES), F32),
                        pltpu.VMEM((DSA_HEADS, tq, LANES), F32),
                        pltpu.VMEM((tq, DSA_W), F32)],
        name="dsa_attention",
        operands=(q, k, v, bias),
        cast=cast and (*cast, tick, before(t // tq), 1))


def _pad_heads(w, heads, width, padded):
    lead = w.shape[:-1]
    w = w.reshape(lead + (heads, width))
    w = jnp.pad(w, [(0, 0)] * len(lead) + [(0, 0), (0, padded - width)])
    return w.reshape(lead + (heads * padded,))


def _pack_plan():
    src, start = {}, 0
    names = ("ml_q", "ml_k", "ml_v", "ml_i", "ml_f", "ml_o", "dsa_cq", "dsa_k", "dsa_v", "idx_k", "idx_w",
             "hg_q", "hg_f", "hg_i", "hg_g")
    for name, width in zip(names, IN_SPLITS):
        src[name] = start
        start += width
    copies = [(OFF_ML_Q, src["ml_q"], ML_QK_W), (OFF_ML_K, src["ml_k"], ML_QK_W),
              (OFF_ML_V, src["ml_v"], ML_V_W), (OFF_ML_O, src["ml_o"], ML_V_W),
              (OFF_HG_Q, src["hg_q"], HG_K_W), (OFF_HG_F, src["hg_f"], HG_K_W),
              (OFF_HG_I, src["hg_i"], HG_V_W), (OFF_HG_G, src["hg_g"], HG_V_W),
              (OFF_DSA_CQ, src["dsa_cq"], DSA_Q_RANK), (OFF_DSA_K, src["dsa_k"], DSA_W), (OFF_DSA_V, src["dsa_v"], DSA_W)]
    small = [(src["ml_i"], 2 * ML_HEADS), (src["idx_k"], IDX_DH + IDX_HEADS)]
    return copies, small


def _pack_kernel(wt_ref, o_ref):
    kt = o_ref.shape[0]
    copies, small = _pack_plan()
    for dst, src, width in copies:
        o_ref[:, dst:dst + width] = wt_ref[src:src + width, :].T.astype(o_ref.dtype)
    rows = [wt_ref[src:src + width, :] for src, width in small]
    used = sum(width for _, width in small)
    rows.append(jnp.zeros((LANES - used, kt), wt_ref.dtype))
    o_ref[:, OFF_SMALL:OFF_SMALL + LANES] = jnp.concatenate(rows, axis=0).T.astype(o_ref.dtype)


def _pack_w_in(w_in, *, kt):
    depth, d, n = w_in.shape
    wt = jnp.swapaxes(w_in, 1, 2)
    return pl.pallas_call(
        _pack_kernel,
        grid=(depth, d // kt),
        in_specs=[pl.BlockSpec((None, n, kt), lambda l, i: (l, 0, i))],
        out_specs=pl.BlockSpec((None, kt, D_IN_PAD), lambda l, i: (l, i, 0)),
        out_shape=jax.ShapeDtypeStruct((depth, d, D_IN_PAD), MXU_DTYPE),
        compiler_params=_params("parallel", "parallel"),
        name="pack_w_in",
    )(wt)


def _rope_freqs():
    def lanes(d):
        rot = d // ROPE_FRACTION
        half = rot // 2
        inv = jnp.power(ROPE_THETA, -jnp.arange(half, dtype=F32) * (2.0 / rot))
        return jnp.concatenate([inv, inv, jnp.zeros((LANES - rot,), F32)])
    return jnp.stack([lanes(DSA_DH), lanes(IDX_DH)])


def kernel(x, positions, ln_mix_g, w_in, ml_conv_w, ml_gate_b, ml_norm_g, dsa_q_norm_g, dsa_w_uq,
           hg_lb_logits, hg_norm_g, w_out, ln_mlp_g, w_up, w_down, ln_final_g):
    bsz, t, d = x.shape
    assert bsz == 1 and t % 512 == 0 and d == D_MODEL
    depth = w_in.shape[0]
    topk = min(TOPK_MAX, t // 4)
    xs = x.reshape(t, d)
    pos = positions.reshape(t, 1)

    w_in_p = _pack_w_in(w_in, kt=256)
    gate_b = jnp.pad(ml_gate_b.reshape(depth, 1, 2 * ML_HEADS), ((0, 0), (0, 0), (0, LANES - 2 * ML_HEADS)))
    w_uq_p = jnp.concatenate([dsa_w_uq[..., :DSA_W],
                              _pad_heads(dsa_w_uq[..., DSA_W:], IDX_HEADS, IDX_DH, IDX_SLOT)],
                             axis=-1).astype(MXU_DTYPE)
    freqs = _rope_freqs()
    lb_cum = jnp.cumsum(jax.nn.softmax(hg_lb_logits.astype(F32), axis=0), axis=0)
    lb = lb_cum - lb_cum[:1]
    lb_tab = jnp.stack([jnp.log(lb), jnp.log1p(-lb), 1.0 - lb], axis=1)

    for layer in range(depth):
        u = _norm_matmul(xs, ln_mix_g[layer], w_in_p, layer, tm=512, tn=1024, act=False, out_dtype=F32)
        y_a, w_out_b = _mlstm(u, ml_conv_w[layer], gate_b[layer], ml_norm_g[layer], rows=256, cast=(w_out, layer))
        q_r, qi_r, k_r, v_b, ki_r, wi = _dsa_prep(u, pos, dsa_q_norm_g[layer], w_uq_p[layer], freqs, rows=256)
        bias = _indexer(qi_r, wi, ki_r, topk=topk)
        y_b, w_down_b = _attention(q_r, k_r, v_b, bias, tq=512, tk=1024, cast=(w_down, layer))
        y_c, w_up_b = _hgrn2(u, lb_tab[layer], hg_norm_g[layer], rows=256, cast=(w_up, layer))
        xs = _mix_out(y_a, y_b, y_c, w_out_b[None], 0, xs, tm=512, tn=1024)
        a = _norm_matmul(xs, ln_mlp_g[layer], w_up_b[None], 0, tm=512, tn=1024, act=True, out_dtype=MXU_DTYPE)
        xs = _matmul_res(a, w_down_b[None], 0, xs, tm=1024, tn=1024, tk=2048)
    return _final_norm(xs, ln_final_g, tm=256).reshape(bsz, t, d)
```

```python
import functools

import jax
import jax.numpy as jnp
import numpy as np
from jax import lax
from jax.experimental import pallas as pl
from jax.experimental.pallas import tpu as pltpu

F32 = jnp.float32
BF16 = jnp.bfloat16
MXU_DTYPE = jnp.bfloat16

D_MODEL = 4096
DEPTH = 4
CHUNK = 64
EPS = 1e-6
ROPE_THETA = 500000.0
ROPE_FRACTION = 4
ML_HEADS = 4
ML_DV = 384
ML_DQK = ML_DV // 2
ML_DQK_PAD = 256
ML_CONV = 4
ML_GATE_CAP = 15.0
DSA_HEADS = 8
DSA_DH = 128
DSA_Q_RANK = 384
IDX_HEADS = 8
IDX_DH = 64
IDX_SLOT = 128
TOPK_MAX = 256
HG_HEADS = 12
HG_DK = 128
HG_DV = 128
D_FF = 4 * D_MODEL

ML_QK_W = ML_HEADS * ML_DQK
ML_V_W = ML_HEADS * ML_DV
DSA_W = DSA_HEADS * DSA_DH
IDX_W = IDX_HEADS * IDX_DH
HG_K_W = HG_HEADS * HG_DK
HG_V_W = HG_HEADS * HG_DV
D_MIX = ML_V_W + DSA_W + HG_V_W
IN_SPLITS = (ML_QK_W, ML_QK_W, ML_V_W, ML_HEADS, ML_HEADS, ML_V_W,
             DSA_Q_RANK, DSA_W, DSA_W, IDX_DH, IDX_HEADS,
             HG_K_W, HG_K_W, HG_V_W, HG_V_W)

LANES = 128
VMEM_LIMIT = 56 * 1024 * 1024

OFF_ML_Q = 0
OFF_ML_K = 768
OFF_ML_V = 1536
OFF_ML_O = 3072
OFF_HG_Q = 4608
OFF_HG_F = 6144
OFF_HG_I = 7680
OFF_HG_G = 9216
OFF_DSA_CQ = 10752
OFF_SMALL = 11136
OFF_DSA_K = 11264
OFF_DSA_V = 12288
D_IN_PAD = 13312
SMALL_GATES = 0
SMALL_IDX_K = 2 * ML_HEADS
SMALL_IDX_W = SMALL_IDX_K + IDX_DH

LOG2E = 1.4426950408889634
Q_SCALE = DSA_DH ** -0.5 * LOG2E

INT_MIN = -2 ** 31
KEY_NEG_INF = -2139095041
NEG_BIG = -1e30


def _mxu(a):
    return a.astype(MXU_DTYPE)


def _dot(a, b):
    return jnp.dot(_mxu(a), _mxu(b), preferred_element_type=F32)


def _dot_nt(a, b):
    return lax.dot_general(_mxu(a), _mxu(b), (((1,), (1,)), ((), ())), preferred_element_type=F32)


def _dot_tn(a, b):
    return lax.dot_general(_mxu(a), _mxu(b), (((0,), (0,)), ((), ())), preferred_element_type=F32)


def _dot_f32(a, b):
    return jnp.dot(a, b, precision=lax.Precision.HIGHEST, preferred_element_type=F32)


def _sigmoid(x):
    return 1.0 / (1.0 + jnp.exp(-x))


def _log_sigmoid(x):
    return jnp.minimum(x, 0.0) - jnp.log1p(jnp.exp(-jnp.abs(x)))


def _params(*sem):
    return pltpu.CompilerParams(dimension_semantics=sem, vmem_limit_bytes=VMEM_LIMIT)


BF16_SUBLANES = 16


def _mixer_call(kernel_fn, *, grid, in_specs, out_spec, out_shape, scratch_shapes, name, operands, cast=None):
    params = _params("parallel", "arbitrary")
    if cast is None:
        return pl.pallas_call(kernel_fn, grid=grid, in_specs=in_specs, out_specs=out_spec, out_shape=out_shape,
                              scratch_shapes=scratch_shapes, compiler_params=params, name=name)(*operands)
    w, layer, tick, ticks, every = cast
    params = _params("arbitrary", "arbitrary")
    rows, cols = w.shape[1:]
    blocks = ticks // every
    while rows % blocks or (rows // blocks) % BF16_SUBLANES:
        blocks -= 1
    block = lambda a, b: jnp.minimum(tick(a, b) // every, blocks - 1)
    n_in = len(in_specs)

    def body(*refs):
        w_ref, w_out = refs[n_in], refs[n_in + 2]
        now = tick(pl.program_id(0), pl.program_id(1))

        @pl.when((now % every == 0) & (now // every < blocks))
        def _():
            w_out[...] = w_ref[...].astype(w_out.dtype)

        kernel_fn(*refs[:n_in], refs[n_in + 1], *refs[n_in + 3:])

    return pl.pallas_call(
        body, grid=grid,
        in_specs=in_specs + [pl.BlockSpec((None, rows // blocks, cols), lambda a, b: (layer, block(a, b), 0))],
        out_specs=(out_spec, pl.BlockSpec((rows // blocks, cols), lambda a, b: (block(a, b), 0))),
        out_shape=(out_shape, jax.ShapeDtypeStruct((rows, cols), MXU_DTYPE)),
        scratch_shapes=scratch_shapes, compiler_params=params, name=name)(*operands, w)


def _norm_matmul_kernel(x_ref, g_ref, w_ref, o_ref, h_ref, *, act):
    @pl.when(pl.program_id(1) == 0)
    def _():
        x = x_ref[...]
        ms = jnp.mean(x * x, axis=-1, keepdims=True)
        h_ref[...] = (x * lax.rsqrt(ms + EPS) * g_ref[...]).astype(h_ref.dtype)

    y = jnp.dot(h_ref[...], w_ref[...], preferred_element_type=F32)
    if act:
        y = jnp.square(jnp.maximum(y, 0.0))
    o_ref[...] = y.astype(o_ref.dtype)


def _norm_matmul(x, g, w, layer, *, tm, tn, act, out_dtype):
    m, k = x.shape
    n = w.shape[2]
    return pl.pallas_call(
        functools.partial(_norm_matmul_kernel, act=act),
        grid=(m // tm, n // tn),
        in_specs=[pl.BlockSpec((tm, k), lambda i, j: (i, 0)),
                  pl.BlockSpec((1, k), lambda i, j: (0, 0)),
                  pl.BlockSpec((None, k, tn), lambda i, j: (layer, 0, j))],
        out_specs=pl.BlockSpec((tm, tn), lambda i, j: (i, j)),
        out_shape=jax.ShapeDtypeStruct((m, n), out_dtype),
        scratch_shapes=[pltpu.VMEM((tm, k), MXU_DTYPE)],
        compiler_params=_params("parallel", "arbitrary"),
        name="norm_matmul",
    )(x, g.reshape(1, k), w)


def _matmul_res_kernel(a_ref, w_ref, r_ref, o_ref):
    part = jnp.dot(a_ref[...], w_ref[...], preferred_element_type=F32)

    @pl.when(pl.program_id(2) == 0)
    def _():
        o_ref[...] = r_ref[...] + part

    @pl.when(pl.program_id(2) != 0)
    def _():
        o_ref[...] += part


def _matmul_res(a, w, layer, res, *, tm, tn, tk):
    m, k = a.shape
    n = w.shape[2]
    return pl.pallas_call(
        _matmul_res_kernel,
        grid=(m // tm, n // tn, k // tk),
        in_specs=[pl.BlockSpec((tm, tk), lambda i, j, kk: (i, kk)),
                  pl.BlockSpec((None, tk, tn), lambda i, j, kk: (layer, kk, j)),
                  pl.BlockSpec((tm, tn), lambda i, j, kk: (i, j))],
        out_specs=pl.BlockSpec((tm, tn), lambda i, j, kk: (i, j)),
        out_shape=jax.ShapeDtypeStruct((m, n), F32),
        compiler_params=_params("parallel", "parallel", "arbitrary"),
        name="matmul_res",
    )(a, w, res)


def _mix_out_kernel(ya_ref, yb_ref, yc_ref, w_ref, r_ref, o_ref):
    acc = r_ref[...]
    acc += jnp.dot(ya_ref[...], w_ref[0:ML_V_W, :], preferred_element_type=F32)
    acc += jnp.dot(yb_ref[...], w_ref[ML_V_W:ML_V_W + DSA_W, :], preferred_element_type=F32)
    acc += jnp.dot(yc_ref[...], w_ref[ML_V_W + DSA_W:D_MIX, :], preferred_element_type=F32)
    o_ref[...] = acc


def _mix_out(ya, yb, yc, w, layer, res, *, tm, tn):
    m = ya.shape[0]
    n = w.shape[2]
    return pl.pallas_call(
        _mix_out_kernel,
        grid=(m // tm, n // tn),
        in_specs=[pl.BlockSpec((tm, ML_V_W), lambda i, j: (i, 0)),
                  pl.BlockSpec((tm, DSA_W), lambda i, j: (i, 0)),
                  pl.BlockSpec((tm, HG_V_W), lambda i, j: (i, 0)),
                  pl.BlockSpec((None, D_MIX, tn), lambda i, j: (layer, 0, j)),
                  pl.BlockSpec((tm, tn), lambda i, j: (i, j))],
        out_specs=pl.BlockSpec((tm, tn), lambda i, j: (i, j)),
        out_shape=jax.ShapeDtypeStruct((m, n), F32),
        compiler_params=_params("parallel", "parallel"),
        name="mix_out",
    )(ya, yb, yc, w, res)


def _final_norm_kernel(x_ref, g_ref, o_ref):
    x = x_ref[...]
    ms = jnp.mean(x * x, axis=-1, keepdims=True)
    o_ref[...] = x * lax.rsqrt(ms + EPS) * g_ref[...]


def _final_norm(x, g, *, tm):
    m, k = x.shape
    return pl.pallas_call(
        _final_norm_kernel,
        grid=(m // tm,),
        in_specs=[pl.BlockSpec((tm, k), lambda i: (i, 0)),
                  pl.BlockSpec((1, k), lambda i: (0, 0))],
        out_specs=pl.BlockSpec((tm, k), lambda i: (i, 0)),
        out_shape=jax.ShapeDtypeStruct((m, k), F32),
        compiler_params=_params("parallel"),
        name="final_norm",
    )(x, g.reshape(1, k))


def _mlstm_kernel(q_ref, k_ref, v_ref, o_ref, sm_ref, cw_ref, gb_ref, ng_ref, out_ref,
                  xbuf, c_ref, *, rows):
    @pl.when(pl.program_id(1) == 0)
    def _():
        xbuf[0:8, :] = jnp.zeros((8, 2 * ML_QK_W), F32)
        c_ref[...] = jnp.zeros_like(c_ref)

    xbuf[8:8 + rows, 0:ML_QK_W] = q_ref[...]
    xbuf[8:8 + rows, ML_QK_W:] = k_ref[...]
    cw = cw_ref[...]
    acc = xbuf[8:8 + rows, :] * cw[ML_CONV - 1:ML_CONV, :]
    for j in range(1, ML_CONV):
        acc += xbuf[8 - j:8 - j + rows, :] * cw[ML_CONV - 1 - j:ML_CONV - j, :]
    xbuf[0:8, :] = xbuf[rows:rows + 8, :]
    qk = acc * _sigmoid(acc)
    qk = jnp.concatenate([qk, jnp.zeros((rows, LANES), F32)], axis=1)

    capped = ML_GATE_CAP * jnp.tanh((sm_ref[...] + gb_ref[...]) * (1.0 / ML_GATE_CAP))
    lsig = _log_sigmoid(capped)
    r_i = lax.broadcasted_iota(jnp.int32, (rows, rows), 0)
    c_i = lax.broadcasted_iota(jnp.int32, (rows, rows), 1)
    causal = c_i <= r_i
    b_all = _dot_f32(jnp.where(causal, 1.0, 0.0), lsig)
    capped_t, b_all_t = capped.T, b_all.T
    one_col = jnp.where(lax.broadcasted_iota(jnp.int32, (rows, LANES), 1) == 0, 1.0, 0.0)
    real = lax.broadcasted_iota(jnp.int32, (rows, ML_DQK_PAD), 1) < ML_DQK

    for h in range(ML_HEADS):
        q = jnp.where(real, qk[:, h * ML_DQK:h * ML_DQK + ML_DQK_PAD], 0.0)
        k = jnp.where(real, qk[:, ML_QK_W + h * ML_DQK:ML_QK_W + h * ML_DQK + ML_DQK_PAD], 0.0) * (ML_DQK ** -0.5)
        ig_col, b_col = capped[:, h:h + 1], b_all[:, ML_HEADS + h:ML_HEADS + h + 1]
        ig_row, b_row = capped_t[h:h + 1, :], b_all_t[ML_HEADS + h:ML_HEADS + h + 1, :]
        dv = slice(h * ML_DV, (h + 1) * ML_DV)

        dmat = jnp.exp(jnp.where(causal, b_col - b_row + ig_row, -jnp.inf))
        s = _dot_nt(q, k) * dmat
        v_ext = jnp.concatenate([v_ref[:, dv], one_col], axis=1)
        c_old = c_ref[h]
        num_ext = _dot(s, v_ext) + jnp.exp(b_col) * _dot(q, c_old)
        num = num_ext[:, 0:ML_DV]
        den = num_ext[:, ML_DV:ML_DV + 1]
        hh = num / jnp.maximum(jnp.abs(den), 1.0)

        b_last = b_col[rows - 1:rows, :]
        w_s = jnp.exp(b_last - b_col + ig_col)
        c_ref[h] = jnp.exp(b_last) * c_old + _dot_tn(k, w_s * v_ext)

        ms = jnp.mean(hh * hh, axis=-1, keepdims=True)
        y = hh * lax.rsqrt(ms + EPS) * ng_ref[:, dv]
        out_ref[:, dv] = (_sigmoid(o_ref[:, dv]) * y).astype(out_ref.dtype)


def _mlstm(u, conv_w, gate_b, norm_g, *, rows, cast=None):
    t = u.shape[0]
    return _mixer_call(
        functools.partial(_mlstm_kernel, rows=rows),
        grid=(1, t // rows),
        in_specs=[pl.BlockSpec((rows, ML_QK_W), lambda _, c: (c, OFF_ML_Q // ML_QK_W)),
                  pl.BlockSpec((rows, ML_QK_W), lambda _, c: (c, OFF_ML_K // ML_QK_W)),
                  pl.BlockSpec((rows, ML_V_W), lambda _, c: (c, OFF_ML_V // ML_V_W)),
                  pl.BlockSpec((rows, ML_V_W), lambda _, c: (c, OFF_ML_O // ML_V_W)),
                  pl.BlockSpec((rows, LANES), lambda _, c: (c, OFF_SMALL // LANES)),
                  pl.BlockSpec((ML_CONV, 2 * ML_QK_W), lambda _, c: (0, 0)),
                  pl.BlockSpec((1, LANES), lambda _, c: (0, 0)),
                  pl.BlockSpec((1, ML_V_W), lambda _, c: (0, 0))],
        out_spec=pl.BlockSpec((rows, ML_V_W), lambda _, c: (c, 0)),
        out_shape=jax.ShapeDtypeStruct((t, ML_V_W), MXU_DTYPE),
        scratch_shapes=[pltpu.VMEM((rows + 8, 2 * ML_QK_W), F32),
                        pltpu.VMEM((ML_HEADS, ML_DQK_PAD, ML_DV + LANES), F32)],
        name="mlstm",
        operands=(u, u, u, u, u, conv_w, gate_b, norm_g.reshape(1, ML_V_W)),
        cast=cast and (*cast, lambda _, c: c, t // rows, 1))


SUBLANES = 8
HG_CAST_EVERY = 1
HG_GROUP = 6


def _pair_level_table(rows):
    t = np.arange(rows)[:, None]
    s = np.arange(rows)[None, :]
    x = np.bitwise_xor(t, s)
    lvl = np.floor(np.log2(np.maximum(x, 1))).astype(np.int32)
    return jnp.asarray(np.where(t > s, lvl, -1).astype(np.int32))


def _block_sums(log_f, rows):
    sub = lax.broadcasted_iota(jnp.int32, log_f.shape, 0)
    groups = rows // SUBLANES

    def row_of_group(x, j):
        x3 = x.reshape(groups, SUBLANES, x.shape[-1])
        return jnp.broadcast_to(x3[:, j:j + 1, :], x3.shape).reshape(x.shape)

    c = log_f
    e = jnp.zeros_like(log_f)
    out = [(c, e)]
    odd = (sub & 1) == 1
    c, e = (c + jnp.where(odd, pltpu.roll(c, 1, 0), 0.0),
            e + jnp.where(odd, 0.0, pltpu.roll(c, rows - 1, 0)))
    out.append((c, e))
    r8 = sub & 7
    c, e = (c + jnp.where((r8 == 2) | (r8 == 3), row_of_group(c, 1),
                          jnp.where((r8 == 6) | (r8 == 7), row_of_group(c, 5), 0.0)),
            e + jnp.where((r8 == 0) | (r8 == 1), row_of_group(c, 3),
                          jnp.where((r8 == 4) | (r8 == 5), row_of_group(c, 7), 0.0)))
    out.append((c, e))
    c, e = (c + jnp.where(r8 >= 4, row_of_group(c, 3), 0.0),
            e + jnp.where(r8 < 4, row_of_group(c, 7), 0.0))
    out.append((c, e))
    m = SUBLANES
    while m < rows:
        cs, es = [], []
        for p in range(rows // (2 * m)):
            lo, mid, hi = 2 * m * p, 2 * m * p + m, 2 * m * (p + 1)
            cs += [c[lo:mid], c[mid:hi] + c[mid - 1:mid]]
            es += [e[lo:mid] + c[hi - 1:hi], e[mid:hi]]
        c, e = jnp.concatenate(cs, axis=0), jnp.concatenate(es, axis=0)
        out.append((c, e))
        m *= 2
    return out


def _hgrn2_kernel(q_ref, f_ref, i_ref, g_ref, lb_ref, ng_ref, lvl_ref, out_ref, st_ref, *, rows):
    @pl.when(pl.program_id(1) == 0)
    def _():
        st_ref[...] = jnp.zeros_like(st_ref)

    for h in range(HG_GROUP):
        dk = slice(h * HG_DK, (h + 1) * HG_DK)
        out, st_ref[h] = _hgrn2_head(q_ref[:, dk], f_ref[:, dk], i_ref[:, dk], g_ref[:, dk], lb_ref[:, dk],
                                     ng_ref[:, dk], lvl_ref[...], st_ref[h], rows)
        out_ref[:, dk] = out.astype(out_ref.dtype)


def _hgrn2_head(qp, fp, v, gp, lb, ng, lvl, st, rows):
    q = qp * _sigmoid(qp)
    log_lb = lb[0:1, :]
    x2 = lb[1:2, :] + _log_sigmoid(fp)
    log_f = jnp.maximum(log_lb, x2) + jnp.log1p(jnp.exp(-jnp.abs(log_lb - x2)))
    k = lb[2:3, :] * _sigmoid(-fp)

    sums = _block_sums(log_f * LOG2E, rows)
    tiles = rows // LANES
    tile_rows = [slice(r * LANES, (r + 1) * LANES) for r in range(tiles)]
    diag = [jnp.zeros((LANES, LANES), F32) for _ in range(tiles)]
    o_tiles = [None] * tiles
    for level, (c_m, e_m) in enumerate(sums[:-1]):
        m = 1 << level
        qh = q * jnp.exp2(c_m)
        kh = k * jnp.exp2(e_m)
        if m < LANES:
            for r, sl in enumerate(tile_rows):
                diag[r] = jnp.where(lvl == level, _dot_nt(qh[sl], kh[sl]), diag[r])
        else:
            for p in range(rows // (2 * m)):
                lo, mid = 2 * m * p, 2 * m * p + m
                for r in range(mid // LANES, (mid + m) // LANES):
                    part = _dot(_dot_nt(qh[tile_rows[r]], kh[lo:mid]), v[lo:mid])
                    o_tiles[r] = part if o_tiles[r] is None else o_tiles[r] + part
    for r, sl in enumerate(tile_rows):
        part = _dot(diag[r], v[sl])
        o_tiles[r] = part if o_tiles[r] is None else o_tiles[r] + part
    b, after = sums[-1]

    o = (jnp.concatenate(o_tiles, axis=0) + jnp.sum(q * k, axis=-1, keepdims=True) * v
         + _dot_nt(q * jnp.exp2(b), st))
    st_new = st * jnp.exp2(b[rows - 1:rows, :]) + _dot_tn(v, k * jnp.exp2(after))

    ms = jnp.mean(o * o, axis=-1, keepdims=True)
    y = o * lax.rsqrt(ms + EPS) * ng
    return y * (gp * _sigmoid(gp)), st_new


def _hgrn2(u, lb_tab, norm_g, *, rows, cast=None):
    t = u.shape[0]
    width = HG_GROUP * HG_DK
    groups = HG_HEADS // HG_GROUP
    qb, fb, ib, gb = OFF_HG_Q // width, OFF_HG_F // width, OFF_HG_I // width, OFF_HG_G // width
    return _mixer_call(
        functools.partial(_hgrn2_kernel, rows=rows),
        grid=(groups, t // rows),
        in_specs=[pl.BlockSpec((rows, width), lambda h, c: (c, qb + h)),
                  pl.BlockSpec((rows, width), lambda h, c: (c, fb + h)),
                  pl.BlockSpec((rows, width), lambda h, c: (c, ib + h)),
                  pl.BlockSpec((rows, width), lambda h, c: (c, gb + h)),
                  pl.BlockSpec((3, width), lambda h, c: (0, h)),
                  pl.BlockSpec((1, width), lambda h, c: (0, h)),
                  pl.BlockSpec((LANES, LANES), lambda h, c: (0, 0))],
        out_spec=pl.BlockSpec((rows, width), lambda h, c: (c, h)),
        out_shape=jax.ShapeDtypeStruct((t, HG_V_W), MXU_DTYPE),
        scratch_shapes=[pltpu.VMEM((HG_GROUP, HG_DV, HG_DK), F32)],
        name="hgrn2",
        operands=(u, u, u, u, lb_tab, norm_g.reshape(1, HG_V_W), _pair_level_table(LANES)),
        cast=cast and (*cast, lambda h, c: h * (t // rows) + c, groups * (t // rows), HG_CAST_EVERY))


def _rope_slab(u, cc, sa, sb, half):
    return u * cc + pltpu.roll(u, LANES - half, 1) * sa + pltpu.roll(u, half, 1) * sb


def _rope_coeffs(ang, half):
    lane = lax.broadcasted_iota(jnp.int32, ang.shape, 1)
    cos, sin = jnp.cos(ang), jnp.sin(ang)
    cc = jnp.where(lane < 2 * half, cos, 1.0)
    sa = jnp.where(lane < half, -sin, 0.0)
    sb = jnp.where((lane >= half) & (lane < 2 * half), sin, 0.0)
    return cc, sa, sb


def _dsa_prep_kernel(cq_ref, k_ref, v_ref, idx_ref, pos_ref, g_ref, w_ref, fr_ref,
                     q_out, qi_out, k_out, v_out, ki_out, wi_out):
    pos = pos_ref[...].astype(F32)
    cq = cq_ref[...]
    ms = jnp.mean(cq * cq, axis=-1, keepdims=True)
    hq = cq * lax.rsqrt(ms + EPS) * g_ref[...]
    q_all = _dot(hq, w_ref[...])

    half_a = DSA_DH // ROPE_FRACTION // 2
    half_i = IDX_DH // ROPE_FRACTION // 2
    ca = _rope_coeffs(pos * fr_ref[0:1, :], half_a)
    ci = _rope_coeffs(pos * fr_ref[1:2, :], half_i)

    kk = k_ref[...]
    for h in range(DSA_HEADS):
        sl = slice(h * DSA_DH, (h + 1) * DSA_DH)
        q_out[:, sl] = (_rope_slab(q_all[:, sl], *ca, half_a) * Q_SCALE).astype(q_out.dtype)
        k_out[:, sl] = _rope_slab(kk[:, sl], *ca, half_a).astype(k_out.dtype)
    for h in range(IDX_HEADS):
        src = slice(DSA_W + h * IDX_SLOT, DSA_W + (h + 1) * IDX_SLOT)
        qi_out[h] = _rope_slab(q_all[:, src], *ci, half_i).astype(qi_out.dtype)
    v_out[...] = v_ref[...].astype(v_out.dtype)

    small = idx_ref[...]
    lane = lax.broadcasted_iota(jnp.int32, small.shape, 1)
    ki = jnp.where(lane < IDX_DH, pltpu.roll(small, LANES - SMALL_IDX_K, 1), 0.0)
    ki_out[...] = _rope_slab(ki, *ci, half_i).astype(ki_out.dtype)
    wi = pltpu.roll(small, LANES - SMALL_IDX_W, 1) * (IDX_HEADS ** -0.5 * IDX_DH ** -0.5)
    wi_out[...] = jnp.where(lane < IDX_HEADS, wi, 0.0)


def _dsa_prep(u, pos, q_norm_g, w_uq_pad, freqs, *, rows):
    t = u.shape[0]
    nq = DSA_W + IDX_HEADS * IDX_SLOT
    outs = (jax.ShapeDtypeStruct((t, DSA_W), MXU_DTYPE),
            jax.ShapeDtypeStruct((IDX_HEADS, t, IDX_SLOT), MXU_DTYPE),
            jax.ShapeDtypeStruct((t, DSA_W), MXU_DTYPE),
            jax.ShapeDtypeStruct((t, DSA_W), MXU_DTYPE),
            jax.ShapeDtypeStruct((t, IDX_SLOT), MXU_DTYPE),
            jax.ShapeDtypeStruct((t, LANES), F32))
    row_spec = lambda w, blk: pl.BlockSpec((rows, w), lambda i: (i, blk))
    return pl.pallas_call(
        _dsa_prep_kernel,
        grid=(t // rows,),
        in_specs=[row_spec(DSA_Q_RANK, OFF_DSA_CQ // DSA_Q_RANK),
                  row_spec(DSA_W, OFF_DSA_K // DSA_W),
                  row_spec(DSA_W, OFF_DSA_V // DSA_W),
                  row_spec(LANES, OFF_SMALL // LANES),
                  pl.BlockSpec((rows, 1), lambda i: (i, 0)),
                  pl.BlockSpec((1, DSA_Q_RANK), lambda i: (0, 0)),
                  pl.BlockSpec((DSA_Q_RANK, nq), lambda i: (0, 0)),
                  pl.BlockSpec((2, LANES), lambda i: (0, 0))],
        out_specs=(row_spec(DSA_W, 0),
                   pl.BlockSpec((IDX_HEADS, rows, IDX_SLOT), lambda i: (0, i, 0)),
                   row_spec(DSA_W, 0), row_spec(DSA_W, 0), row_spec(IDX_SLOT, 0), row_spec(LANES, 0)),
        out_shape=outs,
        compiler_params=_params("parallel"),
        name="dsa_prep",
    )(u, u, u, u, pos, q_norm_g.reshape(1, DSA_Q_RANK), w_uq_pad, freqs)


IDX_TQ = 512
IDX_TK = 512
IDX_SLAB = 128


def _indexer_kernel(qi_ref, wi_ref, ki_ref, incl_ref, bias_ref, key_ref, *, topk):
    tq, tk = IDX_TQ, IDX_TK
    qb = pl.program_id(0)
    nkb = ((qb + 1) * tq + tk - 1) // tk
    q_all = qi_ref[...].reshape(IDX_HEADS * tq, IDX_SLOT)
    w = wi_ref[...]
    w_col = jnp.concatenate([w[:, h:h + 1] for h in range(IDX_HEADS)], axis=0)
    row_chunk = (qb * tq + lax.broadcasted_iota(jnp.int32, (tq, 1), 0)) // CHUNK
    col_in_blk = lax.broadcasted_iota(jnp.int32, (1, tk), 1)

    def score_body(kb, carry):
        kt = ki_ref[pl.ds(pl.multiple_of(kb * tk, tk), tk), :]
        weighted = jnp.maximum(_dot_nt(q_all, kt), 0.0) * w_col
        parts = [weighted[h * tq:(h + 1) * tq] for h in range(IDX_HEADS)]
        while len(parts) > 1:
            parts = [a + b for a, b in zip(parts[0::2], parts[1::2])]
        col_chunk = (kb * tk + col_in_blk) // CHUNK
        score = jnp.where(col_chunk <= row_chunk, parts[0], -jnp.inf)
        bits = pltpu.bitcast(score, jnp.int32)
        bits = jnp.where(bits == INT_MIN, 0, bits)
        key_ref[kb] = jnp.where(bits < 0, bits ^ 0x7FFFFFFF, bits)
        return carry

    lax.fori_loop(0, nkb, score_body, 0)

    ones_mat = jnp.ones((LANES, LANES), MXU_DTYPE)

    slabs = [slice(r0, r0 + IDX_SLAB) for r0 in range(0, tq, IDX_SLAB)]

    def slab_counts(cand_r, rows):
        def one_block(kb, cnt):
            for j in range(tk // LANES):
                cnt += jnp.where(key_ref[kb, rows, j * LANES:(j + 1) * LANES] >= cand_r, 1, 0)
            return cnt

        cnt = lax.fori_loop(0, nkb // 2, lambda p, c: one_block(2 * p + 1, one_block(2 * p, c)),
                            jnp.zeros((IDX_SLAB, LANES), jnp.int32))
        return lax.cond(nkb % 2 == 1, lambda c: one_block(nkb - 1, c), lambda c: c, cnt)

    def row_totals(cnt):
        return jnp.dot(cnt.astype(F32).astype(MXU_DTYPE), ones_mat, preferred_element_type=F32)

    def count_ge(cand):
        return row_totals(jnp.concatenate([slab_counts(cand[rows], rows) for rows in slabs], axis=0))

    zero = jnp.zeros((tq, LANES), jnp.int32)
    cnt0 = count_ge(zero)
    thr_rep = jnp.where(cnt0 >= topk, zero, INT_MIN)
    at_thr = jnp.where(cnt0 >= topk, cnt0, (nkb * tk).astype(F32))

    def bit_body(i, carry):
        thr_rep, at_thr = carry
        cand = thr_rep + jnp.left_shift(jnp.int32(1), 30 - i)
        cnt = count_ge(cand)
        ok = cnt >= topk
        return jnp.where(ok, cand, thr_rep), jnp.where(ok, cnt, at_thr)

    thr_rep, at_thr = lax.fori_loop(0, 31, bit_body, (thr_rep, at_thr))

    bias_ref[...] = jnp.full(bias_ref.shape, NEG_BIG, bias_ref.dtype)

    thr = thr_rep[:, 0:1]
    exact = jnp.all(at_thr == topk)

    def store(kb, take):
        cols = pl.ds(pl.multiple_of(kb * tk, tk), tk)
        bias_ref[:, cols] = jnp.where(take, 0.0, NEG_BIG).astype(bias_ref.dtype)

    @pl.when(exact)
    def _():
        def emit_body(kb, carry):
            key = key_ref[kb]
            store(kb, (key >= thr) & (key > KEY_NEG_INF))
            return carry

        lax.fori_loop(0, nkb, emit_body, 0)

    @pl.when(jnp.logical_not(exact))
    def _():
        need = topk - count_ge(thr_rep + 1)[:, 0:1]

        def emit_body(kb, seen):
            key = key_ref[kb]
            eq = key == thr
            eq_f = jnp.where(eq, 1.0, 0.0)
            rank = seen + jnp.dot(eq_f.astype(MXU_DTYPE), incl_ref[...], preferred_element_type=F32)
            store(kb, ((key > thr) | (eq & (rank <= need))) & (key > KEY_NEG_INF))
            return seen + jnp.sum(eq_f, axis=1, keepdims=True)

        lax.fori_loop(0, nkb, emit_body, jnp.zeros((tq, 1), F32))


def _indexer(qi, wi, ki, *, topk):
    t = ki.shape[0]
    incl = jnp.asarray(np.triu(np.ones((IDX_TK, IDX_TK), np.float32)), MXU_DTYPE)
    return pl.pallas_call(
        functools.partial(_indexer_kernel, topk=topk),
        grid=(t // IDX_TQ,),
        in_specs=[pl.BlockSpec((IDX_HEADS, IDX_TQ, IDX_SLOT), lambda i: (0, i, 0)),
                  pl.BlockSpec((IDX_TQ, LANES), lambda i: (i, 0)),
                  pl.BlockSpec((t, IDX_SLOT), lambda i: (0, 0)),
                  pl.BlockSpec((IDX_TK, IDX_TK), lambda i: (0, 0))],
        out_specs=pl.BlockSpec((IDX_TQ, t), lambda i: (i, 0)),
        out_shape=jax.ShapeDtypeStruct((t, t), BF16),
        scratch_shapes=[pltpu.VMEM((t // IDX_TK, IDX_TQ, IDX_TK), jnp.int32)],
        compiler_params=_params("parallel"),
        name="dsa_indexer",
    )(qi, wi, ki, incl)


def _attn_kernel(q_ref, k_ref, v_ref, bias_ref, o_ref, m_ref, l_ref, acc_ref, *, tq, tk):
    qb, kb = pl.program_id(0), pl.program_id(1)
    last = ((qb + 1) * tq - 1) // tk

    @pl.when(kb == 0)
    def _():
        m_ref[...] = jnp.full(m_ref.shape, NEG_BIG, F32)
        l_ref[...] = jnp.zeros_like(l_ref)
        acc_ref[...] = jnp.zeros_like(acc_ref)

    @pl.when(kb <= last)
    def _():
        bias = bias_ref[...].astype(F32)
        ones = jnp.ones((tk, LANES), v_ref.dtype)
        heads = [slice(h * DSA_DH, (h + 1) * DSA_DH) for h in range(DSA_HEADS)]
        qk = _dot_nt(q_ref[:, heads[0]], k_ref[:, heads[0]])
        for h, sl in enumerate(heads):
            s = qk + bias
            if h + 1 < DSA_HEADS:
                qk = _dot_nt(q_ref[:, heads[h + 1]], k_ref[:, heads[h + 1]])
            m_old = m_ref[h]
            m_new = jnp.maximum(m_old, jnp.max(s, axis=-1, keepdims=True))
            alpha = jnp.exp2(m_old - m_new)
            p = jnp.exp2(s - jnp.concatenate([m_new] * (tk // LANES), axis=1))
            pv = _dot(p, jnp.concatenate([v_ref[:, sl], ones], axis=1))
            l_ref[h] = alpha * l_ref[h] + pv[:, DSA_DH:]
            acc_ref[:, sl] = alpha * acc_ref[:, sl] + pv[:, :DSA_DH]
            m_ref[h] = m_new

    @pl.when(kb == last)
    def _():
        for h in range(DSA_HEADS):
            sl = slice(h * DSA_DH, (h + 1) * DSA_DH)
            o_ref[:, sl] = (acc_ref[:, sl] / l_ref[h]).astype(o_ref.dtype)


def _attention(q, k, v, bias, *, tq, tk, cast=None):
    t = q.shape[0]
    last = lambda i: ((i + 1) * tq - 1) // tk
    ratio = tk // tq
    assert tk == ratio * tq

    def before(i):
        m = i // ratio
        return i + ratio * (m * (m - 1) // 2) + (i - ratio * m) * m

    tick = lambda i, j: before(i) + jnp.minimum(j, last(i))
    return _mixer_call(
        functools.partial(_attn_kernel, tq=tq, tk=tk),
        grid=(t // tq, t // tk),
        in_specs=[pl.BlockSpec((tq, DSA_W), lambda i, j: (i, 0)),
                  pl.BlockSpec((tk, DSA_W), lambda i, j: (jnp.minimum(j, last(i)), 0)),
                  pl.BlockSpec((tk, DSA_W), lambda i, j: (jnp.minimum(j, last(i)), 0)),
                  pl.BlockSpec((tq, tk), lambda i, j: (i, jnp.minimum(j, last(i))))],
        out_spec=pl.BlockSpec((tq, DSA_W), lambda i, j: (i, 0)),
        out_shape=jax.ShapeDtypeStruct((t, DSA_W), MXU_DTYPE),
        scratch_shapes=[pltpu.VMEM((DSA_HEADS, tq, LANES), F32),
                        pltpu.VMEM((DSA_HEADS, tq, LANES), F32),
                        pltpu.VMEM((tq, DSA_W), F32)],
        name="dsa_attention",
        operands=(q, k, v, bias),
        cast=cast and (*cast, tick, before(t // tq), 1))


def _pad_heads(w, heads, width, padded):
    lead = w.shape[:-1]
    w = w.reshape(lead + (heads, width))
    w = jnp.pad(w, [(0, 0)] * len(lead) + [(0, 0), (0, padded - width)])
    return w.reshape(lead + (heads * padded,))


def _pack_plan():
    src, start = {}, 0
    names = ("ml_q", "ml_k", "ml_v", "ml_i", "ml_f", "ml_o", "dsa_cq", "dsa_k", "dsa_v", "idx_k", "idx_w",
             "hg_q", "hg_f", "hg_i", "hg_g")
    for name, width in zip(names, IN_SPLITS):
        src[name] = start
        start += width
    copies = [(OFF_ML_Q, src["ml_q"], ML_QK_W), (OFF_ML_K, src["ml_k"], ML_QK_W),
              (OFF_ML_V, src["ml_v"], ML_V_W), (OFF_ML_O, src["ml_o"], ML_V_W),
              (OFF_HG_Q, src["hg_q"], HG_K_W), (OFF_HG_F, src["hg_f"], HG_K_W),
              (OFF_HG_I, src["hg_i"], HG_V_W), (OFF_HG_G, src["hg_g"], HG_V_W),
              (OFF_DSA_CQ, src["dsa_cq"], DSA_Q_RANK), (OFF_DSA_K, src["dsa_k"], DSA_W), (OFF_DSA_V, src["dsa_v"], DSA_W)]
    small = [(src["ml_i"], 2 * ML_HEADS), (src["idx_k"], IDX_DH + IDX_HEADS)]
    return copies, small


def _pack_kernel(wt_ref, o_ref):
    kt = o_ref.shape[0]
    copies, small = _pack_plan()
    for dst, src, width in copies:
        o_ref[:, dst:dst + width] = wt_ref[src:src + width, :].T.astype(o_ref.dtype)
    rows = [wt_ref[src:src + width, :] for src, width in small]
    used = sum(width for _, width in small)
    rows.append(jnp.zeros((LANES - used, kt), wt_ref.dtype))
    o_ref[:, OFF_SMALL:OFF_SMALL + LANES] = jnp.concatenate(rows, axis=0).T.astype(o_ref.dtype)


def _pack_w_in(w_in, *, kt):
    depth, d, n = w_in.shape
    wt = jnp.swapaxes(w_in, 1, 2)
    return pl.pallas_call(
        _pack_kernel,
        grid=(depth, d // kt),
        in_specs=[pl.BlockSpec((None, n, kt), lambda l, i: (l, 0, i))],
        out_specs=pl.BlockSpec((None, kt, D_IN_PAD), lambda l, i: (l, i, 0)),
        out_shape=jax.ShapeDtypeStruct((depth, d, D_IN_PAD), MXU_DTYPE),
        compiler_params=_params("parallel", "parallel"),
        name="pack_w_in",
    )(wt)


def _rope_freqs():
    def lanes(d):
        rot = d // ROPE_FRACTION
        half = rot // 2
        inv = jnp.power(ROPE_THETA, -jnp.arange(half, dtype=F32) * (2.0 / rot))
        return jnp.concatenate([inv, inv, jnp.zeros((LANES - rot,), F32)])
    return jnp.stack([lanes(DSA_DH), lanes(IDX_DH)])


def kernel(x, positions, ln_mix_g, w_in, ml_conv_w, ml_gate_b, ml_norm_g, dsa_q_norm_g, dsa_w_uq,
           hg_lb_logits, hg_norm_g, w_out, ln_mlp_g, w_up, w_down, ln_final_g):
    bsz, t, d = x.shape
    assert bsz == 1 and t % 512 == 0 and d == D_MODEL
    depth = w_in.shape[0]
    topk = min(TOPK_MAX, t // 4)
    xs = x.reshape(t, d)
    pos = positions.reshape(t, 1)

    w_in_p = _pack_w_in(w_in, kt=256)
    gate_b = jnp.pad(ml_gate_b.reshape(depth, 1, 2 * ML_HEADS), ((0, 0), (0, 0), (0, LANES - 2 * ML_HEADS)))
    w_uq_p = jnp.concatenate([dsa_w_uq[..., :DSA_W],
                              _pad_heads(dsa_w_uq[..., DSA_W:], IDX_HEADS, IDX_DH, IDX_SLOT)],
                             axis=-1).astype(MXU_DTYPE)
    freqs = _rope_freqs()
    lb_cum = jnp.cumsum(jax.nn.softmax(hg_lb_logits.astype(F32), axis=0), axis=0)
    lb = lb_cum - lb_cum[:1]
    lb_tab = jnp.stack([jnp.log(lb), jnp.log1p(-lb), 1.0 - lb], axis=1)

    for layer in range(depth):
        u = _norm_matmul(xs, ln_mix_g[layer], w_in_p, layer, tm=512, tn=1024, act=False, out_dtype=F32)
        y_a, w_out_b = _mlstm(u, ml_conv_w[layer], gate_b[layer], ml_norm_g[layer], rows=256, cast=(w_out, layer))
        q_r, qi_r, k_r, v_b, ki_r, wi = _dsa_prep(u, pos, dsa_q_norm_g[layer], w_uq_p[layer], freqs, rows=512)
        bias = _indexer(qi_r, wi, ki_r, topk=topk)
        y_b, w_down_b = _attention(q_r, k_r, v_b, bias, tq=512, tk=1024, cast=(w_down, layer))
        y_c, w_up_b = _hgrn2(u, lb_tab[layer], hg_norm_g[layer], rows=256, cast=(w_up, layer))
        xs = _mix_out(y_a, y_b, y_c, w_out_b[None], 0, xs, tm=512, tn=1024)
        a = _norm_matmul(xs, ln_mlp_g[layer], w_up_b[None], 0, tm=512, tn=1024, act=True, out_dtype=MXU_DTYPE)
        xs = _matmul_res(a, w_down_b[None], 0, xs, tm=1024, tn=1024, tk=2048)
    return _final_norm(xs, ln_final_g, tm=256).reshape(bsz, t, d)
```

```python
import functools

import jax
import jax.numpy as jnp
import numpy as np
from jax import lax
from jax.experimental import pallas as pl
from jax.experimental.pallas import tpu as pltpu

F32 = jnp.float32
BF16 = jnp.bfloat16
MXU_DTYPE = jnp.bfloat16

D_MODEL = 4096
CHUNK = 64
EPS = 1e-6
ROPE_THETA = 500000.0
ROPE_FRACTION = 4
ML_HEADS = 4
ML_DV = 384
ML_DQK = ML_DV // 2
ML_DQK_PAD = 256
ML_CONV = 4
ML_GATE_CAP = 15.0
DSA_HEADS = 8
DSA_DH = 128
DSA_Q_RANK = 384
IDX_HEADS = 8
IDX_DH = 64
IDX_SLOT = 128
TOPK_MAX = 256
HG_HEADS = 12
HG_DK = 128
HG_DV = 128

ML_QK_W = ML_HEADS * ML_DQK
ML_V_W = ML_HEADS * ML_DV
DSA_W = DSA_HEADS * DSA_DH
HG_K_W = HG_HEADS * HG_DK
HG_V_W = HG_HEADS * HG_DV
D_MIX = ML_V_W + DSA_W + HG_V_W
IN_SPLITS = (ML_QK_W, ML_QK_W, ML_V_W, ML_HEADS, ML_HEADS, ML_V_W,
             DSA_Q_RANK, DSA_W, DSA_W, IDX_DH, IDX_HEADS,
             HG_K_W, HG_K_W, HG_V_W, HG_V_W)

LANES = 128
VMEM_LIMIT = 56 * 1024 * 1024

OFF_ML_Q = 0
OFF_ML_K = 768
OFF_ML_V = 1536
OFF_ML_O = 3072
OFF_HG_Q = 4608
OFF_HG_F = 6144
OFF_HG_I = 7680
OFF_HG_G = 9216
OFF_DSA_CQ = 10752
OFF_SMALL = 11136
OFF_DSA_K = 11264
OFF_DSA_V = 12288
D_IN_PAD = 13312
SMALL_IDX_K = 2 * ML_HEADS
SMALL_IDX_W = SMALL_IDX_K + IDX_DH

PROJ_TM, PROJ_TN = 512, 1024
DOWN_TM, DOWN_TN, DOWN_TK = 1024, 1024, 2048
MIXER_ROWS = 256
PREP_ROWS = 512
ATTN_TQ, ATTN_TK = 512, 1024
NORM_ROWS = 256
PACK_KT = 256

LOG2E = 1.4426950408889634
Q_SCALE = DSA_DH ** -0.5 * LOG2E

INT_MIN = -2 ** 31
KEY_NEG_INF = -2139095041
NEG_BIG = -1e30


def _mxu(a):
    return a.astype(MXU_DTYPE)


def _dot(a, b):
    return jnp.dot(_mxu(a), _mxu(b), preferred_element_type=F32)


def _dot_nt(a, b):
    return lax.dot_general(_mxu(a), _mxu(b), (((1,), (1,)), ((), ())), preferred_element_type=F32)


def _dot_tn(a, b):
    return lax.dot_general(_mxu(a), _mxu(b), (((0,), (0,)), ((), ())), preferred_element_type=F32)


def _dot_f32(a, b):
    return jnp.dot(a, b, precision=lax.Precision.HIGHEST, preferred_element_type=F32)


def _sigmoid(x):
    return 1.0 / (1.0 + jnp.exp(-x))


def _log_sigmoid(x):
    return jnp.minimum(x, 0.0) - jnp.log1p(jnp.exp(-jnp.abs(x)))


def _params(*sem):
    return pltpu.CompilerParams(dimension_semantics=sem, vmem_limit_bytes=VMEM_LIMIT)


BF16_SUBLANES = 16


def _mixer_call(kernel_fn, *, grid, in_specs, out_spec, out_shape, scratch_shapes, name, operands, cast=None):
    params = _params("parallel", "arbitrary")
    if cast is None:
        return pl.pallas_call(kernel_fn, grid=grid, in_specs=in_specs, out_specs=out_spec, out_shape=out_shape,
                              scratch_shapes=scratch_shapes, compiler_params=params, name=name)(*operands)
    w, layer, tick, ticks, every = cast
    params = _params("arbitrary", "arbitrary")
    rows, cols = w.shape[1:]
    blocks = ticks // every
    while rows % blocks or (rows // blocks) % BF16_SUBLANES:
        blocks -= 1
    block = lambda a, b: jnp.minimum(tick(a, b) // every, blocks - 1)
    n_in = len(in_specs)

    def body(*refs):
        w_ref, w_out = refs[n_in], refs[n_in + 2]
        now = tick(pl.program_id(0), pl.program_id(1))

        @pl.when((now % every == 0) & (now // every < blocks))
        def _():
            w_out[...] = w_ref[...].astype(w_out.dtype)

        kernel_fn(*refs[:n_in], refs[n_in + 1], *refs[n_in + 3:])

    return pl.pallas_call(
        body, grid=grid,
        in_specs=in_specs + [pl.BlockSpec((None, rows // blocks, cols), lambda a, b: (layer, block(a, b), 0))],
        out_specs=(out_spec, pl.BlockSpec((rows // blocks, cols), lambda a, b: (block(a, b), 0))),
        out_shape=(out_shape, jax.ShapeDtypeStruct((rows, cols), MXU_DTYPE)),
        scratch_shapes=scratch_shapes, compiler_params=params, name=name)(*operands, w)


def _norm_matmul_kernel(x_ref, g_ref, w_ref, o_ref, h_ref, *, act):
    @pl.when(pl.program_id(1) == 0)
    def _():
        x = x_ref[...]
        ms = jnp.mean(x * x, axis=-1, keepdims=True)
        h_ref[...] = (x * lax.rsqrt(ms + EPS) * g_ref[...]).astype(h_ref.dtype)

    y = jnp.dot(h_ref[...], w_ref[...], preferred_element_type=F32)
    if act:
        y = jnp.square(jnp.maximum(y, 0.0))
    o_ref[...] = y.astype(o_ref.dtype)


def _norm_matmul(x, g, w, layer, *, tm, tn, act, out_dtype):
    m, k = x.shape
    n = w.shape[2]
    return pl.pallas_call(
        functools.partial(_norm_matmul_kernel, act=act),
        grid=(m // tm, n // tn),
        in_specs=[pl.BlockSpec((tm, k), lambda i, j: (i, 0)),
                  pl.BlockSpec((1, k), lambda i, j: (0, 0)),
                  pl.BlockSpec((None, k, tn), lambda i, j: (layer, 0, j))],
        out_specs=pl.BlockSpec((tm, tn), lambda i, j: (i, j)),
        out_shape=jax.ShapeDtypeStruct((m, n), out_dtype),
        scratch_shapes=[pltpu.VMEM((tm, k), MXU_DTYPE)],
        compiler_params=_params("parallel", "arbitrary"),
        name="norm_matmul",
    )(x, g.reshape(1, k), w)


def _matmul_res_kernel(a_ref, w_ref, r_ref, o_ref):
    part = jnp.dot(a_ref[...], w_ref[...], preferred_element_type=F32)

    @pl.when(pl.program_id(2) == 0)
    def _():
        o_ref[...] = r_ref[...] + part

    @pl.when(pl.program_id(2) != 0)
    def _():
        o_ref[...] += part


def _matmul_res(a, w, layer, res, *, tm, tn, tk):
    m, k = a.shape
    n = w.shape[2]
    return pl.pallas_call(
        _matmul_res_kernel,
        grid=(m // tm, n // tn, k // tk),
        in_specs=[pl.BlockSpec((tm, tk), lambda i, j, kk: (i, kk)),
                  pl.BlockSpec((None, tk, tn), lambda i, j, kk: (layer, kk, j)),
                  pl.BlockSpec((tm, tn), lambda i, j, kk: (i, j))],
        out_specs=pl.BlockSpec((tm, tn), lambda i, j, kk: (i, j)),
        out_shape=jax.ShapeDtypeStruct((m, n), F32),
        compiler_params=_params("parallel", "parallel", "arbitrary"),
        name="matmul_res",
    )(a, w, res)


def _mix_out_kernel(ya_ref, yb_ref, yc_ref, w_ref, r_ref, o_ref):
    acc = r_ref[...]
    acc += jnp.dot(ya_ref[...], w_ref[0:ML_V_W, :], preferred_element_type=F32)
    acc += jnp.dot(yb_ref[...], w_ref[ML_V_W:ML_V_W + DSA_W, :], preferred_element_type=F32)
    acc += jnp.dot(yc_ref[...], w_ref[ML_V_W + DSA_W:D_MIX, :], preferred_element_type=F32)
    o_ref[...] = acc


def _mix_out(ya, yb, yc, w, layer, res, *, tm, tn):
    m = ya.shape[0]
    n = w.shape[2]
    return pl.pallas_call(
        _mix_out_kernel,
        grid=(m // tm, n // tn),
        in_specs=[pl.BlockSpec((tm, ML_V_W), lambda i, j: (i, 0)),
                  pl.BlockSpec((tm, DSA_W), lambda i, j: (i, 0)),
                  pl.BlockSpec((tm, HG_V_W), lambda i, j: (i, 0)),
                  pl.BlockSpec((None, D_MIX, tn), lambda i, j: (layer, 0, j)),
                  pl.BlockSpec((tm, tn), lambda i, j: (i, j))],
        out_specs=pl.BlockSpec((tm, tn), lambda i, j: (i, j)),
        out_shape=jax.ShapeDtypeStruct((m, n), F32),
        compiler_params=_params("parallel", "parallel"),
        name="mix_out",
    )(ya, yb, yc, w, res)


def _final_norm_kernel(x_ref, g_ref, o_ref):
    x = x_ref[...]
    ms = jnp.mean(x * x, axis=-1, keepdims=True)
    o_ref[...] = x * lax.rsqrt(ms + EPS) * g_ref[...]


def _final_norm(x, g, *, tm):
    m, k = x.shape
    return pl.pallas_call(
        _final_norm_kernel,
        grid=(m // tm,),
        in_specs=[pl.BlockSpec((tm, k), lambda i: (i, 0)),
                  pl.BlockSpec((1, k), lambda i: (0, 0))],
        out_specs=pl.BlockSpec((tm, k), lambda i: (i, 0)),
        out_shape=jax.ShapeDtypeStruct((m, k), F32),
        compiler_params=_params("parallel"),
        name="final_norm",
    )(x, g.reshape(1, k))


def _mlstm_kernel(q_ref, k_ref, v_ref, o_ref, sm_ref, cw_ref, gb_ref, ng_ref, out_ref,
                  xbuf, c_ref, *, rows):
    @pl.when(pl.program_id(1) == 0)
    def _():
        xbuf[0:8, :] = jnp.zeros((8, 2 * ML_QK_W), F32)
        c_ref[...] = jnp.zeros_like(c_ref)

    xbuf[8:8 + rows, 0:ML_QK_W] = q_ref[...]
    xbuf[8:8 + rows, ML_QK_W:] = k_ref[...]
    cw = cw_ref[...]
    acc = xbuf[8:8 + rows, :] * cw[ML_CONV - 1:ML_CONV, :]
    for j in range(1, ML_CONV):
        acc += xbuf[8 - j:8 - j + rows, :] * cw[ML_CONV - 1 - j:ML_CONV - j, :]
    xbuf[0:8, :] = xbuf[rows:rows + 8, :]
    qk = acc * _sigmoid(acc)
    qk = jnp.concatenate([qk, jnp.zeros((rows, LANES), F32)], axis=1)

    capped = ML_GATE_CAP * jnp.tanh((sm_ref[...] + gb_ref[...]) * (1.0 / ML_GATE_CAP))
    lsig = _log_sigmoid(capped)
    r_i = lax.broadcasted_iota(jnp.int32, (rows, rows), 0)
    c_i = lax.broadcasted_iota(jnp.int32, (rows, rows), 1)
    causal = c_i <= r_i
    b_all = _dot_f32(jnp.where(causal, 1.0, 0.0), lsig)
    capped_t, b_all_t = capped.T, b_all.T
    one_col = jnp.where(lax.broadcasted_iota(jnp.int32, (rows, LANES), 1) == 0, 1.0, 0.0)
    real = lax.broadcasted_iota(jnp.int32, (rows, ML_DQK_PAD), 1) < ML_DQK

    for h in range(ML_HEADS):
        q = jnp.where(real, qk[:, h * ML_DQK:h * ML_DQK + ML_DQK_PAD], 0.0)
        k = jnp.where(real, qk[:, ML_QK_W + h * ML_DQK:ML_QK_W + h * ML_DQK + ML_DQK_PAD], 0.0) * (ML_DQK ** -0.5)
        ig_col, b_col = capped[:, h:h + 1], b_all[:, ML_HEADS + h:ML_HEADS + h + 1]
        ig_row, b_row = capped_t[h:h + 1, :], b_all_t[ML_HEADS + h:ML_HEADS + h + 1, :]
        dv = slice(h * ML_DV, (h + 1) * ML_DV)

        dmat = jnp.exp(jnp.where(causal, b_col - b_row + ig_row, -jnp.inf))
        s = _dot_nt(q, k) * dmat
        v_ext = jnp.concatenate([v_ref[:, dv], one_col], axis=1)
        c_old = c_ref[h]
        num_ext = _dot(s, v_ext) + jnp.exp(b_col) * _dot(q, c_old)
        num = num_ext[:, 0:ML_DV]
        den = num_ext[:, ML_DV:ML_DV + 1]
        hh = num / jnp.maximum(jnp.abs(den), 1.0)

        b_last = b_col[rows - 1:rows, :]
        w_s = jnp.exp(b_last - b_col + ig_col)
        c_ref[h] = jnp.exp(b_last) * c_old + _dot_tn(k, w_s * v_ext)

        ms = jnp.mean(hh * hh, axis=-1, keepdims=True)
        y = hh * lax.rsqrt(ms + EPS) * ng_ref[:, dv]
        out_ref[:, dv] = (_sigmoid(o_ref[:, dv]) * y).astype(out_ref.dtype)


def _mlstm(u, conv_w, gate_b, norm_g, *, rows, cast=None):
    t = u.shape[0]
    return _mixer_call(
        functools.partial(_mlstm_kernel, rows=rows),
        grid=(1, t // rows),
        in_specs=[pl.BlockSpec((rows, ML_QK_W), lambda _, c: (c, OFF_ML_Q // ML_QK_W)),
                  pl.BlockSpec((rows, ML_QK_W), lambda _, c: (c, OFF_ML_K // ML_QK_W)),
                  pl.BlockSpec((rows, ML_V_W), lambda _, c: (c, OFF_ML_V // ML_V_W)),
                  pl.BlockSpec((rows, ML_V_W), lambda _, c: (c, OFF_ML_O // ML_V_W)),
                  pl.BlockSpec((rows, LANES), lambda _, c: (c, OFF_SMALL // LANES)),
                  pl.BlockSpec((ML_CONV, 2 * ML_QK_W), lambda _, c: (0, 0)),
                  pl.BlockSpec((1, LANES), lambda _, c: (0, 0)),
                  pl.BlockSpec((1, ML_V_W), lambda _, c: (0, 0))],
        out_spec=pl.BlockSpec((rows, ML_V_W), lambda _, c: (c, 0)),
        out_shape=jax.ShapeDtypeStruct((t, ML_V_W), MXU_DTYPE),
        scratch_shapes=[pltpu.VMEM((rows + 8, 2 * ML_QK_W), F32),
                        pltpu.VMEM((ML_HEADS, ML_DQK_PAD, ML_DV + LANES), F32)],
        name="mlstm",
        operands=(u, u, u, u, u, conv_w, gate_b, norm_g.reshape(1, ML_V_W)),
        cast=cast and (*cast, lambda _, c: c, t // rows, 1))


SUBLANES = 8
HG_CAST_EVERY = 1
HG_GROUP = 6


def _pair_level_table(rows):
    t = np.arange(rows)[:, None]
    s = np.arange(rows)[None, :]
    x = np.bitwise_xor(t, s)
    lvl = np.floor(np.log2(np.maximum(x, 1))).astype(np.int32)
    return jnp.asarray(np.where(t > s, lvl, -1).astype(np.int32))


def _block_sums(log_f, rows):
    sub = lax.broadcasted_iota(jnp.int32, log_f.shape, 0)
    groups = rows // SUBLANES

    def row_of_group(x, j):
        x3 = x.reshape(groups, SUBLANES, x.shape[-1])
        return jnp.broadcast_to(x3[:, j:j + 1, :], x3.shape).reshape(x.shape)

    c = log_f
    e = jnp.zeros_like(log_f)
    out = [(c, e)]
    odd = (sub & 1) == 1
    c, e = (c + jnp.where(odd, pltpu.roll(c, 1, 0), 0.0),
            e + jnp.where(odd, 0.0, pltpu.roll(c, rows - 1, 0)))
    out.append((c, e))
    r8 = sub & 7
    c, e = (c + jnp.where((r8 == 2) | (r8 == 3), row_of_group(c, 1),
                          jnp.where((r8 == 6) | (r8 == 7), row_of_group(c, 5), 0.0)),
            e + jnp.where((r8 == 0) | (r8 == 1), row_of_group(c, 3),
                          jnp.where((r8 == 4) | (r8 == 5), row_of_group(c, 7), 0.0)))
    out.append((c, e))
    c, e = (c + jnp.where(r8 >= 4, row_of_group(c, 3), 0.0),
            e + jnp.where(r8 < 4, row_of_group(c, 7), 0.0))
    out.append((c, e))
    m = SUBLANES
    while m < rows:
        cs, es = [], []
        for p in range(rows // (2 * m)):
            lo, mid, hi = 2 * m * p, 2 * m * p + m, 2 * m * (p + 1)
            cs += [c[lo:mid], c[mid:hi] + c[mid - 1:mid]]
            es += [e[lo:mid] + c[hi - 1:hi], e[mid:hi]]
        c, e = jnp.concatenate(cs, axis=0), jnp.concatenate(es, axis=0)
        out.append((c, e))
        m *= 2
    return out


def _hgrn2_kernel(q_ref, f_ref, i_ref, g_ref, lb_ref, ng_ref, lvl_ref, out_ref, st_ref, *, rows):
    @pl.when(pl.program_id(1) == 0)
    def _():
        st_ref[...] = jnp.zeros_like(st_ref)

    for h in range(HG_GROUP):
        dk = slice(h * HG_DK, (h + 1) * HG_DK)
        out, st_ref[h] = _hgrn2_head(q_ref[:, dk], f_ref[:, dk], i_ref[:, dk], g_ref[:, dk], lb_ref[:, dk],
                                     ng_ref[:, dk], lvl_ref[...], st_ref[h], rows)
        out_ref[:, dk] = out.astype(out_ref.dtype)


def _hgrn2_head(qp, fp, v, gp, lb, ng, lvl, st, rows):
    q = qp * _sigmoid(qp)
    log_lb = lb[0:1, :]
    x2 = lb[1:2, :] + _log_sigmoid(fp)
    log_f = jnp.maximum(log_lb, x2) + jnp.log1p(jnp.exp(-jnp.abs(log_lb - x2)))
    k = lb[2:3, :] * _sigmoid(-fp)

    sums = _block_sums(log_f * LOG2E, rows)
    tiles = rows // LANES
    tile_rows = [slice(r * LANES, (r + 1) * LANES) for r in range(tiles)]
    diag = [jnp.zeros((LANES, LANES), F32) for _ in range(tiles)]
    o_tiles = [None] * tiles
    for level, (c_m, e_m) in enumerate(sums[:-1]):
        m = 1 << level
        qh = q * jnp.exp2(c_m)
        kh = k * jnp.exp2(e_m)
        if m < LANES:
            for r, sl in enumerate(tile_rows):
                diag[r] = jnp.where(lvl == level, _dot_nt(qh[sl], kh[sl]), diag[r])
        else:
            for p in range(rows // (2 * m)):
                lo, mid = 2 * m * p, 2 * m * p + m
                for r in range(mid // LANES, (mid + m) // LANES):
                    part = _dot(_dot_nt(qh[tile_rows[r]], kh[lo:mid]), v[lo:mid])
                    o_tiles[r] = part if o_tiles[r] is None else o_tiles[r] + part
    for r, sl in enumerate(tile_rows):
        part = _dot(diag[r], v[sl])
        o_tiles[r] = part if o_tiles[r] is None else o_tiles[r] + part
    b, after = sums[-1]

    o = (jnp.concatenate(o_tiles, axis=0) + jnp.sum(q * k, axis=-1, keepdims=True) * v
         + _dot_nt(q * jnp.exp2(b), st))
    st_new = st * jnp.exp2(b[rows - 1:rows, :]) + _dot_tn(v, k * jnp.exp2(after))

    ms = jnp.mean(o * o, axis=-1, keepdims=True)
    y = o * lax.rsqrt(ms + EPS) * ng
    return y * (gp * _sigmoid(gp)), st_new


def _hgrn2(u, lb_tab, norm_g, *, rows, cast=None):
    t = u.shape[0]
    width = HG_GROUP * HG_DK
    groups = HG_HEADS // HG_GROUP
    qb, fb, ib, gb = OFF_HG_Q // width, OFF_HG_F // width, OFF_HG_I // width, OFF_HG_G // width
    return _mixer_call(
        functools.partial(_hgrn2_kernel, rows=rows),
        grid=(groups, t // rows),
        in_specs=[pl.BlockSpec((rows, width), lambda h, c: (c, qb + h)),
                  pl.BlockSpec((rows, width), lambda h, c: (c, fb + h)),
                  pl.BlockSpec((rows, width), lambda h, c: (c, ib + h)),
                  pl.BlockSpec((rows, width), lambda h, c: (c, gb + h)),
                  pl.BlockSpec((3, width), lambda h, c: (0, h)),
                  pl.BlockSpec((1, width), lambda h, c: (0, h)),
                  pl.BlockSpec((LANES, LANES), lambda h, c: (0, 0))],
        out_spec=pl.BlockSpec((rows, width), lambda h, c: (c, h)),
        out_shape=jax.ShapeDtypeStruct((t, HG_V_W), MXU_DTYPE),
        scratch_shapes=[pltpu.VMEM((HG_GROUP, HG_DV, HG_DK), F32)],
        name="hgrn2",
        operands=(u, u, u, u, lb_tab, norm_g.reshape(1, HG_V_W), _pair_level_table(LANES)),
        cast=cast and (*cast, lambda h, c: h * (t // rows) + c, groups * (t // rows), HG_CAST_EVERY))


def _rope_slab(u, cc, sa, sb, half):
    return u * cc + pltpu.roll(u, LANES - half, 1) * sa + pltpu.roll(u, half, 1) * sb


def _rope_coeffs(ang, half):
    lane = lax.broadcasted_iota(jnp.int32, ang.shape, 1)
    cos, sin = jnp.cos(ang), jnp.sin(ang)
    cc = jnp.where(lane < 2 * half, cos, 1.0)
    sa = jnp.where(lane < half, -sin, 0.0)
    sb = jnp.where((lane >= half) & (lane < 2 * half), sin, 0.0)
    return cc, sa, sb


def _dsa_prep_kernel(cq_ref, k_ref, v_ref, idx_ref, pos_ref, g_ref, w_ref, fr_ref,
                     q_out, qi_out, k_out, v_out, ki_out, wi_out):
    pos = pos_ref[...].astype(F32)
    cq = cq_ref[...]
    ms = jnp.mean(cq * cq, axis=-1, keepdims=True)
    hq = cq * lax.rsqrt(ms + EPS) * g_ref[...]
    q_all = _dot(hq, w_ref[...])

    half_a = DSA_DH // ROPE_FRACTION // 2
    half_i = IDX_DH // ROPE_FRACTION // 2
    ca = _rope_coeffs(pos * fr_ref[0:1, :], half_a)
    ci = _rope_coeffs(pos * fr_ref[1:2, :], half_i)

    kk = k_ref[...]
    for h in range(DSA_HEADS):
        sl = slice(h * DSA_DH, (h + 1) * DSA_DH)
        q_out[:, sl] = (_rope_slab(q_all[:, sl], *ca, half_a) * Q_SCALE).astype(q_out.dtype)
        k_out[:, sl] = _rope_slab(kk[:, sl], *ca, half_a).astype(k_out.dtype)
    for h in range(IDX_HEADS):
        src = slice(DSA_W + h * IDX_SLOT, DSA_W + (h + 1) * IDX_SLOT)
        qi_out[h] = _rope_slab(q_all[:, src], *ci, half_i).astype(qi_out.dtype)
    v_out[...] = v_ref[...].astype(v_out.dtype)

    small = idx_ref[...]
    lane = lax.broadcasted_iota(jnp.int32, small.shape, 1)
    ki = jnp.where(lane < IDX_DH, pltpu.roll(small, LANES - SMALL_IDX_K, 1), 0.0)
    ki_out[...] = _rope_slab(ki, *ci, half_i).astype(ki_out.dtype)
    wi = pltpu.roll(small, LANES - SMALL_IDX_W, 1) * (IDX_HEADS ** -0.5 * IDX_DH ** -0.5)
    wi_out[...] = jnp.where(lane < IDX_HEADS, wi, 0.0)


def _dsa_prep(u, pos, q_norm_g, w_uq_pad, freqs, *, rows):
    t = u.shape[0]
    nq = DSA_W + IDX_HEADS * IDX_SLOT
    outs = (jax.ShapeDtypeStruct((t, DSA_W), MXU_DTYPE),
            jax.ShapeDtypeStruct((IDX_HEADS, t, IDX_SLOT), MXU_DTYPE),
            jax.ShapeDtypeStruct((t, DSA_W), MXU_DTYPE),
            jax.ShapeDtypeStruct((t, DSA_W), MXU_DTYPE),
            jax.ShapeDtypeStruct((t, IDX_SLOT), MXU_DTYPE),
            jax.ShapeDtypeStruct((t, LANES), F32))
    row_spec = lambda w, blk: pl.BlockSpec((rows, w), lambda i: (i, blk))
    return pl.pallas_call(
        _dsa_prep_kernel,
        grid=(t // rows,),
        in_specs=[row_spec(DSA_Q_RANK, OFF_DSA_CQ // DSA_Q_RANK),
                  row_spec(DSA_W, OFF_DSA_K // DSA_W),
                  row_spec(DSA_W, OFF_DSA_V // DSA_W),
                  row_spec(LANES, OFF_SMALL // LANES),
                  pl.BlockSpec((rows, 1), lambda i: (i, 0)),
                  pl.BlockSpec((1, DSA_Q_RANK), lambda i: (0, 0)),
                  pl.BlockSpec((DSA_Q_RANK, nq), lambda i: (0, 0)),
                  pl.BlockSpec((2, LANES), lambda i: (0, 0))],
        out_specs=(row_spec(DSA_W, 0),
                   pl.BlockSpec((IDX_HEADS, rows, IDX_SLOT), lambda i: (0, i, 0)),
                   row_spec(DSA_W, 0), row_spec(DSA_W, 0), row_spec(IDX_SLOT, 0), row_spec(LANES, 0)),
        out_shape=outs,
        compiler_params=_params("parallel"),
        name="dsa_prep",
    )(u, u, u, u, pos, q_norm_g.reshape(1, DSA_Q_RANK), w_uq_pad, freqs)


IDX_TQ = 512
IDX_TK = 512
IDX_SLAB = 128


def _indexer_kernel(qi_ref, wi_ref, ki_ref, incl_ref, bias_ref, key_ref, *, topk):
    tq, tk = IDX_TQ, IDX_TK
    qb = pl.program_id(0)
    nkb = ((qb + 1) * tq + tk - 1) // tk
    q_all = qi_ref[...].reshape(IDX_HEADS * tq, IDX_SLOT)
    w = wi_ref[...]
    w_col = jnp.concatenate([w[:, h:h + 1] for h in range(IDX_HEADS)], axis=0)
    row_chunk = (qb * tq + lax.broadcasted_iota(jnp.int32, (tq, 1), 0)) // CHUNK
    col_in_blk = lax.broadcasted_iota(jnp.int32, (1, tk), 1)

    def score_body(kb, carry):
        kt = ki_ref[pl.ds(pl.multiple_of(kb * tk, tk), tk), :]
        weighted = jnp.maximum(_dot_nt(q_all, kt), 0.0) * w_col
        parts = [weighted[h * tq:(h + 1) * tq] for h in range(IDX_HEADS)]
        while len(parts) > 1:
            parts = [a + b for a, b in zip(parts[0::2], parts[1::2])]
        col_chunk = (kb * tk + col_in_blk) // CHUNK
        score = jnp.where(col_chunk <= row_chunk, parts[0], -jnp.inf)
        bits = pltpu.bitcast(score, jnp.int32)
        bits = jnp.where(bits == INT_MIN, 0, bits)
        key_ref[kb] = jnp.where(bits < 0, bits ^ 0x7FFFFFFF, bits)
        return carry

    lax.fori_loop(0, nkb, score_body, 0)

    ones_mat = jnp.ones((LANES, LANES), MXU_DTYPE)

    slabs = [slice(r0, r0 + IDX_SLAB) for r0 in range(0, tq, IDX_SLAB)]

    def slab_counts(cand_r, rows):
        def one_block(kb, cnt):
            for j in range(tk // LANES):
                cnt += jnp.where(key_ref[kb, rows, j * LANES:(j + 1) * LANES] >= cand_r, 1, 0)
            return cnt

        cnt = lax.fori_loop(0, nkb // 2, lambda p, c: one_block(2 * p + 1, one_block(2 * p, c)),
                            jnp.zeros((IDX_SLAB, LANES), jnp.int32))
        return lax.cond(nkb % 2 == 1, lambda c: one_block(nkb - 1, c), lambda c: c, cnt)

    def row_totals(cnt):
        return jnp.dot(cnt.astype(F32).astype(MXU_DTYPE), ones_mat, preferred_element_type=F32)

    def count_ge(cand):
        return row_totals(jnp.concatenate([slab_counts(cand[rows], rows) for rows in slabs], axis=0))

    zero = jnp.zeros((tq, LANES), jnp.int32)
    cnt0 = count_ge(zero)
    thr_rep = jnp.where(cnt0 >= topk, zero, INT_MIN)
    at_thr = jnp.where(cnt0 >= topk, cnt0, (nkb * tk).astype(F32))

    def bit_body(i, carry):
        thr_rep, at_thr = carry
        cand = thr_rep + jnp.left_shift(jnp.int32(1), 30 - i)
        cnt = count_ge(cand)
        ok = cnt >= topk
        return jnp.where(ok, cand, thr_rep), jnp.where(ok, cnt, at_thr)

    thr_rep, at_thr = lax.fori_loop(0, 31, bit_body, (thr_rep, at_thr))

    bias_ref[...] = jnp.full(bias_ref.shape, NEG_BIG, bias_ref.dtype)

    thr = thr_rep[:, 0:1]
    exact = jnp.all(at_thr == topk)

    def store(kb, take):
        cols = pl.ds(pl.multiple_of(kb * tk, tk), tk)
        bias_ref[:, cols] = jnp.where(take, 0.0, NEG_BIG).astype(bias_ref.dtype)

    @pl.when(exact)
    def _():
        def emit_body(kb, carry):
            key = key_ref[kb]
            store(kb, (key >= thr) & (key > KEY_NEG_INF))
            return carry

        lax.fori_loop(0, nkb, emit_body, 0)

    @pl.when(jnp.logical_not(exact))
    def _():
        need = topk - count_ge(thr_rep + 1)[:, 0:1]

        def emit_body(kb, seen):
            key = key_ref[kb]
            eq = key == thr
            eq_f = jnp.where(eq, 1.0, 0.0)
            rank = seen + jnp.dot(eq_f.astype(MXU_DTYPE), incl_ref[...], preferred_element_type=F32)
            store(kb, ((key > thr) | (eq & (rank <= need))) & (key > KEY_NEG_INF))
            return seen + jnp.sum(eq_f, axis=1, keepdims=True)

        lax.fori_loop(0, nkb, emit_body, jnp.zeros((tq, 1), F32))


def _indexer(qi, wi, ki, *, topk):
    t = ki.shape[0]
    incl = jnp.asarray(np.triu(np.ones((IDX_TK, IDX_TK), np.float32)), MXU_DTYPE)
    return pl.pallas_call(
        functools.partial(_indexer_kernel, topk=topk),
        grid=(t // IDX_TQ,),
        in_specs=[pl.BlockSpec((IDX_HEADS, IDX_TQ, IDX_SLOT), lambda i: (0, i, 0)),
                  pl.BlockSpec((IDX_TQ, LANES), lambda i: (i, 0)),
                  pl.BlockSpec((t, IDX_SLOT), lambda i: (0, 0)),
                  pl.BlockSpec((IDX_TK, IDX_TK), lambda i: (0, 0))],
        out_specs=pl.BlockSpec((IDX_TQ, t), lambda i: (i, 0)),
        out_shape=jax.ShapeDtypeStruct((t, t), BF16),
        scratch_shapes=[pltpu.VMEM((t // IDX_TK, IDX_TQ, IDX_TK), jnp.int32)],
        compiler_params=_params("parallel"),
        name="dsa_indexer",
    )(qi, wi, ki, incl)


def _attn_kernel(q_ref, k_ref, v_ref, bias_ref, o_ref, m_ref, l_ref, acc_ref, *, tq, tk):
    qb, kb = pl.program_id(0), pl.program_id(1)
    last = ((qb + 1) * tq - 1) // tk

    @pl.when(kb == 0)
    def _():
        m_ref[...] = jnp.full(m_ref.shape, NEG_BIG, F32)
        l_ref[...] = jnp.zeros_like(l_ref)
        acc_ref[...] = jnp.zeros_like(acc_ref)

    @pl.when(kb <= last)
    def _():
        bias = bias_ref[...].astype(F32)
        ones = jnp.ones((tk, LANES), v_ref.dtype)
        heads = [slice(h * DSA_DH, (h + 1) * DSA_DH) for h in range(DSA_HEADS)]
        qk = _dot_nt(q_ref[:, heads[0]], k_ref[:, heads[0]])
        for h, sl in enumerate(heads):
            s = qk + bias
            if h + 1 < DSA_HEADS:
                qk = _dot_nt(q_ref[:, heads[h + 1]], k_ref[:, heads[h + 1]])
            m_old = m_ref[h]
            m_new = jnp.maximum(m_old, jnp.max(s, axis=-1, keepdims=True))
            alpha = jnp.exp2(m_old - m_new)
            p = jnp.exp2(s - jnp.concatenate([m_new] * (tk // LANES), axis=1))
            pv = _dot(p, jnp.concatenate([v_ref[:, sl], ones], axis=1))
            l_ref[h] = alpha * l_ref[h] + pv[:, DSA_DH:]
            acc_ref[:, sl] = alpha * acc_ref[:, sl] + pv[:, :DSA_DH]
            m_ref[h] = m_new

    @pl.when(kb == last)
    def _():
        for h in range(DSA_HEADS):
            sl = slice(h * DSA_DH, (h + 1) * DSA_DH)
            o_ref[:, sl] = (acc_ref[:, sl] / l_ref[h]).astype(o_ref.dtype)


def _attention(q, k, v, bias, *, tq, tk, cast=None):
    t = q.shape[0]
    last = lambda i: ((i + 1) * tq - 1) // tk
    ratio = tk // tq
    assert tk == ratio * tq

    def before(i):
        m = i // ratio
        return i + ratio * (m * (m - 1) // 2) + (i - ratio * m) * m

    tick = lambda i, j: before(i) + jnp.minimum(j, last(i))
    return _mixer_call(
        functools.partial(_attn_kernel, tq=tq, tk=tk),
        grid=(t // tq, t // tk),
        in_specs=[pl.BlockSpec((tq, DSA_W), lambda i, j: (i, 0)),
                  pl.BlockSpec((tk, DSA_W), lambda i, j: (jnp.minimum(j, last(i)), 0)),
                  pl.BlockSpec((tk, DSA_W), lambda i, j: (jnp.minimum(j, last(i)), 0)),
                  pl.BlockSpec((tq, tk), lambda i, j: (i, jnp.minimum(j, last(i))))],
        out_spec=pl.BlockSpec((tq, DSA_W), lambda i, j: (i, 0)),
        out_shape=jax.ShapeDtypeStruct((t, DSA_W), MXU_DTYPE),
        scratch_shapes=[pltpu.VMEM((DSA_HEADS, tq, LANES), F32),
                        pltpu.VMEM((DSA_HEADS, tq, LANES), F32),
                        pltpu.VMEM((tq, DSA_W), F32)],
        name="dsa_attention",
        operands=(q, k, v, bias),
        cast=cast and (*cast, tick, before(t // tq), 1))


def _pad_heads(w, heads, width, padded):
    lead = w.shape[:-1]
    w = w.reshape(lead + (heads, width))
    w = jnp.pad(w, [(0, 0)] * len(lead) + [(0, 0), (0, padded - width)])
    return w.reshape(lead + (heads * padded,))


def _pack_plan():
    src, start = {}, 0
    names = ("ml_q", "ml_k", "ml_v", "ml_i", "ml_f", "ml_o", "dsa_cq", "dsa_k", "dsa_v", "idx_k", "idx_w",
             "hg_q", "hg_f", "hg_i", "hg_g")
    for name, width in zip(names, IN_SPLITS):
        src[name] = start
        start += width
    copies = [(OFF_ML_Q, src["ml_q"], ML_QK_W), (OFF_ML_K, src["ml_k"], ML_QK_W),
              (OFF_ML_V, src["ml_v"], ML_V_W), (OFF_ML_O, src["ml_o"], ML_V_W),
              (OFF_HG_Q, src["hg_q"], HG_K_W), (OFF_HG_F, src["hg_f"], HG_K_W),
              (OFF_HG_I, src["hg_i"], HG_V_W), (OFF_HG_G, src["hg_g"], HG_V_W),
              (OFF_DSA_CQ, src["dsa_cq"], DSA_Q_RANK), (OFF_DSA_K, src["dsa_k"], DSA_W), (OFF_DSA_V, src["dsa_v"], DSA_W)]
    small = [(src["ml_i"], 2 * ML_HEADS), (src["idx_k"], IDX_DH + IDX_HEADS)]
    return copies, small


def _pack_kernel(wt_ref, o_ref):
    kt = o_ref.shape[0]
    copies, small = _pack_plan()
    for dst, src, width in copies:
        o_ref[:, dst:dst + width] = wt_ref[src:src + width, :].T.astype(o_ref.dtype)
    rows = [wt_ref[src:src + width, :] for src, width in small]
    used = sum(width for _, width in small)
    rows.append(jnp.zeros((LANES - used, kt), wt_ref.dtype))
    o_ref[:, OFF_SMALL:OFF_SMALL + LANES] = jnp.concatenate(rows, axis=0).T.astype(o_ref.dtype)


def _pack_w_in(w_in, *, kt):
    depth, d, n = w_in.shape
    wt = jnp.swapaxes(w_in, 1, 2)
    return pl.pallas_call(
        _pack_kernel,
        grid=(depth, d // kt),
        in_specs=[pl.BlockSpec((None, n, kt), lambda l, i: (l, 0, i))],
        out_specs=pl.BlockSpec((None, kt, D_IN_PAD), lambda l, i: (l, i, 0)),
        out_shape=jax.ShapeDtypeStruct((depth, d, D_IN_PAD), MXU_DTYPE),
        compiler_params=_params("parallel", "parallel"),
        name="pack_w_in",
    )(wt)


def _rope_freqs():
    def lanes(d):
        rot = d // ROPE_FRACTION
        half = rot // 2
        inv = jnp.power(ROPE_THETA, -jnp.arange(half, dtype=F32) * (2.0 / rot))
        return jnp.concatenate([inv, inv, jnp.zeros((LANES - rot,), F32)])
    return jnp.stack([lanes(DSA_DH), lanes(IDX_DH)])


def kernel(x, positions, ln_mix_g, w_in, ml_conv_w, ml_gate_b, ml_norm_g, dsa_q_norm_g, dsa_w_uq,
           hg_lb_logits, hg_norm_g, w_out, ln_mlp_g, w_up, w_down, ln_final_g):
    bsz, t, d = x.shape
    assert bsz == 1 and t % max(DOWN_TM, ATTN_TK, IDX_TQ) == 0 and d == D_MODEL
    depth = w_in.shape[0]
    topk = min(TOPK_MAX, t // 4)
    xs = x.reshape(t, d)
    pos = positions.reshape(t, 1)

    w_in_p = _pack_w_in(w_in, kt=PACK_KT)
    gate_b = jnp.pad(ml_gate_b.reshape(depth, 1, 2 * ML_HEADS), ((0, 0), (0, 0), (0, LANES - 2 * ML_HEADS)))
    w_uq_p = jnp.concatenate([dsa_w_uq[..., :DSA_W],
                              _pad_heads(dsa_w_uq[..., DSA_W:], IDX_HEADS, IDX_DH, IDX_SLOT)],
                             axis=-1).astype(MXU_DTYPE)
    freqs = _rope_freqs()
    lb_cum = jnp.cumsum(jax.nn.softmax(hg_lb_logits.astype(F32), axis=0), axis=0)
    lb = lb_cum - lb_cum[:1]
    lb_tab = jnp.stack([jnp.log(lb), jnp.log1p(-lb), 1.0 - lb], axis=1)

    for layer in range(depth):
        u = _norm_matmul(xs, ln_mix_g[layer], w_in_p, layer, tm=PROJ_TM, tn=PROJ_TN, act=False, out_dtype=F32)
        y_a, w_out_b = _mlstm(u, ml_conv_w[layer], gate_b[layer], ml_norm_g[layer], rows=MIXER_ROWS,
                              cast=(w_out, layer))
        q_r, qi_r, k_r, v_b, ki_r, wi = _dsa_prep(u, pos, dsa_q_norm_g[layer], w_uq_p[layer], freqs, rows=PREP_ROWS)
        bias = _indexer(qi_r, wi, ki_r, topk=topk)
        y_b, w_down_b = _attention(q_r, k_r, v_b, bias, tq=ATTN_TQ, tk=ATTN_TK, cast=(w_down, layer))
        y_c, w_up_b = _hgrn2(u, lb_tab[layer], hg_norm_g[layer], rows=MIXER_ROWS, cast=(w_up, layer))
        xs = _mix_out(y_a, y_b, y_c, w_out_b[None], 0, xs, tm=PROJ_TM, tn=PROJ_TN)
        a = _norm_matmul(xs, ln_mlp_g[layer], w_up_b[None], 0, tm=PROJ_TM, tn=PROJ_TN, act=True, out_dtype=MXU_DTYPE)
        xs = _matmul_res(a, w_down_b[None], 0, xs, tm=DOWN_TM, tn=DOWN_TN, tk=DOWN_TK)
    return _final_norm(xs, ln_final_g, tm=NORM_ROWS).reshape(bsz, t, d)
```

```python
import functools

import jax
import jax.numpy as jnp
import numpy as np
from jax import lax
from jax.experimental import pallas as pl
from jax.experimental.pallas import tpu as pltpu

F32 = jnp.float32
BF16 = jnp.bfloat16
MXU_DTYPE = jnp.bfloat16

D_MODEL = 4096
CHUNK = 64
EPS = 1e-6
ROPE_THETA = 500000.0
ROPE_FRACTION = 4
ML_HEADS = 4
ML_DV = 384
ML_DQK = ML_DV // 2
ML_DQK_PAD = 256
ML_CONV = 4
ML_GATE_CAP = 15.0
DSA_HEADS = 8
DSA_DH = 128
DSA_Q_RANK = 384
IDX_HEADS = 8
IDX_DH = 64
IDX_SLOT = 128
TOPK_MAX = 256
HG_HEADS = 12
HG_DK = 128
HG_DV = 128

ML_QK_W = ML_HEADS * ML_DQK
ML_V_W = ML_HEADS * ML_DV
DSA_W = DSA_HEADS * DSA_DH
HG_K_W = HG_HEADS * HG_DK
HG_V_W = HG_HEADS * HG_DV
D_MIX = ML_V_W + DSA_W + HG_V_W
IN_SPLITS = (ML_QK_W, ML_QK_W, ML_V_W, ML_HEADS, ML_HEADS, ML_V_W,
             DSA_Q_RANK, DSA_W, DSA_W, IDX_DH, IDX_HEADS,
             HG_K_W, HG_K_W, HG_V_W, HG_V_W)

LANES = 128
VMEM_LIMIT = 56 * 1024 * 1024

OFF_ML_Q = 0
OFF_ML_K = 768
OFF_ML_V = 1536
OFF_ML_O = 3072
OFF_HG_Q = 4608
OFF_HG_F = 6144
OFF_HG_I = 7680
OFF_HG_G = 9216
OFF_DSA_CQ = 10752
OFF_SMALL = 11136
OFF_DSA_K = 11264
OFF_DSA_V = 12288
D_IN_PAD = 13312
SMALL_IDX_K = 2 * ML_HEADS
SMALL_IDX_W = SMALL_IDX_K + IDX_DH

PROJ_TM, PROJ_TN = 512, 1024
DOWN_TM, DOWN_TN, DOWN_TK = 1024, 1024, 2048
MIXER_ROWS = 256
PREP_ROWS = 512
ATTN_TQ, ATTN_TK = 512, 1024
NORM_ROWS = 256
PACK_KT = 256

LOG2E = 1.4426950408889634
Q_SCALE = DSA_DH ** -0.5 * LOG2E

INT_MIN = -2 ** 31
KEY_NEG_INF = -2139095041
NEG_BIG = -1e30


def _mxu(a):
    return a.astype(MXU_DTYPE)


def _dot(a, b):
    return jnp.dot(_mxu(a), _mxu(b), preferred_element_type=F32)


def _dot_nt(a, b):
    return lax.dot_general(_mxu(a), _mxu(b), (((1,), (1,)), ((), ())), preferred_element_type=F32)


def _dot_tn(a, b):
    return lax.dot_general(_mxu(a), _mxu(b), (((0,), (0,)), ((), ())), preferred_element_type=F32)


def _dot_f32(a, b):
    return jnp.dot(a, b, precision=lax.Precision.HIGHEST, preferred_element_type=F32)


def _sigmoid(x):
    return 1.0 / (1.0 + jnp.exp(-x))


def _log_sigmoid(x):
    return jnp.minimum(x, 0.0) - jnp.log1p(jnp.exp(-jnp.abs(x)))


def _params(*sem):
    return pltpu.CompilerParams(dimension_semantics=sem, vmem_limit_bytes=VMEM_LIMIT)


BF16_SUBLANES = 16


def _mixer_call(kernel_fn, *, grid, in_specs, out_spec, out_shape, scratch_shapes, name, operands, cast=None):
    params = _params("parallel", "arbitrary")
    if cast is None:
        return pl.pallas_call(kernel_fn, grid=grid, in_specs=in_specs, out_specs=out_spec, out_shape=out_shape,
                              scratch_shapes=scratch_shapes, compiler_params=params, name=name)(*operands)
    w, layer, tick, ticks, every = cast
    params = _params("arbitrary", "arbitrary")
    rows, cols = w.shape[1:]
    blocks = ticks // every
    while rows % blocks or (rows // blocks) % BF16_SUBLANES:
        blocks -= 1
    block = lambda a, b: jnp.minimum(tick(a, b) // every, blocks - 1)
    n_in = len(in_specs)

    def body(*refs):
        w_ref, w_out = refs[n_in], refs[n_in + 2]
        now = tick(pl.program_id(0), pl.program_id(1))

        @pl.when((now % every == 0) & (now // every < blocks))
        def _():
            w_out[...] = w_ref[...].astype(w_out.dtype)

        kernel_fn(*refs[:n_in], refs[n_in + 1], *refs[n_in + 3:])

    return pl.pallas_call(
        body, grid=grid,
        in_specs=in_specs + [pl.BlockSpec((None, rows // blocks, cols), lambda a, b: (layer, block(a, b), 0))],
        out_specs=(out_spec, pl.BlockSpec((rows // blocks, cols), lambda a, b: (block(a, b), 0))),
        out_shape=(out_shape, jax.ShapeDtypeStruct((rows, cols), MXU_DTYPE)),
        scratch_shapes=scratch_shapes, compiler_params=params, name=name)(*operands, w)


def _norm_matmul_kernel(x_ref, g_ref, w_ref, o_ref, h_ref, *, act):
    @pl.when(pl.program_id(1) == 0)
    def _():
        x = x_ref[...]
        ms = jnp.mean(x * x, axis=-1, keepdims=True)
        h_ref[...] = (x * lax.rsqrt(ms + EPS) * g_ref[...]).astype(h_ref.dtype)

    y = jnp.dot(h_ref[...], w_ref[...], preferred_element_type=F32)
    if act:
        y = jnp.square(jnp.maximum(y, 0.0))
    o_ref[...] = y.astype(o_ref.dtype)


def _norm_matmul(x, g, w, layer, *, tm, tn, act, out_dtype):
    m, k = x.shape
    n = w.shape[2]
    return pl.pallas_call(
        functools.partial(_norm_matmul_kernel, act=act),
        grid=(m // tm, n // tn),
        in_specs=[pl.BlockSpec((tm, k), lambda i, j: (i, 0)),
                  pl.BlockSpec((1, k), lambda i, j: (0, 0)),
                  pl.BlockSpec((None, k, tn), lambda i, j: (layer, 0, j))],
        out_specs=pl.BlockSpec((tm, tn), lambda i, j: (i, j)),
        out_shape=jax.ShapeDtypeStruct((m, n), out_dtype),
        scratch_shapes=[pltpu.VMEM((tm, k), MXU_DTYPE)],
        compiler_params=_params("parallel", "arbitrary"),
        name="norm_matmul",
    )(x, g.reshape(1, k), w)


def _matmul_res_kernel(a_ref, w_ref, r_ref, o_ref):
    part = jnp.dot(a_ref[...], w_ref[...], preferred_element_type=F32)

    @pl.when(pl.program_id(2) == 0)
    def _():
        o_ref[...] = r_ref[...] + part

    @pl.when(pl.program_id(2) != 0)
    def _():
        o_ref[...] += part


def _matmul_res(a, w, layer, res, *, tm, tn, tk):
    m, k = a.shape
    n = w.shape[2]
    return pl.pallas_call(
        _matmul_res_kernel,
        grid=(m // tm, n // tn, k // tk),
        in_specs=[pl.BlockSpec((tm, tk), lambda i, j, kk: (i, kk)),
                  pl.BlockSpec((None, tk, tn), lambda i, j, kk: (layer, kk, j)),
                  pl.BlockSpec((tm, tn), lambda i, j, kk: (i, j))],
        out_specs=pl.BlockSpec((tm, tn), lambda i, j, kk: (i, j)),
        out_shape=jax.ShapeDtypeStruct((m, n), F32),
        compiler_params=_params("parallel", "parallel", "arbitrary"),
        name="matmul_res",
    )(a, w, res)


def _mix_out_kernel(ya_ref, yb_ref, yc_ref, w_ref, r_ref, o_ref):
    acc = r_ref[...]
    acc += jnp.dot(ya_ref[...], w_ref[0:ML_V_W, :], preferred_element_type=F32)
    acc += jnp.dot(yb_ref[...], w_ref[ML_V_W:ML_V_W + DSA_W, :], preferred_element_type=F32)
    acc += jnp.dot(yc_ref[...], w_ref[ML_V_W + DSA_W:D_MIX, :], preferred_element_type=F32)
    o_ref[...] = acc


def _mix_out(ya, yb, yc, w, layer, res, *, tm, tn):
    m = ya.shape[0]
    n = w.shape[2]
    return pl.pallas_call(
        _mix_out_kernel,
        grid=(m // tm, n // tn),
        in_specs=[pl.BlockSpec((tm, ML_V_W), lambda i, j: (i, 0)),
                  pl.BlockSpec((tm, DSA_W), lambda i, j: (i, 0)),
                  pl.BlockSpec((tm, HG_V_W), lambda i, j: (i, 0)),
                  pl.BlockSpec((None, D_MIX, tn), lambda i, j: (layer, 0, j)),
                  pl.BlockSpec((tm, tn), lambda i, j: (i, j))],
        out_specs=pl.BlockSpec((tm, tn), lambda i, j: (i, j)),
        out_shape=jax.ShapeDtypeStruct((m, n), F32),
        compiler_params=_params("parallel", "parallel"),
        name="mix_out",
    )(ya, yb, yc, w, res)


def _final_norm_kernel(x_ref, g_ref, o_ref):
    x = x_ref[...]
    ms = jnp.mean(x * x, axis=-1, keepdims=True)
    o_ref[...] = x * lax.rsqrt(ms + EPS) * g_ref[...]


def _final_norm(x, g, *, tm):
    m, k = x.shape
    return pl.pallas_call(
        _final_norm_kernel,
        grid=(m // tm,),
        in_specs=[pl.BlockSpec((tm, k), lambda i: (i, 0)),
                  pl.BlockSpec((1, k), lambda i: (0, 0))],
        out_specs=pl.BlockSpec((tm, k), lambda i: (i, 0)),
        out_shape=jax.ShapeDtypeStruct((m, k), F32),
        compiler_params=_params("parallel"),
        name="final_norm",
    )(x, g.reshape(1, k))


def _mlstm_kernel(q_ref, k_ref, v_ref, o_ref, sm_ref, cw_ref, gb_ref, ng_ref, out_ref,
                  xbuf, c_ref, *, rows):
    @pl.when(pl.program_id(1) == 0)
    def _():
        xbuf[0:8, :] = jnp.zeros((8, 2 * ML_QK_W), F32)
        c_ref[...] = jnp.zeros_like(c_ref)

    xbuf[8:8 + rows, 0:ML_QK_W] = q_ref[...]
    xbuf[8:8 + rows, ML_QK_W:] = k_ref[...]
    cw = cw_ref[...]
    acc = xbuf[8:8 + rows, :] * cw[ML_CONV - 1:ML_CONV, :]
    for j in range(1, ML_CONV):
        acc += xbuf[8 - j:8 - j + rows, :] * cw[ML_CONV - 1 - j:ML_CONV - j, :]
    xbuf[0:8, :] = xbuf[rows:rows + 8, :]
    qk = acc * _sigmoid(acc)
    qk = jnp.concatenate([qk, jnp.zeros((rows, LANES), F32)], axis=1)

    capped = ML_GATE_CAP * jnp.tanh((sm_ref[...] + gb_ref[...]) * (1.0 / ML_GATE_CAP))
    lsig = _log_sigmoid(capped)
    r_i = lax.broadcasted_iota(jnp.int32, (rows, rows), 0)
    c_i = lax.broadcasted_iota(jnp.int32, (rows, rows), 1)
    causal = c_i <= r_i
    b_all = _dot_f32(jnp.where(causal, 1.0, 0.0), lsig)
    capped_t, b_all_t = capped.T, b_all.T
    one_col = jnp.where(lax.broadcasted_iota(jnp.int32, (rows, LANES), 1) == 0, 1.0, 0.0)
    real = lax.broadcasted_iota(jnp.int32, (rows, ML_DQK_PAD), 1) < ML_DQK

    for h in range(ML_HEADS):
        q = jnp.where(real, qk[:, h * ML_DQK:h * ML_DQK + ML_DQK_PAD], 0.0)
        k = jnp.where(real, qk[:, ML_QK_W + h * ML_DQK:ML_QK_W + h * ML_DQK + ML_DQK_PAD], 0.0) * (ML_DQK ** -0.5)
        ig_col, b_col = capped[:, h:h + 1], b_all[:, ML_HEADS + h:ML_HEADS + h + 1]
        ig_row, b_row = capped_t[h:h + 1, :], b_all_t[ML_HEADS + h:ML_HEADS + h + 1, :]
        dv = slice(h * ML_DV, (h + 1) * ML_DV)

        dmat = jnp.exp(jnp.where(causal, b_col - b_row + ig_row, -jnp.inf))
        s = _dot_nt(q, k) * dmat
        v_ext = jnp.concatenate([v_ref[:, dv], one_col], axis=1)
        c_old = c_ref[h]
        num_ext = _dot(s, v_ext) + jnp.exp(b_col) * _dot(q, c_old)
        num = num_ext[:, 0:ML_DV]
        den = num_ext[:, ML_DV:ML_DV + 1]
        hh = num / jnp.maximum(jnp.abs(den), 1.0)

        b_last = b_col[rows - 1:rows, :]
        w_s = jnp.exp(b_last - b_col + ig_col)
        c_ref[h] = jnp.exp(b_last) * c_old + _dot_tn(k, w_s * v_ext)

        ms = jnp.mean(hh * hh, axis=-1, keepdims=True)
        y = hh * lax.rsqrt(ms + EPS) * ng_ref[:, dv]
        out_ref[:, dv] = (_sigmoid(o_ref[:, dv]) * y).astype(out_ref.dtype)


def _mlstm(u, conv_w, gate_b, norm_g, *, rows, cast=None):
    t = u.shape[0]
    return _mixer_call(
        functools.partial(_mlstm_kernel, rows=rows),
        grid=(1, t // rows),
        in_specs=[pl.BlockSpec((rows, ML_QK_W), lambda _, c: (c, OFF_ML_Q // ML_QK_W)),
                  pl.BlockSpec((rows, ML_QK_W), lambda _, c: (c, OFF_ML_K // ML_QK_W)),
                  pl.BlockSpec((rows, ML_V_W), lambda _, c: (c, OFF_ML_V // ML_V_W)),
                  pl.BlockSpec((rows, ML_V_W), lambda _, c: (c, OFF_ML_O // ML_V_W)),
                  pl.BlockSpec((rows, LANES), lambda _, c: (c, OFF_SMALL // LANES)),
                  pl.BlockSpec((ML_CONV, 2 * ML_QK_W), lambda _, c: (0, 0)),
                  pl.BlockSpec((1, LANES), lambda _, c: (0, 0)),
                  pl.BlockSpec((1, ML_V_W), lambda _, c: (0, 0))],
        out_spec=pl.BlockSpec((rows, ML_V_W), lambda _, c: (c, 0)),
        out_shape=jax.ShapeDtypeStruct((t, ML_V_W), MXU_DTYPE),
        scratch_shapes=[pltpu.VMEM((rows + 8, 2 * ML_QK_W), F32),
                        pltpu.VMEM((ML_HEADS, ML_DQK_PAD, ML_DV + LANES), F32)],
        name="mlstm",
        operands=(u, u, u, u, u, conv_w, gate_b, norm_g.reshape(1, ML_V_W)),
        cast=cast and (*cast, lambda _, c: c, t // rows, 1))


SUBLANES = 8
HG_CAST_EVERY = 1
HG_GROUP = 12


def _pair_level_table(rows):
    t = np.arange(rows)[:, None]
    s = np.arange(rows)[None, :]
    x = np.bitwise_xor(t, s)
    lvl = np.floor(np.log2(np.maximum(x, 1))).astype(np.int32)
    return jnp.asarray(np.where(t > s, lvl, -1).astype(np.int32))


def _block_sums(log_f, rows):
    sub = lax.broadcasted_iota(jnp.int32, log_f.shape, 0)
    groups = rows // SUBLANES

    def row_of_group(x, j):
        x3 = x.reshape(groups, SUBLANES, x.shape[-1])
        return jnp.broadcast_to(x3[:, j:j + 1, :], x3.shape).reshape(x.shape)

    c = log_f
    e = jnp.zeros_like(log_f)
    out = [(c, e)]
    odd = (sub & 1) == 1
    c, e = (c + jnp.where(odd, pltpu.roll(c, 1, 0), 0.0),
            e + jnp.where(odd, 0.0, pltpu.roll(c, rows - 1, 0)))
    out.append((c, e))
    r8 = sub & 7
    c, e = (c + jnp.where((r8 == 2) | (r8 == 3), row_of_group(c, 1),
                          jnp.where((r8 == 6) | (r8 == 7), row_of_group(c, 5), 0.0)),
            e + jnp.where((r8 == 0) | (r8 == 1), row_of_group(c, 3),
                          jnp.where((r8 == 4) | (r8 == 5), row_of_group(c, 7), 0.0)))
    out.append((c, e))
    c, e = (c + jnp.where(r8 >= 4, row_of_group(c, 3), 0.0),
            e + jnp.where(r8 < 4, row_of_group(c, 7), 0.0))
    out.append((c, e))
    m = SUBLANES
    while m < rows:
        cs, es = [], []
        for p in range(rows // (2 * m)):
            lo, mid, hi = 2 * m * p, 2 * m * p + m, 2 * m * (p + 1)
            cs += [c[lo:mid], c[mid:hi] + c[mid - 1:mid]]
            es += [e[lo:mid] + c[hi - 1:hi], e[mid:hi]]
        c, e = jnp.concatenate(cs, axis=0), jnp.concatenate(es, axis=0)
        out.append((c, e))
        m *= 2
    return out


def _hgrn2_kernel(q_ref, f_ref, i_ref, g_ref, lb_ref, ng_ref, lvl_ref, out_ref, st_ref, *, rows):
    @pl.when(pl.program_id(1) == 0)
    def _():
        st_ref[...] = jnp.zeros_like(st_ref)

    for h in range(HG_GROUP):
        dk = slice(h * HG_DK, (h + 1) * HG_DK)
        out, st_ref[h] = _hgrn2_head(q_ref[:, dk], f_ref[:, dk], i_ref[:, dk], g_ref[:, dk], lb_ref[:, dk],
                                     ng_ref[:, dk], lvl_ref[...], st_ref[h], rows)
        out_ref[:, dk] = out.astype(out_ref.dtype)


def _hgrn2_head(qp, fp, v, gp, lb, ng, lvl, st, rows):
    q = qp * _sigmoid(qp)
    log_lb = lb[0:1, :]
    x2 = lb[1:2, :] + _log_sigmoid(fp)
    log_f = jnp.maximum(log_lb, x2) + jnp.log1p(jnp.exp(-jnp.abs(log_lb - x2)))
    k = lb[2:3, :] * _sigmoid(-fp)

    sums = _block_sums(log_f * LOG2E, rows)
    tiles = rows // LANES
    tile_rows = [slice(r * LANES, (r + 1) * LANES) for r in range(tiles)]
    diag = [jnp.zeros((LANES, LANES), F32) for _ in range(tiles)]
    o_tiles = [None] * tiles
    for level, (c_m, e_m) in enumerate(sums[:-1]):
        m = 1 << level
        qh = q * jnp.exp2(c_m)
        kh = k * jnp.exp2(e_m)
        if m < LANES:
            for r, sl in enumerate(tile_rows):
                diag[r] = jnp.where(lvl == level, _dot_nt(qh[sl], kh[sl]), diag[r])
        else:
            for p in range(rows // (2 * m)):
                lo, mid = 2 * m * p, 2 * m * p + m
                for r in range(mid // LANES, (mid + m) // LANES):
                    part = _dot(_dot_nt(qh[tile_rows[r]], kh[lo:mid]), v[lo:mid])
                    o_tiles[r] = part if o_tiles[r] is None else o_tiles[r] + part
    for r, sl in enumerate(tile_rows):
        part = _dot(diag[r], v[sl])
        o_tiles[r] = part if o_tiles[r] is None else o_tiles[r] + part
    b, after = sums[-1]

    o = (jnp.concatenate(o_tiles, axis=0) + jnp.sum(q * k, axis=-1, keepdims=True) * v
         + _dot_nt(q * jnp.exp2(b), st))
    st_new = st * jnp.exp2(b[rows - 1:rows, :]) + _dot_tn(v, k * jnp.exp2(after))

    ms = jnp.mean(o * o, axis=-1, keepdims=True)
    y = o * lax.rsqrt(ms + EPS) * ng
    return y * (gp * _sigmoid(gp)), st_new


def _hgrn2(u, lb_tab, norm_g, *, rows, cast=None):
    t = u.shape[0]
    width = HG_GROUP * HG_DK
    groups = HG_HEADS // HG_GROUP
    qb, fb, ib, gb = OFF_HG_Q // width, OFF_HG_F // width, OFF_HG_I // width, OFF_HG_G // width
    return _mixer_call(
        functools.partial(_hgrn2_kernel, rows=rows),
        grid=(groups, t // rows),
        in_specs=[pl.BlockSpec((rows, width), lambda h, c: (c, qb + h)),
                  pl.BlockSpec((rows, width), lambda h, c: (c, fb + h)),
                  pl.BlockSpec((rows, width), lambda h, c: (c, ib + h)),
                  pl.BlockSpec((rows, width), lambda h, c: (c, gb + h)),
                  pl.BlockSpec((3, width), lambda h, c: (0, h)),
                  pl.BlockSpec((1, width), lambda h, c: (0, h)),
                  pl.BlockSpec((LANES, LANES), lambda h, c: (0, 0))],
        out_spec=pl.BlockSpec((rows, width), lambda h, c: (c, h)),
        out_shape=jax.ShapeDtypeStruct((t, HG_V_W), MXU_DTYPE),
        scratch_shapes=[pltpu.VMEM((HG_GROUP, HG_DV, HG_DK), F32)],
        name="hgrn2",
        operands=(u, u, u, u, lb_tab, norm_g.reshape(1, HG_V_W), _pair_level_table(LANES)),
        cast=cast and (*cast, lambda h, c: h * (t // rows) + c, groups * (t // rows), HG_CAST_EVERY))


def _rope_slab(u, cc, sa, sb, half):
    return u * cc + pltpu.roll(u, LANES - half, 1) * sa + pltpu.roll(u, half, 1) * sb


def _rope_coeffs(ang, half):
    lane = lax.broadcasted_iota(jnp.int32, ang.shape, 1)
    cos, sin = jnp.cos(ang), jnp.sin(ang)
    cc = jnp.where(lane < 2 * half, cos, 1.0)
    sa = jnp.where(lane < half, -sin, 0.0)
    sb = jnp.where((lane >= half) & (lane < 2 * half), sin, 0.0)
    return cc, sa, sb


def _dsa_prep_kernel(cq_ref, k_ref, v_ref, idx_ref, pos_ref, g_ref, w_ref, fr_ref,
                     q_out, qi_out, k_out, v_out, ki_out, wi_out):
    pos = pos_ref[...].astype(F32)
    cq = cq_ref[...]
    ms = jnp.mean(cq * cq, axis=-1, keepdims=True)
    hq = cq * lax.rsqrt(ms + EPS) * g_ref[...]
    q_all = _dot(hq, w_ref[...])

    half_a = DSA_DH // ROPE_FRACTION // 2
    half_i = IDX_DH // ROPE_FRACTION // 2
    ca = _rope_coeffs(pos * fr_ref[0:1, :], half_a)
    ci = _rope_coeffs(pos * fr_ref[1:2, :], half_i)

    kk = k_ref[...]
    for h in range(DSA_HEADS):
        sl = slice(h * DSA_DH, (h + 1) * DSA_DH)
        q_out[:, sl] = (_rope_slab(q_all[:, sl], *ca, half_a) * Q_SCALE).astype(q_out.dtype)
        k_out[:, sl] = _rope_slab(kk[:, sl], *ca, half_a).astype(k_out.dtype)
    for h in range(IDX_HEADS):
        src = slice(DSA_W + h * IDX_SLOT, DSA_W + (h + 1) * IDX_SLOT)
        qi_out[h] = _rope_slab(q_all[:, src], *ci, half_i).astype(qi_out.dtype)
    v_out[...] = v_ref[...].astype(v_out.dtype)

    small = idx_ref[...]
    lane = lax.broadcasted_iota(jnp.int32, small.shape, 1)
    ki = jnp.where(lane < IDX_DH, pltpu.roll(small, LANES - SMALL_IDX_K, 1), 0.0)
    ki_out[...] = _rope_slab(ki, *ci, half_i).astype(ki_out.dtype)
    wi = pltpu.roll(small, LANES - SMALL_IDX_W, 1) * (IDX_HEADS ** -0.5 * IDX_DH ** -0.5)
    wi_out[...] = jnp.where(lane < IDX_HEADS, wi, 0.0)


def _dsa_prep(u, pos, q_norm_g, w_uq_pad, freqs, *, rows):
    t = u.shape[0]
    nq = DSA_W + IDX_HEADS * IDX_SLOT
    outs = (jax.ShapeDtypeStruct((t, DSA_W), MXU_DTYPE),
            jax.ShapeDtypeStruct((IDX_HEADS, t, IDX_SLOT), MXU_DTYPE),
            jax.ShapeDtypeStruct((t, DSA_W), MXU_DTYPE),
            jax.ShapeDtypeStruct((t, DSA_W), MXU_DTYPE),
            jax.ShapeDtypeStruct((t, IDX_SLOT), MXU_DTYPE),
            jax.ShapeDtypeStruct((t, LANES), F32))
    row_spec = lambda w, blk: pl.BlockSpec((rows, w), lambda i: (i, blk))
    return pl.pallas_call(
        _dsa_prep_kernel,
        grid=(t // rows,),
        in_specs=[row_spec(DSA_Q_RANK, OFF_DSA_CQ // DSA_Q_RANK),
                  row_spec(DSA_W, OFF_DSA_K // DSA_W),
                  row_spec(DSA_W, OFF_DSA_V // DSA_W),
                  row_spec(LANES, OFF_SMALL // LANES),
                  pl.BlockSpec((rows, 1), lambda i: (i, 0)),
                  pl.BlockSpec((1, DSA_Q_RANK), lambda i: (0, 0)),
                  pl.BlockSpec((DSA_Q_RANK, nq), lambda i: (0, 0)),
                  pl.BlockSpec((2, LANES), lambda i: (0, 0))],
        out_specs=(row_spec(DSA_W, 0),
                   pl.BlockSpec((IDX_HEADS, rows, IDX_SLOT), lambda i: (0, i, 0)),
                   row_spec(DSA_W, 0), row_spec(DSA_W, 0), row_spec(IDX_SLOT, 0), row_spec(LANES, 0)),
        out_shape=outs,
        compiler_params=_params("parallel"),
        name="dsa_prep",
    )(u, u, u, u, pos, q_norm_g.reshape(1, DSA_Q_RANK), w_uq_pad, freqs)


IDX_TQ = 512
IDX_TK = 512
IDX_SLAB = 128


def _indexer_kernel(qi_ref, wi_ref, ki_ref, incl_ref, bias_ref, key_ref, *, topk):
    tq, tk = IDX_TQ, IDX_TK
    qb = pl.program_id(0)
    nkb = ((qb + 1) * tq + tk - 1) // tk
    q_all = qi_ref[...].reshape(IDX_HEADS * tq, IDX_SLOT)
    w = wi_ref[...]
    w_col = jnp.concatenate([w[:, h:h + 1] for h in range(IDX_HEADS)], axis=0)
    row_chunk = (qb * tq + lax.broadcasted_iota(jnp.int32, (tq, 1), 0)) // CHUNK
    col_in_blk = lax.broadcasted_iota(jnp.int32, (1, tk), 1)

    def score_body(kb, carry):
        kt = ki_ref[pl.ds(pl.multiple_of(kb * tk, tk), tk), :]
        weighted = jnp.maximum(_dot_nt(q_all, kt), 0.0) * w_col
        parts = [weighted[h * tq:(h + 1) * tq] for h in range(IDX_HEADS)]
        while len(parts) > 1:
            parts = [a + b for a, b in zip(parts[0::2], parts[1::2])]
        col_chunk = (kb * tk + col_in_blk) // CHUNK
        score = jnp.where(col_chunk <= row_chunk, parts[0], -jnp.inf)
        bits = pltpu.bitcast(score, jnp.int32)
        bits = jnp.where(bits == INT_MIN, 0, bits)
        key_ref[kb] = jnp.where(bits < 0, bits ^ 0x7FFFFFFF, bits)
        return carry

    lax.fori_loop(0, nkb, score_body, 0)

    ones_mat = jnp.ones((LANES, LANES), MXU_DTYPE)

    slabs = [slice(r0, r0 + IDX_SLAB) for r0 in range(0, tq, IDX_SLAB)]

    def slab_counts(cand_r, rows):
        def one_block(kb, cnt):
            for j in range(tk // LANES):
                cnt += jnp.where(key_ref[kb, rows, j * LANES:(j + 1) * LANES] >= cand_r, 1, 0)
            return cnt

        cnt = lax.fori_loop(0, nkb // 2, lambda p, c: one_block(2 * p + 1, one_block(2 * p, c)),
                            jnp.zeros((IDX_SLAB, LANES), jnp.int32))
        return lax.cond(nkb % 2 == 1, lambda c: one_block(nkb - 1, c), lambda c: c, cnt)

    def row_totals(cnt):
        return jnp.dot(cnt.astype(F32).astype(MXU_DTYPE), ones_mat, preferred_element_type=F32)

    def count_ge(cand):
        return row_totals(jnp.concatenate([slab_counts(cand[rows], rows) for rows in slabs], axis=0))

    zero = jnp.zeros((tq, LANES), jnp.int32)
    cnt0 = count_ge(zero)
    thr_rep = jnp.where(cnt0 >= topk, zero, INT_MIN)
    at_thr = jnp.where(cnt0 >= topk, cnt0, (nkb * tk).astype(F32))

    def bit_body(i, carry):
        thr_rep, at_thr = carry
        cand = thr_rep + jnp.left_shift(jnp.int32(1), 30 - i)
        cnt = count_ge(cand)
        ok = cnt >= topk
        return jnp.where(ok, cand, thr_rep), jnp.where(ok, cnt, at_thr)

    thr_rep, at_thr = lax.fori_loop(0, 31, bit_body, (thr_rep, at_thr))

    bias_ref[...] = jnp.full(bias_ref.shape, NEG_BIG, bias_ref.dtype)

    thr = thr_rep[:, 0:1]
    exact = jnp.all(at_thr == topk)

    def store(kb, take):
        cols = pl.ds(pl.multiple_of(kb * tk, tk), tk)
        bias_ref[:, cols] = jnp.where(take, 0.0, NEG_BIG).astype(bias_ref.dtype)

    @pl.when(exact)
    def _():
        def emit_body(kb, carry):
            key = key_ref[kb]
            store(kb, (key >= thr) & (key > KEY_NEG_INF))
            return carry

        lax.fori_loop(0, nkb, emit_body, 0)

    @pl.when(jnp.logical_not(exact))
    def _():
        need = topk - count_ge(thr_rep + 1)[:, 0:1]

        def emit_body(kb, seen):
            key = key_ref[kb]
            eq = key == thr
            eq_f = jnp.where(eq, 1.0, 0.0)
            rank = seen + jnp.dot(eq_f.astype(MXU_DTYPE), incl_ref[...], preferred_element_type=F32)
            store(kb, ((key > thr) | (eq & (rank <= need))) & (key > KEY_NEG_INF))
            return seen + jnp.sum(eq_f, axis=1, keepdims=True)

        lax.fori_loop(0, nkb, emit_body, jnp.zeros((tq, 1), F32))


def _indexer(qi, wi, ki, *, topk):
    t = ki.shape[0]
    incl = jnp.asarray(np.triu(np.ones((IDX_TK, IDX_TK), np.float32)), MXU_DTYPE)
    return pl.pallas_call(
        functools.partial(_indexer_kernel, topk=topk),
        grid=(t // IDX_TQ,),
        in_specs=[pl.BlockSpec((IDX_HEADS, IDX_TQ, IDX_SLOT), lambda i: (0, i, 0)),
                  pl.BlockSpec((IDX_TQ, LANES), lambda i: (i, 0)),
                  pl.BlockSpec((t, IDX_SLOT), lambda i: (0, 0)),
                  pl.BlockSpec((IDX_TK, IDX_TK), lambda i: (0, 0))],
        out_specs=pl.BlockSpec((IDX_TQ, t), lambda i: (i, 0)),
        out_shape=jax.ShapeDtypeStruct((t, t), BF16),
        scratch_shapes=[pltpu.VMEM((t // IDX_TK, IDX_TQ, IDX_TK), jnp.int32)],
        compiler_params=_params("parallel"),
        name="dsa_indexer",
    )(qi, wi, ki, incl)


def _attn_kernel(q_ref, k_ref, v_ref, bias_ref, o_ref, m_ref, l_ref, acc_ref, *, tq, tk):
    qb, kb = pl.program_id(0), pl.program_id(1)
    last = ((qb + 1) * tq - 1) // tk

    @pl.when(kb == 0)
    def _():
        m_ref[...] = jnp.full(m_ref.shape, NEG_BIG, F32)
        l_ref[...] = jnp.zeros_like(l_ref)
        acc_ref[...] = jnp.zeros_like(acc_ref)

    @pl.when(kb <= last)
    def _():
        ones = jnp.ones((tk, LANES), v_ref.dtype)
        heads = [slice(h * DSA_DH, (h + 1) * DSA_DH) for h in range(DSA_HEADS)]
        qk = _dot_nt(q_ref[:, heads[0]], k_ref[:, heads[0]])
        for h, sl in enumerate(heads):
            s = qk + bias_ref[...].astype(F32)
            if h + 1 < DSA_HEADS:
                qk = _dot_nt(q_ref[:, heads[h + 1]], k_ref[:, heads[h + 1]])
            m_old = m_ref[h]
            m_new = jnp.maximum(m_old, jnp.max(s, axis=-1, keepdims=True))
            alpha = jnp.exp2(m_old - m_new)
            p = jnp.exp2(s - jnp.concatenate([m_new] * (tk // LANES), axis=1))
            pv = _dot(p, jnp.concatenate([v_ref[:, sl], ones], axis=1))
            l_ref[h] = alpha * l_ref[h] + pv[:, DSA_DH:]
            acc_ref[:, sl] = alpha * acc_ref[:, sl] + pv[:, :DSA_DH]
            m_ref[h] = m_new

    @pl.when(kb == last)
    def _():
        for h in range(DSA_HEADS):
            sl = slice(h * DSA_DH, (h + 1) * DSA_DH)
            o_ref[:, sl] = (acc_ref[:, sl] / l_ref[h]).astype(o_ref.dtype)


def _attention(q, k, v, bias, *, tq, tk, cast=None):
    t = q.shape[0]
    last = lambda i: ((i + 1) * tq - 1) // tk
    ratio = tk // tq
    assert tk == ratio * tq

    def before(i):
        m = i // ratio
        return i + ratio * (m * (m - 1) // 2) + (i - ratio * m) * m

    tick = lambda i, j: before(i) + jnp.minimum(j, last(i))
    return _mixer_call(
        functools.partial(_attn_kernel, tq=tq, tk=tk),
        grid=(t // tq, t // tk),
        in_specs=[pl.BlockSpec((tq, DSA_W), lambda i, j: (i, 0)),
                  pl.BlockSpec((tk, DSA_W), lambda i, j: (jnp.minimum(j, last(i)), 0)),
                  pl.BlockSpec((tk, DSA_W), lambda i, j: (jnp.minimum(j, last(i)), 0)),
                  pl.BlockSpec((tq, tk), lambda i, j: (i, jnp.minimum(j, last(i))))],
        out_spec=pl.BlockSpec((tq, DSA_W), lambda i, j: (i, 0)),
        out_shape=jax.ShapeDtypeStruct((t, DSA_W), MXU_DTYPE),
        scratch_shapes=[pltpu.VMEM((DSA_HEADS, tq, LANES), F32),
                        pltpu.VMEM((DSA_HEADS, tq, LANES), F32),
                        pltpu.VMEM((tq, DSA_W), F32)],
        name="dsa_attention",
        operands=(q, k, v, bias),
        cast=cast and (*cast, tick, before(t // tq), 1))


def _pad_heads(w, heads, width, padded):
    lead = w.shape[:-1]
    w = w.reshape(lead + (heads, width))
    w = jnp.pad(w, [(0, 0)] * len(lead) + [(0, 0), (0, padded - width)])
    return w.reshape(lead + (heads * padded,))


def _pack_plan():
    src, start = {}, 0
    names = ("ml_q", "ml_k", "ml_v", "ml_i", "ml_f", "ml_o", "dsa_cq", "dsa_k", "dsa_v", "idx_k", "idx_w",
             "hg_q", "hg_f", "hg_i", "hg_g")
    for name, width in zip(names, IN_SPLITS):
        src[name] = start
        start += width
    copies = [(OFF_ML_Q, src["ml_q"], ML_QK_W), (OFF_ML_K, src["ml_k"], ML_QK_W),
              (OFF_ML_V, src["ml_v"], ML_V_W), (OFF_ML_O, src["ml_o"], ML_V_W),
              (OFF_HG_Q, src["hg_q"], HG_K_W), (OFF_HG_F, src["hg_f"], HG_K_W),
              (OFF_HG_I, src["hg_i"], HG_V_W), (OFF_HG_G, src["hg_g"], HG_V_W),
              (OFF_DSA_CQ, src["dsa_cq"], DSA_Q_RANK), (OFF_DSA_K, src["dsa_k"], DSA_W), (OFF_DSA_V, src["dsa_v"], DSA_W)]
    small = [(src["ml_i"], 2 * ML_HEADS), (src["idx_k"], IDX_DH + IDX_HEADS)]
    return copies, small


def _pack_kernel(wt_ref, o_ref):
    kt = o_ref.shape[0]
    copies, small = _pack_plan()
    for dst, src, width in copies:
        o_ref[:, dst:dst + width] = wt_ref[src:src + width, :].T.astype(o_ref.dtype)
    rows = [wt_ref[src:src + width, :] for src, width in small]
    used = sum(width for _, width in small)
    rows.append(jnp.zeros((LANES - used, kt), wt_ref.dtype))
    o_ref[:, OFF_SMALL:OFF_SMALL + LANES] = jnp.concatenate(rows, axis=0).T.astype(o_ref.dtype)


def _pack_w_in(w_in, *, kt):
    depth, d, n = w_in.shape
    wt = jnp.swapaxes(w_in, 1, 2)
    return pl.pallas_call(
        _pack_kernel,
        grid=(depth, d // kt),
        in_specs=[pl.BlockSpec((None, n, kt), lambda l, i: (l, 0, i))],
        out_specs=pl.BlockSpec((None, kt, D_IN_PAD), lambda l, i: (l, i, 0)),
        out_shape=jax.ShapeDtypeStruct((depth, d, D_IN_PAD), MXU_DTYPE),
        compiler_params=_params("parallel", "parallel"),
        name="pack_w_in",
    )(wt)


def _rope_freqs():
    def lanes(d):
        rot = d // ROPE_FRACTION
        half = rot // 2
        inv = jnp.power(ROPE_THETA, -jnp.arange(half, dtype=F32) * (2.0 / rot))
        return jnp.concatenate([inv, inv, jnp.zeros((LANES - rot,), F32)])
    return jnp.stack([lanes(DSA_DH), lanes(IDX_DH)])


def kernel(x, positions, ln_mix_g, w_in, ml_conv_w, ml_gate_b, ml_norm_g, dsa_q_norm_g, dsa_w_uq,
           hg_lb_logits, hg_norm_g, w_out, ln_mlp_g, w_up, w_down, ln_final_g):
    bsz, t, d = x.shape
    assert bsz == 1 and t % max(DOWN_TM, ATTN_TK, IDX_TQ) == 0 and d == D_MODEL
    depth = w_in.shape[0]
    topk = min(TOPK_MAX, t // 4)
    xs = x.reshape(t, d)
    pos = positions.reshape(t, 1)

    w_in_p = _pack_w_in(w_in, kt=PACK_KT)
    gate_b = jnp.pad(ml_gate_b.reshape(depth, 1, 2 * ML_HEADS), ((0, 0), (0, 0), (0, LANES - 2 * ML_HEADS)))
    w_uq_p = jnp.concatenate([dsa_w_uq[..., :DSA_W],
                              _pad_heads(dsa_w_uq[..., DSA_W:], IDX_HEADS, IDX_DH, IDX_SLOT)],
                             axis=-1).astype(MXU_DTYPE)
    freqs = _rope_freqs()
    lb_cum = jnp.cumsum(jax.nn.softmax(hg_lb_logits.astype(F32), axis=0), axis=0)
    lb = lb_cum - lb_cum[:1]
    lb_tab = jnp.stack([jnp.log(lb), jnp.log1p(-lb), 1.0 - lb], axis=1)

    for layer in range(depth):
        u = _norm_matmul(xs, ln_mix_g[layer], w_in_p, layer, tm=PROJ_TM, tn=PROJ_TN, act=False, out_dtype=F32)
        y_a, w_out_b = _mlstm(u, ml_conv_w[layer], gate_b[layer], ml_norm_g[layer], rows=MIXER_ROWS,
                              cast=(w_out, layer))
        q_r, qi_r, k_r, v_b, ki_r, wi = _dsa_prep(u, pos, dsa_q_norm_g[layer], w_uq_p[layer], freqs, rows=PREP_ROWS)
        bias = _indexer(qi_r, wi, ki_r, topk=topk)
        y_b, w_down_b = _attention(q_r, k_r, v_b, bias, tq=ATTN_TQ, tk=ATTN_TK, cast=(w_down, layer))
        y_c, w_up_b = _hgrn2(u, lb_tab[layer], hg_norm_g[layer], rows=MIXER_ROWS, cast=(w_up, layer))
        xs = _mix_out(y_a, y_b, y_c, w_out_b[None], 0, xs, tm=PROJ_TM, tn=PROJ_TN)
        a = _norm_matmul(xs, ln_mlp_g[layer], w_up_b[None], 0, tm=PROJ_TM, tn=PROJ_TN, act=True, out_dtype=MXU_DTYPE)
        xs = _matmul_res(a, w_down_b[None], 0, xs, tm=DOWN_TM, tn=DOWN_TN, tk=DOWN_TK)
    return _final_norm(xs, ln_final_g, tm=NORM_ROWS).reshape(bsz, t, d)
```
